```python
import math
import jax, jax.numpy as jnp
from jax import lax
import numpy as np

D_MODEL = 1024
BATCH = 4
SEQ = 8192
DEPTH = 4

N_MIXERS = 4
D_FF = 2816
NORM_EPS = 1e-6
N_SUBLAYERS = 3
N_ADA = 3 * N_SUBLAYERS
POOL_WINDOWS = (2, 4, 8, 16)
POOL_GROUP = D_MODEL // len(POOL_WINDOWS)
POOL_MAX_W = max(POOL_WINDOWS)
FOX_HEADS = 16
FOX_HEAD_DIM = D_MODEL // FOX_HEADS
FOX_BLOCK = 128
S5_GROUP = 16
S5_GROUPS = D_MODEL // S5_GROUP
S5_STATE = 64
S5_DT_MIN = 1e-3
S5_DT_MAX = 1e-1
CONV_WIDTH = 3
LAYERS_PER_MIXER = tuple(len(range(m, DEPTH, N_MIXERS)) for m in range(N_MIXERS))

kernel_name = "hybrid_pool_fox_s5_conv_macaron"


def rms_norm(x, gain):
    xf = x.astype(jnp.float32)
    y = xf * lax.rsqrt(jnp.mean(xf * xf, axis=-1, keepdims=True) + NORM_EPS)
    return (y * gain.astype(jnp.float32)).astype(x.dtype)


def adaln(x, gain, shift, scale):
    return rms_norm(x, gain) * (1.0 + scale[:, None, :]) + shift[:, None, :]


def swiglu(h, w_in, w_out):
    g, u = jnp.split(h @ w_in, 2, axis=-1)
    return (jax.nn.silu(g) * u) @ w_out


def pool_mixer(h, w_grp, scale):
    b_, s_, d_ = h.shape
    hf = h.astype(jnp.float32)
    cs = jnp.cumsum(hf, axis=1)
    cs_pad = jnp.pad(cs, ((0, 0), (POOL_MAX_W, 0), (0, 0)))
    pos = jnp.arange(s_)
    outs = []
    for gi, w in enumerate(POOL_WINDOWS):
        sl = slice(gi * POOL_GROUP, (gi + 1) * POOL_GROUP)
        prev = cs_pad[:, POOL_MAX_W - w:POOL_MAX_W - w + s_, sl]
        cnt = jnp.minimum(pos + 1, w).astype(jnp.float32)[None, :, None]
        outs.append((cs[..., sl] - prev) / cnt - hf[..., sl])
    pooled = jnp.stack(outs, axis=2)
    mixed = jnp.einsum('bsgc,gcd->bsgd', pooled, w_grp.astype(jnp.float32))
    return (mixed.reshape(b_, s_, d_) * scale.astype(jnp.float32)).astype(h.dtype)


def fox_attention(h, w_in, b_f, q_gain, k_gain, w_o):
    b_, s_, d_ = h.shape
    proj = h @ w_in
    q, k, v, f_logit = jnp.split(proj, [d_, 2 * d_, 3 * d_], axis=-1)
    q = rms_norm(q.reshape(b_, s_, FOX_HEADS, FOX_HEAD_DIM), q_gain) * (FOX_HEAD_DIM ** -0.5)
    k = rms_norm(k.reshape(b_, s_, FOX_HEADS, FOX_HEAD_DIM), k_gain)
    v = v.reshape(b_, s_, FOX_HEADS, FOX_HEAD_DIM)
    q, k, v = (t.transpose(0, 2, 1, 3) for t in (q, k, v))
    log_f = jax.nn.log_sigmoid((f_logit + b_f).astype(jnp.float32))
    cum_f = jnp.cumsum(log_f, axis=1).transpose(0, 2, 1)
    q_idx = jnp.arange(FOX_BLOCK)
    outs = []
    for blk in range(s_ // FOX_BLOCK):
        q0 = blk * FOX_BLOCK
        kv_end = q0 + FOX_BLOCK
        s = jnp.einsum('bhqd,bhkd->bhqk', q[:, :, q0:kv_end], k[:, :, :kv_end]).astype(jnp.float32)
        s = s + cum_f[:, :, q0:kv_end, None] - cum_f[:, :, None, :kv_end]
        mask = (q0 + q_idx)[:, None] >= jnp.arange(kv_end)[None, :]
        p = jax.nn.softmax(jnp.where(mask, s, -jnp.inf), axis=-1)
        outs.append(jnp.einsum('bhqk,bhkd->bhqd', p.astype(v.dtype), v[:, :, :kv_end]))
    o = jnp.concatenate(outs, axis=2).transpose(0, 2, 1, 3).reshape(b_, s_, d_)
    return o @ w_o


def s5_mixer(h, lam_re, lam_im, log_dt, b_re, b_im, c_re, c_im, d_skip, w_glu):
    f32 = jnp.float32
    b_, s_, d_ = h.shape
    u = h.astype(f32).reshape(b_, s_, S5_GROUPS, S5_GROUP)
    dt = jnp.exp(log_dt.astype(f32))[:, None]
    ar, ai = lam_re.astype(f32), lam_im.astype(f32)
    mag = jnp.exp(ar * dt)
    lb_re, lb_im = mag * jnp.cos(ai * dt), mag * jnp.sin(ai * dt)
    den = ar * ar + ai * ai
    nr, ni = lb_re - 1.0, lb_im
    k_re = (nr * ar + ni * ai) / den
    k_im = (ni * ar - nr * ai) / den
    br, bi = b_re.astype(f32), b_im.astype(f32)
    bb_re = k_re[..., None] * br - k_im[..., None] * bi
    bb_im = k_re[..., None] * bi + k_im[..., None] * br
    x_re = jnp.einsum('bsgi,gni->sbgn', u, bb_re)
    x_im = jnp.einsum('bsgi,gni->sbgn', u, bb_im)
    a_re = jnp.broadcast_to(lb_re, (s_, 1) + lb_re.shape)
    a_im = jnp.broadcast_to(lb_im, (s_, 1) + lb_im.shape)

    def combine(e1, e2):
        a1r, a1i, b1r, b1i = e1
        a2r, a2i, b2r, b2i = e2
        return (a1r * a2r - a1i * a2i, a1r * a2i + a1i * a2r,
                a2r * b1r - a2i * b1i + b2r, a2r * b1i + a2i * b1r + b2i)

    _, _, st_re, st_im = lax.associative_scan(combine, (a_re, a_im, x_re, x_im), axis=0)
    y = (jnp.einsum('sbgn,gin->bsgi', st_re, c_re.astype(f32))
         - jnp.einsum('sbgn,gin->bsgi', st_im, c_im.astype(f32)))
    y = y + d_skip.astype(f32).reshape(S5_GROUPS, S5_GROUP) * u
    g = jax.nn.gelu(y.reshape(b_, s_, d_).astype(h.dtype))
    return g * jax.nn.sigmoid(g @ w_glu)


def short_conv_mixer(h, w_in, conv_w, w_out):
    b_gate, c_gate, z = jnp.split(h @ w_in, 3, axis=-1)
    conv = lax.conv_general_dilated(c_gate * z, conv_w, window_strides=(1,),
                                    padding=((CONV_WIDTH - 1, 0),),
                                    dimension_numbers=('NWC', 'WIO', 'NWC'),
                                    feature_group_count=D_MODEL)
    return (b_gate * conv) @ w_out


def setup_inputs(seed: int = 0) -> dict:
    key = jax.random.key(seed)
    ks = iter(jax.random.split(key, 32))
    f32 = jnp.float32
    D = D_MODEL
    n_a, n_b, n_c, n_d = LAYERS_PER_MIXER

    def nrm(shape, std):
        return jax.random.normal(next(ks), shape, f32) * std

    x = nrm((BATCH, SEQ, D), 1.0)
    c = nrm((BATCH, D), 1.0)
    ada_w = nrm((DEPTH, D, N_ADA * D), 0.1 * D ** -0.5)
    ada_b = nrm((DEPTH, N_ADA * D), 0.01)
    norm_g = 1.0 + nrm((DEPTH, N_SUBLAYERS, D), 0.01)
    ffn_w_in = nrm((DEPTH, 2, D, 2 * D_FF), D ** -0.5)
    ffn_w_out = nrm((DEPTH, 2, D_FF, D), D_FF ** -0.5)
    pool_w = nrm((n_a, len(POOL_WINDOWS), POOL_GROUP, POOL_GROUP), POOL_GROUP ** -0.5)
    pool_scale = 1.0 + nrm((n_a, D), 0.02)
    fox_w_in = nrm((n_b, D, 3 * D + FOX_HEADS), D ** -0.5)
    fox_b_f = 2.0 + nrm((n_b, FOX_HEADS), 0.5)
    fox_q_gain = 1.0 + nrm((n_b, FOX_HEAD_DIM), 0.01)
    fox_k_gain = 1.0 + nrm((n_b, FOX_HEAD_DIM), 0.01)
    fox_w_o = nrm((n_b, D, D), D ** -0.5)
    n_idx = jnp.arange(S5_STATE, dtype=f32)
    s5_lam_re = -0.5 + nrm((n_c, S5_GROUPS, S5_STATE), 0.01)
    s5_lam_im = math.pi * n_idx + nrm((n_c, S5_GROUPS, S5_STATE), 0.01)
    s5_log_dt = jax.random.uniform(next(ks), (n_c, S5_GROUPS), f32,
                                   math.log(S5_DT_MIN), math.log(S5_DT_MAX))
    s5_b_re = nrm((n_c, S5_GROUPS, S5_STATE, S5_GROUP), (2 * S5_GROUP) ** -0.5)
    s5_b_im = nrm((n_c, S5_GROUPS, S5_STATE, S5_GROUP), (2 * S5_GROUP) ** -0.5)
    s5_c_re = nrm((n_c, S5_GROUPS, S5_GROUP, S5_STATE), 2.0 * S5_STATE ** -0.5)
    s5_c_im = nrm((n_c, S5_GROUPS, S5_GROUP, S5_STATE), 2.0 * S5_STATE ** -0.5)
    s5_d = nrm((n_c, D), 0.5)
    s5_w_glu = nrm((n_c, D, D), D ** -0.5)
    conv_w_in = nrm((n_d, D, 3 * D), D ** -0.5)
    conv_w = nrm((n_d, CONV_WIDTH, 1, D), CONV_WIDTH ** -0.5)
    conv_w_out = nrm((n_d, D, D), D ** -0.5)
    return {"x": x, "c": c, "ada_w": ada_w, "ada_b": ada_b, "norm_g": norm_g,
            "ffn_w_in": ffn_w_in, "ffn_w_out": ffn_w_out,
            "pool_w": pool_w, "pool_scale": pool_scale,
            "fox_w_in": fox_w_in, "fox_b_f": fox_b_f, "fox_q_gain": fox_q_gain,
            "fox_k_gain": fox_k_gain, "fox_w_o": fox_w_o,
            "s5_lam_re": s5_lam_re, "s5_lam_im": s5_lam_im, "s5_log_dt": s5_log_dt,
            "s5_b_re": s5_b_re, "s5_b_im": s5_b_im, "s5_c_re": s5_c_re, "s5_c_im": s5_c_im,
            "s5_d": s5_d, "s5_w_glu": s5_w_glu,
            "conv_w_in": conv_w_in, "conv_w": conv_w, "conv_w_out": conv_w_out}


def reference(x, c, ada_w, ada_b, norm_g, ffn_w_in, ffn_w_out, pool_w, pool_scale,
              fox_w_in, fox_b_f, fox_q_gain, fox_k_gain, fox_w_o,
              s5_lam_re, s5_lam_im, s5_log_dt, s5_b_re, s5_b_im, s5_c_re, s5_c_im,
              s5_d, s5_w_glu, conv_w_in, conv_w, conv_w_out):
    b_ = x.shape[0]
    cond = jax.nn.silu(c)
    for i in range(DEPTH):
        mod = (cond @ ada_w[i] + ada_b[i]).reshape(b_, N_SUBLAYERS, 3, D_MODEL)
        h = adaln(x, norm_g[i, 0], mod[:, 0, 0], mod[:, 0, 1])
        x = x + 0.5 * (1.0 + mod[:, 0, 2][:, None, :]) * swiglu(h, ffn_w_in[i, 0], ffn_w_out[i, 0])
        h = adaln(x, norm_g[i, 1], mod[:, 1, 0], mod[:, 1, 1])
        m, r = i % N_MIXERS, i // N_MIXERS
        if m == 0:
            y = pool_mixer(h, pool_w[r], pool_scale[r])
        elif m == 1:
            y = fox_attention(h, fox_w_in[r], fox_b_f[r], fox_q_gain[r], fox_k_gain[r], fox_w_o[r])
        elif m == 2:
            y = s5_mixer(h, s5_lam_re[r], s5_lam_im[r], s5_log_dt[r], s5_b_re[r], s5_b_im[r],
                         s5_c_re[r], s5_c_im[r], s5_d[r], s5_w_glu[r])
        else:
            y = short_conv_mixer(h, conv_w_in[r], conv_w[r], conv_w_out[r])
        x = x + (1.0 + mod[:, 1, 2][:, None, :]) * y
        h = adaln(x, norm_g[i, 2], mod[:, 2, 0], mod[:, 2, 1])
        x = x + 0.5 * (1.0 + mod[:, 2, 2][:, None, :]) * swiglu(h, ffn_w_in[i, 1], ffn_w_out[i, 1])
    return x
```

```python
import functools
import math

import jax
import jax.numpy as jnp
from jax import lax
from jax.experimental import pallas as pl
from jax.experimental.pallas import tpu as pltpu

F32 = jnp.float32
BF16 = jnp.bfloat16

NORM_EPS = 1e-6
N_SUBLAYERS = 3
N_ADA = 3 * N_SUBLAYERS
POOL_WINDOWS = (2, 4, 8, 16)
POOL_HALO = 16
FOX_HEADS = 16
FOX_HEAD_DIM = 64
S5_GROUP = 16
S5_STATE = 64
S5_CHUNK = 128
CONV_HALO = 8
LANES = 128
NEG_BIG = -1e30

VMEM_LIMIT = 56 * 1024 * 1024


def _params(n_grid, vmem=VMEM_LIMIT):
    return pltpu.CompilerParams(dimension_semantics=("arbitrary",) * n_grid,
                                vmem_limit_bytes=vmem)


def _dot(a, b):
    return jnp.dot(a, b, preferred_element_type=F32)


def _adaln(x, gain, shift, scale):
    ms = jnp.mean(x * x, axis=-1, keepdims=True)
    y = x * lax.rsqrt(ms + NORM_EPS) * gain
    return y * (1.0 + scale) + shift


def _mod_rows(mod_ref, sub):
    m = mod_ref[0]
    return m[3 * sub:3 * sub + 1], m[3 * sub + 1:3 * sub + 2], m[3 * sub + 2:3 * sub + 3]


def _const_spec(shape):
    nd = len(shape)
    return pl.BlockSpec(shape, lambda *_: (0,) * nd, pipeline_mode=pl.Buffered(1))


def _tile_specs(tm, d):
    x_spec = pl.BlockSpec((1, tm, d), lambda b, t: (b, t, 0))
    mod_spec = pl.BlockSpec((1, N_ADA, d), lambda b, t: (b, 0, 0))
    return x_spec, mod_spec


def _mod_kernel(c_ref, w_ref, b_ref, o_ref):
    c = c_ref[...]
    cond = (c * jax.nn.sigmoid(c)).astype(BF16)
    o_ref[0] = _dot(cond, w_ref[0].astype(BF16)) + b_ref[0]


def _modulation(c, ada_w, ada_b):
    depth, d, n = ada_w.shape
    b = c.shape[0]
    tn = 1024
    return pl.pallas_call(
        _mod_kernel,
        out_shape=jax.ShapeDtypeStruct((depth, b, n), F32),
        grid=(depth, n // tn),
        in_specs=[pl.BlockSpec((b, d), lambda l, j: (0, 0)),
                  pl.BlockSpec((1, d, tn), lambda l, j: (l, 0, j)),
                  pl.BlockSpec((1, 1, tn), lambda l, j: (l, 0, j))],
        out_specs=pl.BlockSpec((1, b, tn), lambda l, j: (l, 0, j)),
        compiler_params=_params(2),
    )(c, ada_w, ada_b.reshape(depth, 1, n))


def _ffn_kernel(sub, tf, x_ref, mod_ref, g_ref, win_ref, wout_ref, o_ref, h_sc, a_sc):
    x = x_ref[0]
    shift, scale, gate = _mod_rows(mod_ref, sub)
    h_sc[...] = _adaln(x, g_ref[...], shift, scale).astype(BF16)
    dff = a_sc.shape[1]
    for j in range(dff // tf):
        h = h_sc[...]
        g = _dot(h, win_ref[:, j * tf:(j + 1) * tf])
        u = _dot(h, win_ref[:, dff + j * tf:dff + (j + 1) * tf])
        a_sc[:, j * tf:(j + 1) * tf] = (g * jax.nn.sigmoid(g) * u).astype(BF16)
    y = _dot(a_sc[...], wout_ref[...])
    o_ref[0] = x + (0.5 * (1.0 + gate)) * y


def _ffn(x, mod, gain, w_in, w_out, sub, tm=512, tf=256):
    b, s, d = x.shape
    dff = w_out.shape[0]
    x_spec, mod_spec = _tile_specs(tm, d)
    return pl.pallas_call(
        functools.partial(_ffn_kernel, sub, tf),
        out_shape=jax.ShapeDtypeStruct(x.shape, F32),
        grid=(b, s // tm),
        in_specs=[x_spec, mod_spec, _const_spec((1, d)),
                  _const_spec((d, 2 * dff)), _const_spec((dff, d))],
        out_specs=x_spec,
        scratch_shapes=[pltpu.VMEM((tm, d), BF16), pltpu.VMEM((tm, dff), BF16)],
        compiler_params=_params(2),
    )(x, mod, gain.reshape(1, d), w_in.astype(BF16), w_out.astype(BF16))


def _pool_kernel(tm, x_ref, mod_ref, g_ref, w_ref, sc_ref, o_ref, h_sc):
    t = pl.program_id(1)
    x = x_ref[0]
    d = x.shape[1]
    cg = d // len(POOL_WINDOWS)
    shift, scale, gate = _mod_rows(mod_ref, 1)

    @pl.when(t == 0)
    def _():
        h_sc[0:POOL_HALO, :] = jnp.zeros((POOL_HALO, d), F32)

    h_sc[POOL_HALO:, :] = _adaln(x, g_ref[...], shift, scale)
    pos = t * tm + lax.broadcasted_iota(jnp.int32, (tm, 1), 0)
    ys = []
    for gi, w in enumerate(POOL_WINDOWS):
        sl = slice(gi * cg, (gi + 1) * cg)
        cur = h_sc[POOL_HALO:, sl]
        acc = cur
        for k in range(1, w):
            acc = acc + h_sc[POOL_HALO - k:POOL_HALO - k + tm, sl]
        cnt = jnp.minimum(pos + 1, w).astype(F32)
        pooled = acc / cnt - cur
        ys.append(_dot(pooled.astype(BF16), w_ref[gi]))
    y = jnp.concatenate(ys, axis=1) * sc_ref[...]
    h_sc[0:POOL_HALO, :] = h_sc[tm:tm + POOL_HALO, :]
    o_ref[0] = x + (1.0 + gate) * y


def _pool_layer(x, mod, gain, w_grp, scale, tm=512):
    b, s, d = x.shape
    ng, cg, _ = w_grp.shape
    x_spec, mod_spec = _tile_specs(tm, d)
    return pl.pallas_call(
        functools.partial(_pool_kernel, tm),
        out_shape=jax.ShapeDtypeStruct(x.shape, F32),
        grid=(b, s // tm),
        in_specs=[x_spec, mod_spec, _const_spec((1, d)),
                  _const_spec((ng, cg, cg)), _const_spec((1, d))],
        out_specs=x_spec,
        scratch_shapes=[pltpu.VMEM((tm + POOL_HALO, d), F32)],
        compiler_params=_params(2),
    )(x, mod, gain.reshape(1, d), w_grp.astype(BF16), scale.reshape(1, d))


def _split3(v):
    p1 = v.astype(BF16)
    r1 = v - p1.astype(F32)
    p2 = r1.astype(BF16)
    p3 = (r1 - p2.astype(F32)).astype(BF16)
    return p1, p2, p3


def _head_rms(q, seg_ref, segt_ref):
    sq = q * q
    ms = sum(_dot(p, seg_ref[...]) for p in _split3(sq))
    r = lax.rsqrt(ms + NORM_EPS)
    rb = sum(_dot(p, segt_ref[...]) for p in _split3(r))
    return q * rb


def _foxproj_kernel(x_ref, mod_ref, g_ref, wq_ref, wk_ref, wv_ref, wf_ref, bf_ref, qg_ref, kg_ref,
                    seg_ref, segt_ref, tri_ref, q_ref, k_ref, v_ref, cf_ref, carry_sc):
    t = pl.program_id(1)
    x = x_ref[0]
    shift, scale, _ = _mod_rows(mod_ref, 1)
    h = _adaln(x, g_ref[...], shift, scale).astype(BF16)
    q = _head_rms(_dot(h, wq_ref[...]), seg_ref, segt_ref)
    q_ref[0] = (q * qg_ref[...]).astype(BF16)
    k = _head_rms(_dot(h, wk_ref[...]), seg_ref, segt_ref)
    k_ref[0] = (k * kg_ref[...]).astype(BF16)
    v_ref[0] = _dot(h, wv_ref[...]).astype(BF16)
    fl = _dot(h, wf_ref[...]) + bf_ref[...]
    logf = jnp.minimum(fl, 0.0) - jnp.log(1.0 + jnp.exp(-jnp.abs(fl)))

    @pl.when(t == 0)
    def _():
        carry_sc[...] = jnp.zeros_like(carry_sc)

    tri = tri_ref[...]
    cf = sum(_dot(tri, p) for p in _split3(logf)) + carry_sc[0:1, :]
    cf_ref[0] = cf
    tm = cf.shape[0]
    carry_sc[0:1, :] = cf[tm - 1:tm, :]


def _fox_project(x, mod, gain, w_in, b_f, q_gain, k_gain, tm=512):
    b, s, d = x.shape
    x_spec, mod_spec = _tile_specs(tm, d)
    w = w_in.astype(BF16)
    wf = jnp.zeros((d, LANES), BF16).at[:, :FOX_HEADS].set(w[:, 3 * d:])
    bf = jnp.zeros((1, LANES), F32).at[0, :FOX_HEADS].set(b_f)
    head_of = jnp.arange(d) // FOX_HEAD_DIM
    seg = (head_of[:, None] == jnp.arange(LANES)[None, :]).astype(F32)
    qg = jnp.tile(q_gain, FOX_HEADS).reshape(1, d) * (FOX_HEAD_DIM ** -0.5)
    kg = jnp.tile(k_gain, FOX_HEADS).reshape(1, d)
    tri = (jnp.arange(tm)[:, None] >= jnp.arange(tm)[None, :]).astype(BF16)
    act = jax.ShapeDtypeStruct((b, s, d), BF16)
    return pl.pallas_call(
        _foxproj_kernel,
        out_shape=(act, act, act, jax.ShapeDtypeStruct((b, s, LANES), F32)),
        grid=(b, s // tm),
        in_specs=[x_spec, mod_spec, _const_spec((1, d)),
                  _const_spec((d, d)), _const_spec((d, d)), _const_spec((d, d)),
                  _const_spec((d, LANES)), _const_spec((1, LANES)),
                  _const_spec((1, d)), _const_spec((1, d)),
                  _const_spec((d, LANES)), _const_spec((LANES, d)), _const_spec((tm, tm))],
        out_specs=(pl.BlockSpec((1, tm, d), lambda b_, t: (b_, t, 0)),) * 3
        + (pl.BlockSpec((1, tm, LANES), lambda b_, t: (b_, t, 0)),),
        scratch_shapes=[pltpu.VMEM((8, LANES), F32)],
        compiler_params=_params(2),
    )(x, mod, gain.reshape(1, d), w[:, :d], w[:, d:2 * d], w[:, 2 * d:3 * d], wf, bf, qg, kg,
      (seg / FOX_HEAD_DIM).astype(BF16), seg.T.astype(BF16), tri)


def _foxattn_kernel(tq, q_ref, k_ref, v_ref, fcol_ref, frow_ref, o_ref):
    qi = pl.program_id(2)
    q = q_ref[0]
    lane = lax.broadcasted_iota(jnp.int32, (1, LANES), 1)
    first = lane < FOX_HEAD_DIM
    zero = jnp.zeros_like(q)
    q_heads = (jnp.where(first, q, zero), jnp.where(first, zero, q))
    fq = fcol_ref[0, 0]
    row = lax.broadcasted_iota(jnp.int32, (tq, tq), 0)
    col = lax.broadcasted_iota(jnp.int32, (tq, tq), 1)

    def step(j, carry, diagonal):
        start = pl.multiple_of(j * tq, tq)
        kj = k_ref[0, pl.ds(start, tq), :]
        vj = v_ref[0, pl.ds(start, tq), :]
        fk = frow_ref[0, 0, j]
        out = []
        for hd in range(2):
            m, l, acc = carry[hd]
            s = lax.dot_general(q_heads[hd], kj, (((1,), (1,)), ((), ())), preferred_element_type=F32)
            s = s + (fq[:, hd:hd + 1] - fk[hd:hd + 1, :])
            if diagonal:
                s = jnp.where(row >= col, s, NEG_BIG)
            m_new = jnp.maximum(m, jnp.max(s, axis=-1, keepdims=True))
            alpha = jnp.exp(m - m_new)
            p = jnp.exp(s - m_new)
            l = alpha * l + jnp.sum(p, axis=-1, keepdims=True)
            acc = alpha * acc + _dot(p.astype(BF16), vj)
            out.append((m_new, l, acc))
        return tuple(out)

    init = tuple((jnp.full((tq, 1), NEG_BIG, F32), jnp.zeros((tq, 1), F32), jnp.zeros((tq, LANES), F32))
                 for _ in range(2))
    carry = lax.fori_loop(0, qi, lambda j, c: step(j, c, False), init)
    (_, l0, a0), (_, l1, a1) = step(qi, carry, True)
    o_ref[0] = jnp.where(first, a0 / l0, a1 / l1).astype(o_ref.dtype)


def _fox_attend(q, k, v, cum_f, tq=256):
    b, s, d = q.shape
    pairs = d // LANES
    cf = cum_f[:, :, :FOX_HEADS].reshape(b, s, pairs, 2)
    fcol = cf.transpose(0, 2, 1, 3)
    frow = cf.reshape(b, s // tq, tq, pairs, 2).transpose(0, 3, 1, 4, 2)
    return pl.pallas_call(
        functools.partial(_foxattn_kernel, tq),
        out_shape=jax.ShapeDtypeStruct((b, s, d), BF16),
        grid=(b, pairs, s // tq),
        in_specs=[pl.BlockSpec((1, tq, LANES), lambda b_, p, i: (b_, i, p)),
                  pl.BlockSpec((1, s, LANES), lambda b_, p, i: (b_, 0, p)),
                  pl.BlockSpec((1, s, LANES), lambda b_, p, i: (b_, 0, p)),
                  pl.BlockSpec((1, 1, tq, 2), lambda b_, p, i: (b_, p, i, 0)),
                  pl.BlockSpec((1, 1, s // tq, 2, tq), lambda b_, p, i: (b_, p, 0, 0, 0))],
        out_specs=pl.BlockSpec((1, tq, LANES), lambda b_, p, i: (b_, i, p)),
        compiler_params=_params(3),
    )(q, k, v, fcol, frow)


def _proj_out_kernel(a_ref, x_ref, mod_ref, w_ref, o_ref):
    _, _, gate = _mod_rows(mod_ref, 1)
    o_ref[0] = x_ref[0] + (1.0 + gate) * _dot(a_ref[0], w_ref[...])


def _proj_out(a, x, mod, w, tm=512):
    b, s, d = x.shape
    x_spec, mod_spec = _tile_specs(tm, d)
    return pl.pallas_call(
        _proj_out_kernel,
        out_shape=jax.ShapeDtypeStruct(x.shape, F32),
        grid=(b, s // tm),
        in_specs=[x_spec, x_spec, mod_spec, _const_spec((d, d))],
        out_specs=x_spec,
        compiler_params=_params(2),
    )(a, x, mod, w.astype(BF16))


def _cmul(a, t):
    w = a.shape[1] // 2
    ar, ai, tr, ti = a[:, :w], a[:, w:], t[:, :w], t[:, w:]
    return jnp.concatenate([ar * tr - ai * ti, ar * ti + ai * tr], axis=1)


def _s5_kernel(tm, x_ref, mod_ref, g_ref, bm_ref, cm_ref, e_ref, f_ref, a_ref, tri_ref, dskip_ref,
               wglu_ref, o_ref, u_sc, y_sc, s_sc):
    t = pl.program_id(1)
    x = x_ref[0]
    shift, scale, gate = _mod_rows(mod_ref, 1)
    u_sc[...] = _adaln(x, g_ref[...], shift, scale)
    L = S5_CHUNK
    n_blocks = bm_ref.shape[0]

    @pl.when(t == 0)
    def _():
        s_sc[...] = jnp.zeros_like(s_sc)

    tri = tri_ref[...]
    for c in range(tm // L):
        rows = slice(c * L, (c + 1) * L)
        for kb in range(n_blocks):
            cols = slice(kb * LANES, (kb + 1) * LANES)
            bu = _dot(u_sc[rows, cols].astype(BF16), bm_ref[kb])
            z = _cmul(bu, e_ref[kb]).astype(BF16)
            w = _dot(tri, z) + s_sc[kb:kb + 1, :]
            xs = _cmul(w, f_ref[kb])
            s_sc[kb:kb + 1, :] = _cmul(xs[L - 1:L, :], a_ref[kb])
            y_sc[rows, cols] = _dot(xs.astype(BF16), cm_ref[kb])
    y = y_sc[...] + dskip_ref[...] * u_sc[...]
    g = jax.nn.gelu(y, approximate=True)
    out = g * jax.nn.sigmoid(_dot(g.astype(BF16), wglu_ref[...]))
    o_ref[0] = x + (1.0 + gate) * out


def _s5_tables(lam_re, lam_im, log_dt, b_re, b_im, c_re, c_im):
    g_, n_ = lam_re.shape
    i_ = b_re.shape[2]
    gl = LANES // i_
    nb = g_ // gl
    dt = jnp.exp(log_dt)[:, None]
    ar, ai = lam_re, lam_im
    mag = jnp.exp(ar * dt)
    lb_re, lb_im = mag * jnp.cos(ai * dt), mag * jnp.sin(ai * dt)
    den = ar * ar + ai * ai
    nr, ni = lb_re - 1.0, lb_im
    k_re = (nr * ar + ni * ai) / den
    k_im = (ni * ar - nr * ai) / den
    bb_re = k_re[..., None] * b_re - k_im[..., None] * b_im
    bb_im = k_re[..., None] * b_im + k_im[..., None] * b_re
    eye = jnp.eye(gl, dtype=F32)
    bb = jnp.stack([bb_re, bb_im]).reshape(2, nb, gl, n_, i_)
    bm = jnp.einsum('gh,pkgni->kgiphn', eye, bb).reshape(nb, gl * i_, 2 * gl * n_)
    cc = jnp.stack([c_re, -c_im]).reshape(2, nb, gl, i_, n_)
    cm = jnp.einsum('gh,pkgin->kpgnhi', eye, cc).reshape(nb, 2 * gl * n_, gl * i_)

    def lbar_pow(p):
        e = jnp.exp(p[:, None, None] * (ar * dt)[None])
        th = p[:, None, None] * (ai * dt)[None]
        t = jnp.stack([e * jnp.cos(th), e * jnp.sin(th)], axis=1)
        return t.reshape(-1, 2, nb, gl, n_).transpose(2, 0, 1, 3, 4).reshape(nb, -1, 2 * gl * n_)

    mid = S5_CHUNK // 2
    j = jnp.arange(S5_CHUNK, dtype=F32) - mid
    return (bm.astype(BF16), cm.astype(BF16), lbar_pow(-j), lbar_pow(j),
            lbar_pow(jnp.full((1,), mid + 1.0, F32)))


def _s5_layer(x, mod, gain, lam_re, lam_im, log_dt, b_re, b_im, c_re, c_im, d_skip, w_glu, tm=512):
    b, s, d = x.shape
    bm, cm, e_tab, f_tab, a_tab = _s5_tables(lam_re, lam_im, log_dt, b_re, b_im, c_re, c_im)
    nb, _, sw = bm.shape
    L = S5_CHUNK
    tri = (jnp.arange(L)[:, None] >= jnp.arange(L)[None, :]).astype(BF16)
    x_spec, mod_spec = _tile_specs(tm, d)
    return pl.pallas_call(
        functools.partial(_s5_kernel, tm),
        out_shape=jax.ShapeDtypeStruct(x.shape, F32),
        grid=(b, s // tm),
        in_specs=[x_spec, mod_spec, _const_spec((1, d)),
                  _const_spec((nb, LANES, sw)), _const_spec((nb, sw, LANES)),
                  _const_spec((nb, L, sw)), _const_spec((nb, L, sw)), _const_spec((nb, 1, sw)),
                  _const_spec((L, L)), _const_spec((1, d)), _const_spec((d, d))],
        out_specs=x_spec,
        scratch_shapes=[pltpu.VMEM((tm, d), F32), pltpu.VMEM((tm, d), F32), pltpu.VMEM((nb, sw), F32)],
        compiler_params=_params(2),
    )(x, mod, gain.reshape(1, d), bm, cm, e_tab, f_tab, a_tab, tri, d_skip.reshape(1, d),
      w_glu.astype(BF16))


def _conv_kernel(tm, x_ref, mod_ref, g_ref, win_ref, cw_ref, wout_ref, o_ref, cz_sc):
    t = pl.program_id(1)
    x = x_ref[0]
    d = x.shape[1]
    shift, scale, gate = _mod_rows(mod_ref, 1)
    h = _adaln(x, g_ref[...], shift, scale).astype(BF16)

    @pl.when(t == 0)
    def _():
        cz_sc[0:CONV_HALO, :] = jnp.zeros((CONV_HALO, d), F32)

    cz_sc[CONV_HALO:, :] = _dot(h, win_ref[:, d:2 * d]) * _dot(h, win_ref[:, 2 * d:3 * d])
    cw = cw_ref[...]
    conv = (cw[0:1] * cz_sc[CONV_HALO - 2:CONV_HALO - 2 + tm, :]
            + cw[1:2] * cz_sc[CONV_HALO - 1:CONV_HALO - 1 + tm, :]
            + cw[2:3] * cz_sc[CONV_HALO:, :])
    cz_sc[0:CONV_HALO, :] = cz_sc[tm:tm + CONV_HALO, :]
    gated = (_dot(h, win_ref[:, 0:d]) * conv).astype(BF16)
    o_ref[0] = x + (1.0 + gate) * _dot(gated, wout_ref[...])


def _conv_layer(x, mod, gain, w_in, conv_w, w_out, tm=512):
    b, s, d = x.shape
    kw = conv_w.shape[0]
    x_spec, mod_spec = _tile_specs(tm, d)
    return pl.pallas_call(
        functools.partial(_conv_kernel, tm),
        out_shape=jax.ShapeDtypeStruct(x.shape, F32),
        grid=(b, s // tm),
        in_specs=[x_spec, mod_spec, _const_spec((1, d)),
                  _const_spec((d, 3 * d)), _const_spec((kw, d)), _const_spec((d, d))],
        out_specs=x_spec,
        scratch_shapes=[pltpu.VMEM((tm + CONV_HALO, d), F32)],
        compiler_params=_params(2),
    )(x, mod, gain.reshape(1, d), w_in.astype(BF16), conv_w.reshape(kw, d), w_out.astype(BF16))


def kernel(x, c, ada_w, ada_b, norm_g, ffn_w_in, ffn_w_out, pool_w, pool_scale, fox_w_in, fox_b_f, fox_q_gain, fox_k_gain, fox_w_o, s5_lam_re, s5_lam_im, s5_log_dt, s5_b_re, s5_b_im, s5_c_re, s5_c_im, s5_d, s5_w_glu, conv_w_in, conv_w, conv_w_out):
    b, s, d = x.shape
    depth = ada_w.shape[0]
    mod_all = _modulation(c, ada_w, ada_b).reshape(depth, b, N_ADA, d)
    n_mixers = 4
    for i in range(depth):
        mod = mod_all[i]
        x = _ffn(x, mod, norm_g[i, 0], ffn_w_in[i, 0], ffn_w_out[i, 0], sub=0)
        m, r = i % n_mixers, i // n_mixers
        if m == 0:
            x = _pool_layer(x, mod, norm_g[i, 1], pool_w[r], pool_scale[r])
        elif m == 1:
            q, k, v, cum_f = _fox_project(x, mod, norm_g[i, 1], fox_w_in[r], fox_b_f[r],
                                          fox_q_gain[r], fox_k_gain[r])
            o = _fox_attend(q, k, v, cum_f)
            x = _proj_out(o, x, mod, fox_w_o[r])
        elif m == 2:
            x = _s5_layer(x, mod, norm_g[i, 1], s5_lam_re[r], s5_lam_im[r], s5_log_dt[r],
                          s5_b_re[r], s5_b_im[r], s5_c_re[r], s5_c_im[r], s5_d[r], s5_w_glu[r])
        else:
            x = _conv_layer(x, mod, norm_g[i, 1], conv_w_in[r], conv_w[r], conv_w_out[r])
        x = _ffn(x, mod, norm_g[i, 2], ffn_w_in[i, 1], ffn_w_out[i, 1], sub=2)
    return x
```

```python
import functools
import math

import jax
import jax.numpy as jnp
from jax import lax
from jax.experimental import pallas as pl
from jax.experimental.pallas import tpu as pltpu

F32 = jnp.float32
BF16 = jnp.bfloat16

NORM_EPS = 1e-6
N_SUBLAYERS = 3
N_ADA = 3 * N_SUBLAYERS
POOL_WINDOWS = (2, 4, 8, 16)
POOL_HALO = 16
FOX_HEADS = 16
FOX_HEAD_DIM = 64
S5_GROUP = 16
S5_STATE = 64
S5_CHUNK = 128
CONV_HALO = 8
LANES = 128
NEG_BIG = -1e30

VMEM_LIMIT = 56 * 1024 * 1024


def _params(n_grid, vmem=VMEM_LIMIT):
    return pltpu.CompilerParams(dimension_semantics=("arbitrary",) * n_grid,
                                vmem_limit_bytes=vmem)


def _dot(a, b):
    return jnp.dot(a, b, preferred_element_type=F32)


def _adaln(x, gain, shift, scale):
    ms = jnp.mean(x * x, axis=-1, keepdims=True)
    y = x * lax.rsqrt(ms + NORM_EPS) * gain
    return y * (1.0 + scale) + shift


def _mod_rows(mod_ref, sub):
    m = mod_ref[0]
    return m[3 * sub:3 * sub + 1], m[3 * sub + 1:3 * sub + 2], m[3 * sub + 2:3 * sub + 3]


def _const_spec(shape):
    nd = len(shape)
    return pl.BlockSpec(shape, lambda *_: (0,) * nd, pipeline_mode=pl.Buffered(1))


def _tile_specs(tm, d):
    x_spec = pl.BlockSpec((1, tm, d), lambda b, t: (b, t, 0))
    mod_spec = pl.BlockSpec((1, N_ADA, d), lambda b, t: (b, 0, 0))
    return x_spec, mod_spec


def _mod_kernel(c_ref, w_ref, b_ref, o_ref):
    c = c_ref[...]
    cond = (c * jax.nn.sigmoid(c)).astype(BF16)
    o_ref[0] = _dot(cond, w_ref[0].astype(BF16)) + b_ref[0]


def _modulation(c, ada_w, ada_b):
    depth, d, n = ada_w.shape
    b = c.shape[0]
    tn = 1024
    return pl.pallas_call(
        _mod_kernel,
        out_shape=jax.ShapeDtypeStruct((depth, b, n), F32),
        grid=(depth, n // tn),
        in_specs=[pl.BlockSpec((b, d), lambda l, j: (0, 0)),
                  pl.BlockSpec((1, d, tn), lambda l, j: (l, 0, j)),
                  pl.BlockSpec((1, 1, tn), lambda l, j: (l, 0, j))],
        out_specs=pl.BlockSpec((1, b, tn), lambda l, j: (l, 0, j)),
        compiler_params=_params(2),
    )(c, ada_w, ada_b.reshape(depth, 1, n))


def _ffn_kernel(sub, tf, x_ref, mod_ref, g_ref, win_ref, wout_ref, o_ref, h_sc, a_sc):
    x = x_ref[0]
    shift, scale, gate = _mod_rows(mod_ref, sub)
    h_sc[...] = _adaln(x, g_ref[...], shift, scale).astype(BF16)
    dff = a_sc.shape[1]
    for j in range(dff // tf):
        h = h_sc[...]
        g = _dot(h, win_ref[:, j * tf:(j + 1) * tf])
        u = _dot(h, win_ref[:, dff + j * tf:dff + (j + 1) * tf])
        a_sc[:, j * tf:(j + 1) * tf] = (g * jax.nn.sigmoid(g) * u).astype(BF16)
    y = _dot(a_sc[...], wout_ref[...])
    o_ref[0] = x + (0.5 * (1.0 + gate)) * y


def _ffn(x, mod, gain, w_in, w_out, sub, tm=512, tf=256):
    b, s, d = x.shape
    dff = w_out.shape[0]
    x_spec, mod_spec = _tile_specs(tm, d)
    return pl.pallas_call(
        functools.partial(_ffn_kernel, sub, tf),
        out_shape=jax.ShapeDtypeStruct(x.shape, F32),
        grid=(b, s // tm),
        in_specs=[x_spec, mod_spec, _const_spec((1, d)),
                  _const_spec((d, 2 * dff)), _const_spec((dff, d))],
        out_specs=x_spec,
        scratch_shapes=[pltpu.VMEM((tm, d), BF16), pltpu.VMEM((tm, dff), BF16)],
        compiler_params=_params(2),
    )(x, mod, gain.reshape(1, d), w_in.astype(BF16), w_out.astype(BF16))


def _pool_kernel(tm, x_ref, mod_ref, g_ref, w_ref, sc_ref, o_ref, h_sc):
    t = pl.program_id(1)
    x = x_ref[0]
    d = x.shape[1]
    cg = d // len(POOL_WINDOWS)
    shift, scale, gate = _mod_rows(mod_ref, 1)

    @pl.when(t == 0)
    def _():
        h_sc[0:POOL_HALO, :] = jnp.zeros((POOL_HALO, d), F32)

    h_sc[POOL_HALO:, :] = _adaln(x, g_ref[...], shift, scale)
    pos = t * tm + lax.broadcasted_iota(jnp.int32, (tm, 1), 0)
    ys = []
    for gi, w in enumerate(POOL_WINDOWS):
        sl = slice(gi * cg, (gi + 1) * cg)
        cur = h_sc[POOL_HALO:, sl]
        acc = cur
        for k in range(1, w):
            acc = acc + h_sc[POOL_HALO - k:POOL_HALO - k + tm, sl]
        cnt = jnp.minimum(pos + 1, w).astype(F32)
        pooled = acc / cnt - cur
        ys.append(_dot(pooled.astype(BF16), w_ref[gi]))
    y = jnp.concatenate(ys, axis=1) * sc_ref[...]
    h_sc[0:POOL_HALO, :] = h_sc[tm:tm + POOL_HALO, :]
    o_ref[0] = x + (1.0 + gate) * y


def _pool_layer(x, mod, gain, w_grp, scale, tm=512):
    b, s, d = x.shape
    ng, cg, _ = w_grp.shape
    x_spec, mod_spec = _tile_specs(tm, d)
    return pl.pallas_call(
        functools.partial(_pool_kernel, tm),
        out_shape=jax.ShapeDtypeStruct(x.shape, F32),
        grid=(b, s // tm),
        in_specs=[x_spec, mod_spec, _const_spec((1, d)),
                  _const_spec((ng, cg, cg)), _const_spec((1, d))],
        out_specs=x_spec,
        scratch_shapes=[pltpu.VMEM((tm + POOL_HALO, d), F32)],
        compiler_params=_params(2),
    )(x, mod, gain.reshape(1, d), w_grp.astype(BF16), scale.reshape(1, d))


def _split3(v):
    p1 = v.astype(BF16)
    r1 = v - p1.astype(F32)
    p2 = r1.astype(BF16)
    p3 = (r1 - p2.astype(F32)).astype(BF16)
    return p1, p2, p3


def _head_rms(q, seg_ref, segt_ref):
    sq = q * q
    ms = sum(_dot(p, seg_ref[...]) for p in _split3(sq))
    r = lax.rsqrt(ms + NORM_EPS)
    rb = sum(_dot(p, segt_ref[...]) for p in _split3(r))
    return q * rb


def _foxproj_kernel(x_ref, mod_ref, g_ref, wq_ref, wk_ref, wv_ref, wf_ref, bf_ref, qg_ref, kg_ref,
                    seg_ref, segt_ref, tri_ref, q_ref, k_ref, v_ref, cf_ref, carry_sc):
    t = pl.program_id(1)
    x = x_ref[0]
    shift, scale, _ = _mod_rows(mod_ref, 1)
    h = _adaln(x, g_ref[...], shift, scale).astype(BF16)
    q = _head_rms(_dot(h, wq_ref[...]), seg_ref, segt_ref)
    q_ref[0] = (q * qg_ref[...]).astype(BF16)
    k = _head_rms(_dot(h, wk_ref[...]), seg_ref, segt_ref)
    k_ref[0] = (k * kg_ref[...]).astype(BF16)
    v_ref[0] = _dot(h, wv_ref[...]).astype(BF16)
    fl = _dot(h, wf_ref[...]) + bf_ref[...]
    logf = jnp.minimum(fl, 0.0) - jnp.log(1.0 + jnp.exp(-jnp.abs(fl)))

    @pl.when(t == 0)
    def _():
        carry_sc[...] = jnp.zeros_like(carry_sc)

    tri = tri_ref[...]
    cf = sum(_dot(tri, p) for p in _split3(logf)) + carry_sc[0:1, :]
    cf_ref[0] = cf
    tm = cf.shape[0]
    carry_sc[0:1, :] = cf[tm - 1:tm, :]


def _fox_project(x, mod, gain, w_in, b_f, q_gain, k_gain, tm=512):
    b, s, d = x.shape
    x_spec, mod_spec = _tile_specs(tm, d)
    w = w_in.astype(BF16)
    wf = jnp.zeros((d, LANES), BF16).at[:, :FOX_HEADS].set(w[:, 3 * d:])
    bf = jnp.zeros((1, LANES), F32).at[0, :FOX_HEADS].set(b_f)
    head_of = jnp.arange(d) // FOX_HEAD_DIM
    seg = (head_of[:, None] == jnp.arange(LANES)[None, :]).astype(F32)
    qg = jnp.tile(q_gain, FOX_HEADS).reshape(1, d) * (FOX_HEAD_DIM ** -0.5)
    kg = jnp.tile(k_gain, FOX_HEADS).reshape(1, d)
    tri = (jnp.arange(tm)[:, None] >= jnp.arange(tm)[None, :]).astype(BF16)
    act = jax.ShapeDtypeStruct((b, s, d), BF16)
    return pl.pallas_call(
        _foxproj_kernel,
        out_shape=(act, act, act, jax.ShapeDtypeStruct((b, s, LANES), F32)),
        grid=(b, s // tm),
        in_specs=[x_spec, mod_spec, _const_spec((1, d)),
                  _const_spec((d, d)), _const_spec((d, d)), _const_spec((d, d)),
                  _const_spec((d, LANES)), _const_spec((1, LANES)),
                  _const_spec((1, d)), _const_spec((1, d)),
                  _const_spec((d, LANES)), _const_spec((LANES, d)), _const_spec((tm, tm))],
        out_specs=(pl.BlockSpec((1, tm, d), lambda b_, t: (b_, t, 0)),) * 3
        + (pl.BlockSpec((1, tm, LANES), lambda b_, t: (b_, t, 0)),),
        scratch_shapes=[pltpu.VMEM((8, LANES), F32)],
        compiler_params=_params(2),
    )(x, mod, gain.reshape(1, d), w[:, :d], w[:, d:2 * d], w[:, 2 * d:3 * d], wf, bf, qg, kg,
      (seg / FOX_HEAD_DIM).astype(BF16), seg.T.astype(BF16), tri)


def _foxattn_kernel(T, q_ref, k_ref, v_ref, fcol_ref, frow_ref, o_ref,
                    sa_sc, sb_sc, qh_sc, fq_sc, m_sc, acc_sc):
    qi = pl.program_id(2)
    rep = T // LANES
    lane = lax.broadcasted_iota(jnp.int32, (1, LANES), 1)
    first = lane < FOX_HEAD_DIM
    q = q_ref[0]
    zero = jnp.zeros_like(q)
    qh_sc[0] = jnp.where(first, q, zero)
    qh_sc[1] = jnp.where(first, zero, q)
    fq = fcol_ref[0, 0]
    for hd in range(2):
        fq_sc[hd] = jnp.broadcast_to(fq[:, hd:hd + 1], (T, LANES))
        m_sc[hd] = jnp.full((T, LANES), NEG_BIG, F32)
        acc_sc[hd] = jnp.zeros((T, LANES), F32)
    row = lax.broadcasted_iota(jnp.int32, (T, T), 0)
    col = lax.broadcasted_iota(jnp.int32, (T, T), 1)

    def tile_rows(j):
        return pl.ds(pl.multiple_of(j * T, T), T)

    def scores(j, s_sc):
        kj = k_ref[0, tile_rows(j), :]
        for hd in range(2):
            s_sc[hd] = lax.dot_general(qh_sc[hd], kj, (((1,), (1,)), ((), ())),
                                       preferred_element_type=F32)

    def absorb(j, s_sc, diagonal):
        vj = v_ref[0, tile_rows(j), :]
        ones = jnp.ones_like(vj)
        v_heads = (jnp.where(first, vj, ones), jnp.where(first, ones, vj))
        fk = frow_ref[0, 0, j]
        for hd in range(2):
            s = s_sc[hd] + (jnp.tile(fq_sc[hd], (1, rep)) - fk[hd:hd + 1, :])
            if diagonal:
                s = jnp.where(row >= col, s, NEG_BIG)
            m_prev = m_sc[hd]
            m_new = jnp.maximum(m_prev, jnp.max(s, axis=1, keepdims=True))
            p = jnp.exp(s - jnp.tile(m_new, (1, rep))).astype(BF16)
            acc_sc[hd] = jnp.exp(m_prev - m_new) * acc_sc[hd] + _dot(p, v_heads[hd])
            m_sc[hd] = m_new

    scores(0, sa_sc)

    def two_tiles(jj, carry):
        j = 2 * jj
        scores(j + 1, sb_sc)
        absorb(j, sa_sc, False)
        scores(j + 2, sa_sc)
        absorb(j + 1, sb_sc, False)
        return carry

    lax.fori_loop(0, qi // 2, two_tiles, 0)

    @pl.when(qi % 2 == 0)
    def _():
        absorb(qi, sa_sc, True)

    @pl.when(qi % 2 == 1)
    def _():
        scores(qi, sb_sc)
        absorb(qi - 1, sa_sc, False)
        absorb(qi, sb_sc, True)

    a0, a1 = acc_sc[0], acc_sc[1]
    half = LANES // 2
    o = jnp.where(first, a0 / pltpu.roll(a0, half, axis=1), a1 / pltpu.roll(a1, half, axis=1))
    o_ref[0] = o.astype(o_ref.dtype)


def _fox_attend(q, k, v, cum_f, tile=512):
    b, s, d = q.shape
    pairs = d // LANES
    cf = cum_f[:, :, :FOX_HEADS].reshape(b, s, pairs, 2)
    fcol = cf.transpose(0, 2, 1, 3)
    frow = cf.reshape(b, s // tile, tile, pairs, 2).transpose(0, 3, 1, 4, 2)
    return pl.pallas_call(
        functools.partial(_foxattn_kernel, tile),
        out_shape=jax.ShapeDtypeStruct((b, s, d), BF16),
        grid=(b, pairs, s // tile),
        in_specs=[pl.BlockSpec((1, tile, LANES), lambda b_, p, i: (b_, i, p)),
                  pl.BlockSpec((1, s, LANES), lambda b_, p, i: (b_, 0, p)),
                  pl.BlockSpec((1, s, LANES), lambda b_, p, i: (b_, 0, p)),
                  pl.BlockSpec((1, 1, tile, 2), lambda b_, p, i: (b_, p, i, 0)),
                  pl.BlockSpec((1, 1, s // tile, 2, tile), lambda b_, p, i: (b_, p, 0, 0, 0))],
        out_specs=pl.BlockSpec((1, tile, LANES), lambda b_, p, i: (b_, i, p)),
        scratch_shapes=[pltpu.VMEM((2, tile, tile), F32), pltpu.VMEM((2, tile, tile), F32),
                        pltpu.VMEM((2, tile, LANES), BF16), pltpu.VMEM((2, tile, LANES), F32),
                        pltpu.VMEM((2, tile, LANES), F32), pltpu.VMEM((2, tile, LANES), F32)],
        compiler_params=_params(3),
    )(q, k, v, fcol, frow)


def _proj_out_kernel(a_ref, x_ref, mod_ref, w_ref, o_ref):
    _, _, gate = _mod_rows(mod_ref, 1)
    o_ref[0] = x_ref[0] + (1.0 + gate) * _dot(a_ref[0], w_ref[...])


def _proj_out(a, x, mod, w, tm=512):
    b, s, d = x.shape
    x_spec, mod_spec = _tile_specs(tm, d)
    return pl.pallas_call(
        _proj_out_kernel,
        out_shape=jax.ShapeDtypeStruct(x.shape, F32),
        grid=(b, s // tm),
        in_specs=[x_spec, x_spec, mod_spec, _const_spec((d, d))],
        out_specs=x_spec,
        compiler_params=_params(2),
    )(a, x, mod, w.astype(BF16))


def _cmul(a, t):
    w = a.shape[1] // 2
    ar, ai, tr, ti = a[:, :w], a[:, w:], t[:, :w], t[:, w:]
    return jnp.concatenate([ar * tr - ai * ti, ar * ti + ai * tr], axis=1)


def _s5_kernel(tm, x_ref, mod_ref, g_ref, bm_ref, cm_ref, e_ref, f_ref, a_ref, tri_ref, dskip_ref,
               wglu_ref, o_ref, u_sc, y_sc, s_sc):
    t = pl.program_id(1)
    x = x_ref[0]
    shift, scale, gate = _mod_rows(mod_ref, 1)
    u_sc[...] = _adaln(x, g_ref[...], shift, scale)
    L = S5_CHUNK
    n_blocks = bm_ref.shape[0]

    @pl.when(t == 0)
    def _():
        s_sc[...] = jnp.zeros_like(s_sc)

    tri = tri_ref[...]
    for c in range(tm // L):
        rows = slice(c * L, (c + 1) * L)
        for kb in range(n_blocks):
            cols = slice(kb * LANES, (kb + 1) * LANES)
            bu = _dot(u_sc[rows, cols].astype(BF16), bm_ref[kb])
            z = _cmul(bu, e_ref[kb]).astype(BF16)
            w = _dot(tri, z) + s_sc[kb:kb + 1, :]
            xs = _cmul(w, f_ref[kb])
            s_sc[kb:kb + 1, :] = _cmul(xs[L - 1:L, :], a_ref[kb])
            y_sc[rows, cols] = _dot(xs.astype(BF16), cm_ref[kb])
    y = y_sc[...] + dskip_ref[...] * u_sc[...]
    g = jax.nn.gelu(y, approximate=True)
    out = g * jax.nn.sigmoid(_dot(g.astype(BF16), wglu_ref[...]))
    o_ref[0] = x + (1.0 + gate) * out


def _s5_tables(lam_re, lam_im, log_dt, b_re, b_im, c_re, c_im):
    g_, n_ = lam_re.shape
    i_ = b_re.shape[2]
    gl = LANES // i_
    nb = g_ // gl
    dt = jnp.exp(log_dt)[:, None]
    ar, ai = lam_re, lam_im
    mag = jnp.exp(ar * dt)
    lb_re, lb_im = mag * jnp.cos(ai * dt), mag * jnp.sin(ai * dt)
    den = ar * ar + ai * ai
    nr, ni = lb_re - 1.0, lb_im
    k_re = (nr * ar + ni * ai) / den
    k_im = (ni * ar - nr * ai) / den
    bb_re = k_re[..., None] * b_re - k_im[..., None] * b_im
    bb_im = k_re[..., None] * b_im + k_im[..., None] * b_re
    eye = jnp.eye(gl, dtype=F32)
    bb = jnp.stack([bb_re, bb_im]).reshape(2, nb, gl, n_, i_)
    bm = jnp.einsum('gh,pkgni->kgiphn', eye, bb).reshape(nb, gl * i_, 2 * gl * n_)
    cc = jnp.stack([c_re, -c_im]).reshape(2, nb, gl, i_, n_)
    cm = jnp.einsum('gh,pkgin->kpgnhi', eye, cc).reshape(nb, 2 * gl * n_, gl * i_)

    def lbar_pow(p):
        e = jnp.exp(p[:, None, None] * (ar * dt)[None])
        th = p[:, None, None] * (ai * dt)[None]
        t = jnp.stack([e * jnp.cos(th), e * jnp.sin(th)], axis=1)
        return t.reshape(-1, 2, nb, gl, n_).transpose(2, 0, 1, 3, 4).reshape(nb, -1, 2 * gl * n_)

    mid = S5_CHUNK // 2
    j = jnp.arange(S5_CHUNK, dtype=F32) - mid
    return (bm.astype(BF16), cm.astype(BF16), lbar_pow(-j), lbar_pow(j),
            lbar_pow(jnp.full((1,), mid + 1.0, F32)))


def _s5_layer(x, mod, gain, lam_re, lam_im, log_dt, b_re, b_im, c_re, c_im, d_skip, w_glu, tm=512):
    b, s, d = x.shape
    bm, cm, e_tab, f_tab, a_tab = _s5_tables(lam_re, lam_im, log_dt, b_re, b_im, c_re, c_im)
    nb, _, sw = bm.shape
    L = S5_CHUNK
    tri = (jnp.arange(L)[:, None] >= jnp.arange(L)[None, :]).astype(BF16)
    x_spec, mod_spec = _tile_specs(tm, d)
    return pl.pallas_call(
        functools.partial(_s5_kernel, tm),
        out_shape=jax.ShapeDtypeStruct(x.shape, F32),
        grid=(b, s // tm),
        in_specs=[x_spec, mod_spec, _const_spec((1, d)),
                  _const_spec((nb, LANES, sw)), _const_spec((nb, sw, LANES)),
                  _const_spec((nb, L, sw)), _const_spec((nb, L, sw)), _const_spec((nb, 1, sw)),
                  _const_spec((L, L)), _const_spec((1, d)), _const_spec((d, d))],
        out_specs=x_spec,
        scratch_shapes=[pltpu.VMEM((tm, d), F32), pltpu.VMEM((tm, d), F32), pltpu.VMEM((nb, sw), F32)],
        compiler_params=_params(2),
    )(x, mod, gain.reshape(1, d), bm, cm, e_tab, f_tab, a_tab, tri, d_skip.reshape(1, d),
      w_glu.astype(BF16))


def _conv_kernel(tm, x_ref, mod_ref, g_ref, win_ref, cw_ref, wout_ref, o_ref, cz_sc):
    t = pl.program_id(1)
    x = x_ref[0]
    d = x.shape[1]
    shift, scale, gate = _mod_rows(mod_ref, 1)
    h = _adaln(x, g_ref[...], shift, scale).astype(BF16)

    @pl.when(t == 0)
    def _():
        cz_sc[0:CONV_HALO, :] = jnp.zeros((CONV_HALO, d), F32)

    cz_sc[CONV_HALO:, :] = _dot(h, win_ref[:, d:2 * d]) * _dot(h, win_ref[:, 2 * d:3 * d])
    cw = cw_ref[...]
    conv = (cw[0:1] * cz_sc[CONV_HALO - 2:CONV_HALO - 2 + tm, :]
            + cw[1:2] * cz_sc[CONV_HALO - 1:CONV_HALO - 1 + tm, :]
            + cw[2:3] * cz_sc[CONV_HALO:, :])
    cz_sc[0:CONV_HALO, :] = cz_sc[tm:tm + CONV_HALO, :]
    gated = (_dot(h, win_ref[:, 0:d]) * conv).astype(BF16)
    o_ref[0] = x + (1.0 + gate) * _dot(gated, wout_ref[...])


def _conv_layer(x, mod, gain, w_in, conv_w, w_out, tm=512):
    b, s, d = x.shape
    kw = conv_w.shape[0]
    x_spec, mod_spec = _tile_specs(tm, d)
    return pl.pallas_call(
        functools.partial(_conv_kernel, tm),
        out_shape=jax.ShapeDtypeStruct(x.shape, F32),
        grid=(b, s // tm),
        in_specs=[x_spec, mod_spec, _const_spec((1, d)),
                  _const_spec((d, 3 * d)), _const_spec((kw, d)), _const_spec((d, d))],
        out_specs=x_spec,
        scratch_shapes=[pltpu.VMEM((tm + CONV_HALO, d), F32)],
        compiler_params=_params(2),
    )(x, mod, gain.reshape(1, d), w_in.astype(BF16), conv_w.reshape(kw, d), w_out.astype(BF16))


def kernel(x, c, ada_w, ada_b, norm_g, ffn_w_in, ffn_w_out, pool_w, pool_scale, fox_w_in, fox_b_f, fox_q_gain, fox_k_gain, fox_w_o, s5_lam_re, s5_lam_im, s5_log_dt, s5_b_re, s5_b_im, s5_c_re, s5_c_im, s5_d, s5_w_glu, conv_w_in, conv_w, conv_w_out):
    b, s, d = x.shape
    depth = ada_w.shape[0]
    mod_all = _modulation(c, ada_w, ada_b).reshape(depth, b, N_ADA, d)
    n_mixers = 4
    for i in range(depth):
        mod = mod_all[i]
        x = _ffn(x, mod, norm_g[i, 0], ffn_w_in[i, 0], ffn_w_out[i, 0], sub=0)
        m, r = i % n_mixers, i // n_mixers
        if m == 0:
            x = _pool_layer(x, mod, norm_g[i, 1], pool_w[r], pool_scale[r])
        elif m == 1:
            q, k, v, cum_f = _fox_project(x, mod, norm_g[i, 1], fox_w_in[r], fox_b_f[r],
                                          fox_q_gain[r], fox_k_gain[r])
            o = _fox_attend(q, k, v, cum_f)
            x = _proj_out(o, x, mod, fox_w_o[r])
        elif m == 2:
            x = _s5_layer(x, mod, norm_g[i, 1], s5_lam_re[r], s5_lam_im[r], s5_log_dt[r],
                          s5_b_re[r], s5_b_im[r], s5_c_re[r], s5_c_im[r], s5_d[r], s5_w_glu[r])
        else:
            x = _conv_layer(x, mod, norm_g[i, 1], conv_w_in[r], conv_w[r], conv_w_out[r])
        x = _ffn(x, mod, norm_g[i, 2], ffn_w_in[i, 1], ffn_w_out[i, 1], sub=2)
    return x
```

```python
import functools
import math

import jax
import jax.numpy as jnp
from jax import lax
from jax.experimental import pallas as pl
from jax.experimental.pallas import tpu as pltpu

F32 = jnp.float32
BF16 = jnp.bfloat16

NORM_EPS = 1e-6
N_SUBLAYERS = 3
N_ADA = 3 * N_SUBLAYERS
POOL_WINDOWS = (2, 4, 8, 16)
POOL_HALO = 16
FOX_HEADS = 16
FOX_HEAD_DIM = 64
S5_GROUP = 16
S5_STATE = 64
S5_CHUNK = 128
CONV_HALO = 8
LANES = 128
NEG_BIG = -1e30
LOG2E = math.log2(math.e)
AUG_LANES = 8

VMEM_LIMIT = 56 * 1024 * 1024


def _params(n_grid, vmem=VMEM_LIMIT):
    return pltpu.CompilerParams(dimension_semantics=("arbitrary",) * n_grid,
                                vmem_limit_bytes=vmem)


def _dot(a, b):
    return jnp.dot(a, b, preferred_element_type=F32)


def _adaln(x, gain, shift, scale):
    ms = jnp.mean(x * x, axis=-1, keepdims=True)
    y = x * lax.rsqrt(ms + NORM_EPS) * gain
    return y * (1.0 + scale) + shift


def _mod_rows(mod_ref, sub):
    m = mod_ref[0]
    return m[3 * sub:3 * sub + 1], m[3 * sub + 1:3 * sub + 2], m[3 * sub + 2:3 * sub + 3]


def _const_spec(shape):
    nd = len(shape)
    return pl.BlockSpec(shape, lambda *_: (0,) * nd, pipeline_mode=pl.Buffered(1))


def _tile_specs(tm, d):
    x_spec = pl.BlockSpec((1, tm, d), lambda b, t: (b, t, 0))
    mod_spec = pl.BlockSpec((1, N_ADA, d), lambda b, t: (b, 0, 0))
    return x_spec, mod_spec


def _mod_kernel(c_ref, w_ref, b_ref, o_ref):
    c = c_ref[...]
    cond = (c * jax.nn.sigmoid(c)).astype(BF16)
    o_ref[0] = _dot(cond, w_ref[0].astype(BF16)) + b_ref[0]


def _modulation(c, ada_w, ada_b):
    depth, d, n = ada_w.shape
    b = c.shape[0]
    tn = 1024
    return pl.pallas_call(
        _mod_kernel,
        out_shape=jax.ShapeDtypeStruct((depth, b, n), F32),
        grid=(depth, n // tn),
        in_specs=[pl.BlockSpec((b, d), lambda l, j: (0, 0)),
                  pl.BlockSpec((1, d, tn), lambda l, j: (l, 0, j)),
                  pl.BlockSpec((1, 1, tn), lambda l, j: (l, 0, j))],
        out_specs=pl.BlockSpec((1, b, tn), lambda l, j: (l, 0, j)),
        compiler_params=_params(2),
    )(c, ada_w, ada_b.reshape(depth, 1, n))


def _ffn_kernel(sub, tf, x_ref, mod_ref, g_ref, win_ref, wout_ref, o_ref, h_sc, a_sc):
    x = x_ref[0]
    shift, scale, gate = _mod_rows(mod_ref, sub)
    h_sc[...] = _adaln(x, g_ref[...], shift, scale).astype(BF16)
    dff = a_sc.shape[1]
    for j in range(dff // tf):
        h = h_sc[...]
        g = _dot(h, win_ref[:, j * tf:(j + 1) * tf])
        u = _dot(h, win_ref[:, dff + j * tf:dff + (j + 1) * tf])
        a_sc[:, j * tf:(j + 1) * tf] = (g * jax.nn.sigmoid(g) * u).astype(BF16)
    y = _dot(a_sc[...], wout_ref[...])
    o_ref[0] = x + (0.5 * (1.0 + gate)) * y


def _ffn(x, mod, gain, w_in, w_out, sub, tm=512, tf=256):
    b, s, d = x.shape
    dff = w_out.shape[0]
    x_spec, mod_spec = _tile_specs(tm, d)
    return pl.pallas_call(
        functools.partial(_ffn_kernel, sub, tf),
        out_shape=jax.ShapeDtypeStruct(x.shape, F32),
        grid=(b, s // tm),
        in_specs=[x_spec, mod_spec, _const_spec((1, d)),
                  _const_spec((d, 2 * dff)), _const_spec((dff, d))],
        out_specs=x_spec,
        scratch_shapes=[pltpu.VMEM((tm, d), BF16), pltpu.VMEM((tm, dff), BF16)],
        compiler_params=_params(2),
    )(x, mod, gain.reshape(1, d), w_in.astype(BF16), w_out.astype(BF16))


def _pool_kernel(tm, x_ref, mod_ref, g_ref, w_ref, sc_ref, o_ref, h_sc):
    t = pl.program_id(1)
    x = x_ref[0]
    d = x.shape[1]
    cg = d // len(POOL_WINDOWS)
    shift, scale, gate = _mod_rows(mod_ref, 1)

    @pl.when(t == 0)
    def _():
        h_sc[0:POOL_HALO, :] = jnp.zeros((POOL_HALO, d), F32)

    h_sc[POOL_HALO:, :] = _adaln(x, g_ref[...], shift, scale)
    pos = t * tm + lax.broadcasted_iota(jnp.int32, (tm, 1), 0)
    ys = []
    for gi, w in enumerate(POOL_WINDOWS):
        sl = slice(gi * cg, (gi + 1) * cg)
        cur = h_sc[POOL_HALO:, sl]
        acc = cur
        for k in range(1, w):
            acc = acc + h_sc[POOL_HALO - k:POOL_HALO - k + tm, sl]
        cnt = jnp.minimum(pos + 1, w).astype(F32)
        pooled = acc / cnt - cur
        ys.append(_dot(pooled.astype(BF16), w_ref[gi]))
    y = jnp.concatenate(ys, axis=1) * sc_ref[...]
    h_sc[0:POOL_HALO, :] = h_sc[tm:tm + POOL_HALO, :]
    o_ref[0] = x + (1.0 + gate) * y


def _pool_layer(x, mod, gain, w_grp, scale, tm=512):
    b, s, d = x.shape
    ng, cg, _ = w_grp.shape
    x_spec, mod_spec = _tile_specs(tm, d)
    return pl.pallas_call(
        functools.partial(_pool_kernel, tm),
        out_shape=jax.ShapeDtypeStruct(x.shape, F32),
        grid=(b, s // tm),
        in_specs=[x_spec, mod_spec, _const_spec((1, d)),
                  _const_spec((ng, cg, cg)), _const_spec((1, d))],
        out_specs=x_spec,
        scratch_shapes=[pltpu.VMEM((tm + POOL_HALO, d), F32)],
        compiler_params=_params(2),
    )(x, mod, gain.reshape(1, d), w_grp.astype(BF16), scale.reshape(1, d))


def _split3(v):
    p1 = v.astype(BF16)
    r1 = v - p1.astype(F32)
    p2 = r1.astype(BF16)
    p3 = (r1 - p2.astype(F32)).astype(BF16)
    return p1, p2, p3


def _head_rms(q, seg_ref, segt_ref):
    ms = _dot((q * q).astype(BF16), seg_ref[...])
    r = lax.rsqrt(ms + NORM_EPS)
    r_hi = r.astype(BF16)
    r_lo = (r - r_hi.astype(F32)).astype(BF16)
    rb = _dot(r_hi, segt_ref[...]) + _dot(r_lo, segt_ref[...])
    return q * rb


def _foxproj_kernel(x_ref, mod_ref, g_ref, wq_ref, wk_ref, wv_ref, wf_ref, bf_ref, qg_ref, kg_ref,
                    seg_ref, segt_ref, tri_ref, place_ref, ones_ref,
                    q_ref, k_ref, v_ref, qa_ref, ka_ref, carry_sc):
    t = pl.program_id(1)
    x = x_ref[0]
    shift, scale, _ = _mod_rows(mod_ref, 1)
    h = _adaln(x, g_ref[...], shift, scale).astype(BF16)
    q = _head_rms(_dot(h, wq_ref[...]), seg_ref, segt_ref)
    q_ref[0] = (q * qg_ref[...]).astype(BF16)
    k = _head_rms(_dot(h, wk_ref[...]), seg_ref, segt_ref)
    k_ref[0] = (k * kg_ref[...]).astype(BF16)
    v_ref[0] = _dot(h, wv_ref[...]).astype(BF16)
    fl = _dot(h, wf_ref[...]) + bf_ref[...]
    logf = jnp.minimum(fl, 0.0) - jnp.log(1.0 + jnp.exp(-jnp.abs(fl)))

    @pl.when(t == 0)
    def _():
        carry_sc[...] = jnp.zeros_like(carry_sc)

    tri = tri_ref[...]
    cf = sum(_dot(tri, p) for p in _split3(logf)) + carry_sc[0:1, :]
    tm = cf.shape[0]
    carry_sc[0:1, :] = cf[tm - 1:tm, :]
    pieces = jnp.concatenate(_split3(cf * LOG2E), axis=1)
    aug = _dot(pieces, place_ref[...]) + ones_ref[...]
    qa_ref[0] = aug[:, :LANES].astype(BF16)
    ka_ref[0] = aug[:, LANES:].astype(BF16)


def _fox_project(x, mod, gain, w_in, b_f, q_gain, k_gain, tm=512):
    b, s, d = x.shape
    x_spec, mod_spec = _tile_specs(tm, d)
    w = w_in.astype(BF16)
    wf = jnp.zeros((d, LANES), BF16).at[:, :FOX_HEADS].set(w[:, 3 * d:])
    bf = jnp.zeros((1, LANES), F32).at[0, :FOX_HEADS].set(b_f)
    head_of = jnp.arange(d) // FOX_HEAD_DIM
    seg = (head_of[:, None] == jnp.arange(LANES)[None, :]).astype(F32)
    qg = jnp.tile(q_gain, FOX_HEADS).reshape(1, d) * (FOX_HEAD_DIM ** -0.5 * LOG2E)
    kg = jnp.tile(k_gain, FOX_HEADS).reshape(1, d)
    tri = (jnp.arange(tm)[:, None] >= jnp.arange(tm)[None, :]).astype(BF16)
    hh, ii = jnp.meshgrid(jnp.arange(FOX_HEADS), jnp.arange(3), indexing="ij")
    place = jnp.zeros((3 * LANES, 2 * LANES), F32)
    place = place.at[ii * LANES + hh, AUG_LANES * hh + 3 + ii].set(1.0)
    place = place.at[ii * LANES + hh, LANES + AUG_LANES * hh + ii].set(-1.0)
    ones = jnp.zeros((1, 2 * LANES), F32)
    ones = ones.at[0, AUG_LANES * hh + ii].set(1.0).at[0, LANES + AUG_LANES * hh + 3 + ii].set(1.0)
    act = jax.ShapeDtypeStruct((b, s, d), BF16)
    aug = jax.ShapeDtypeStruct((b, s, LANES), BF16)
    return pl.pallas_call(
        _foxproj_kernel,
        out_shape=(act, act, act, aug, aug),
        grid=(b, s // tm),
        in_specs=[x_spec, mod_spec, _const_spec((1, d)),
                  _const_spec((d, d)), _const_spec((d, d)), _const_spec((d, d)),
                  _const_spec((d, LANES)), _const_spec((1, LANES)),
                  _const_spec((1, d)), _const_spec((1, d)),
                  _const_spec((d, LANES)), _const_spec((LANES, d)), _const_spec((tm, tm)),
                  _const_spec((3 * LANES, 2 * LANES)), _const_spec((1, 2 * LANES))],
        out_specs=(pl.BlockSpec((1, tm, d), lambda b_, t: (b_, t, 0)),) * 3
        + (pl.BlockSpec((1, tm, LANES), lambda b_, t: (b_, t, 0)),) * 2,
        scratch_shapes=[pltpu.VMEM((8, LANES), F32)],
        compiler_params=_params(2),
    )(x, mod, gain.reshape(1, d), w[:, :d], w[:, d:2 * d], w[:, 2 * d:3 * d], wf, bf, qg, kg,
      (seg / FOX_HEAD_DIM).astype(BF16), seg.T.astype(BF16), tri, place.astype(BF16), ones)


def _foxattn_kernel(T, q_ref, qa_ref, k_ref, ka_ref, v_ref, o_ref, sa_sc, sb_sc, qh_sc, m_sc, acc_sc):
    pair = pl.program_id(1)
    qi = pl.program_id(2)
    rep = T // LANES
    lane = lax.broadcasted_iota(jnp.int32, (1, LANES), 1)
    first = lane < FOX_HEAD_DIM
    q, qa = q_ref[0], qa_ref[0]
    zero = jnp.zeros_like(q)
    for hd in range(2):
        own = (lane < FOX_HEAD_DIM) == (hd == 0)
        bias_lo = AUG_LANES * (2 * pair + hd)
        own_bias = (lane >= bias_lo) & (lane < bias_lo + AUG_LANES)
        qh_sc[hd] = jnp.concatenate([jnp.where(own, q, zero), jnp.where(own_bias, qa, zero)], axis=1)
        m_sc[hd] = jnp.full((T, LANES), NEG_BIG, F32)
        acc_sc[hd] = jnp.zeros((T, LANES), F32)
    row = lax.broadcasted_iota(jnp.int32, (T, T), 0)
    col = lax.broadcasted_iota(jnp.int32, (T, T), 1)

    def tile_rows(j):
        return pl.ds(pl.multiple_of(j * T, T), T)

    def scores(j, s_sc):
        kj = jnp.concatenate([k_ref[0, tile_rows(j), :], ka_ref[0, tile_rows(j), :]], axis=1)
        for hd in range(2):
            s_sc[hd] = lax.dot_general(qh_sc[hd], kj, (((1,), (1,)), ((), ())),
                                       preferred_element_type=F32)

    def absorb(j, s_sc, diagonal):
        vj = v_ref[0, tile_rows(j), :]
        ones = jnp.ones_like(vj)
        v_heads = (jnp.where(first, vj, ones), jnp.where(first, ones, vj))
        for hd in range(2):
            s = s_sc[hd]
            if diagonal:
                s = jnp.where(row >= col, s, NEG_BIG)
            m_prev = m_sc[hd]
            m_new = jnp.maximum(m_prev, jnp.max(s, axis=1, keepdims=True))
            p = jnp.exp2(s - jnp.tile(m_new, (1, rep))).astype(BF16)
            acc_sc[hd] = jnp.exp2(m_prev - m_new) * acc_sc[hd] + _dot(p, v_heads[hd])
            m_sc[hd] = m_new

    scores(0, sa_sc)

    def two_tiles(jj, carry):
        j = 2 * jj
        scores(j + 1, sb_sc)
        absorb(j, sa_sc, False)
        scores(j + 2, sa_sc)
        absorb(j + 1, sb_sc, False)
        return carry

    lax.fori_loop(0, qi // 2, two_tiles, 0)

    @pl.when(qi % 2 == 0)
    def _():
        absorb(qi, sa_sc, True)

    @pl.when(qi % 2 == 1)
    def _():
        scores(qi, sb_sc)
        absorb(qi - 1, sa_sc, False)
        absorb(qi, sb_sc, True)

    a0, a1 = acc_sc[0], acc_sc[1]
    half = LANES // 2
    o = jnp.where(first, a0 / pltpu.roll(a0, half, axis=1), a1 / pltpu.roll(a1, half, axis=1))
    o_ref[0] = o.astype(o_ref.dtype)


def _fox_attend(q, qa, k, ka, v, tile=512):
    b, s, d = q.shape
    pairs = d // LANES
    return pl.pallas_call(
        functools.partial(_foxattn_kernel, tile),
        out_shape=jax.ShapeDtypeStruct((b, s, d), BF16),
        grid=(b, pairs, s // tile),
        in_specs=[pl.BlockSpec((1, tile, LANES), lambda b_, p, i: (b_, i, p)),
                  pl.BlockSpec((1, tile, LANES), lambda b_, p, i: (b_, i, 0)),
                  pl.BlockSpec((1, s, LANES), lambda b_, p, i: (b_, 0, p)),
                  pl.BlockSpec((1, s, LANES), lambda b_, p, i: (b_, 0, 0)),
                  pl.BlockSpec((1, s, LANES), lambda b_, p, i: (b_, 0, p))],
        out_specs=pl.BlockSpec((1, tile, LANES), lambda b_, p, i: (b_, i, p)),
        scratch_shapes=[pltpu.VMEM((2, tile, tile), F32), pltpu.VMEM((2, tile, tile), F32),
                        pltpu.VMEM((2, tile, 2 * LANES), BF16),
                        pltpu.VMEM((2, tile, LANES), F32), pltpu.VMEM((2, tile, LANES), F32)],
        compiler_params=_params(3),
    )(q, qa, k, ka, v)


def _proj_out_kernel(a_ref, x_ref, mod_ref, w_ref, o_ref):
    _, _, gate = _mod_rows(mod_ref, 1)
    o_ref[0] = x_ref[0] + (1.0 + gate) * _dot(a_ref[0], w_ref[...])


def _proj_out(a, x, mod, w, tm=512):
    b, s, d = x.shape
    x_spec, mod_spec = _tile_specs(tm, d)
    return pl.pallas_call(
        _proj_out_kernel,
        out_shape=jax.ShapeDtypeStruct(x.shape, F32),
        grid=(b, s // tm),
        in_specs=[x_spec, x_spec, mod_spec, _const_spec((d, d))],
        out_specs=x_spec,
        compiler_params=_params(2),
    )(a, x, mod, w.astype(BF16))


def _cmul(a, t):
    w = a.shape[1] // 2
    ar, ai, tr, ti = a[:, :w], a[:, w:], t[:, :w], t[:, w:]
    return jnp.concatenate([ar * tr - ai * ti, ar * ti + ai * tr], axis=1)


def _s5_kernel(tm, x_ref, mod_ref, g_ref, bm_ref, cm_ref, e_ref, f_ref, a_ref, tri_ref, dskip_ref,
               wglu_ref, o_ref, u_sc, y_sc, s_sc):
    t = pl.program_id(1)
    x = x_ref[0]
    shift, scale, gate = _mod_rows(mod_ref, 1)
    u_sc[...] = _adaln(x, g_ref[...], shift, scale)
    L = S5_CHUNK
    n_blocks = bm_ref.shape[0]

    @pl.when(t == 0)
    def _():
        s_sc[...] = jnp.zeros_like(s_sc)

    tri = tri_ref[...]
    chunks = [slice(c * L, (c + 1) * L) for c in range(tm // L)]
    for kb in range(n_blocks):
        cols = slice(kb * LANES, (kb + 1) * LANES)
        bu = _dot(u_sc[:, cols].astype(BF16), bm_ref[kb])
        sums = [_dot(tri, _cmul(bu[rows], e_ref[kb]).astype(BF16)) for rows in chunks]
        f_last = f_ref[kb, L - 1:L, :]
        carried = [s_sc[kb:kb + 1, :]]
        for p in sums:
            carried.append(_cmul(_cmul(p[L - 1:L, :] + carried[-1], f_last), a_ref[kb]))
        s_sc[kb:kb + 1, :] = carried[-1]
        xs = jnp.concatenate([_cmul(p + s, f_ref[kb]).astype(BF16) for p, s in zip(sums, carried)],
                             axis=0)
        y_sc[:, cols] = _dot(xs, cm_ref[kb])
    y = y_sc[...] + dskip_ref[...] * u_sc[...]
    g = jax.nn.gelu(y, approximate=True)
    out = g * jax.nn.sigmoid(_dot(g.astype(BF16), wglu_ref[...]))
    o_ref[0] = x + (1.0 + gate) * out


def _s5_tables(lam_re, lam_im, log_dt, b_re, b_im, c_re, c_im):
    g_, n_ = lam_re.shape
    i_ = b_re.shape[2]
    gl = LANES // i_
    nb = g_ // gl
    dt = jnp.exp(log_dt)[:, None]
    ar, ai = lam_re, lam_im
    mag = jnp.exp(ar * dt)
    lb_re, lb_im = mag * jnp.cos(ai * dt), mag * jnp.sin(ai * dt)
    den = ar * ar + ai * ai
    nr, ni = lb_re - 1.0, lb_im
    k_re = (nr * ar + ni * ai) / den
    k_im = (ni * ar - nr * ai) / den
    bb_re = k_re[..., None] * b_re - k_im[..., None] * b_im
    bb_im = k_re[..., None] * b_im + k_im[..., None] * b_re
    eye = jnp.eye(gl, dtype=F32)
    bb = jnp.stack([bb_re, bb_im]).reshape(2, nb, gl, n_, i_)
    bm = jnp.einsum('gh,pkgni->kgiphn', eye, bb).reshape(nb, gl * i_, 2 * gl * n_)
    cc = jnp.stack([c_re, -c_im]).reshape(2, nb, gl, i_, n_)
    cm = jnp.einsum('gh,pkgin->kpgnhi', eye, cc).reshape(nb, 2 * gl * n_, gl * i_)

    def lbar_pow(p):
        e = jnp.exp(p[:, None, None] * (ar * dt)[None])
        th = p[:, None, None] * (ai * dt)[None]
        t = jnp.stack([e * jnp.cos(th), e * jnp.sin(th)], axis=1)
        return t.reshape(-1, 2, nb, gl, n_).transpose(2, 0, 1, 3, 4).reshape(nb, -1, 2 * gl * n_)

    mid = S5_CHUNK // 2
    j = jnp.arange(S5_CHUNK, dtype=F32) - mid
    return (bm.astype(BF16), cm.astype(BF16), lbar_pow(-j), lbar_pow(j),
            lbar_pow(jnp.full((1,), mid + 1.0, F32)))


def _s5_layer(x, mod, gain, lam_re, lam_im, log_dt, b_re, b_im, c_re, c_im, d_skip, w_glu, tm=512):
    b, s, d = x.shape
    bm, cm, e_tab, f_tab, a_tab = _s5_tables(lam_re, lam_im, log_dt, b_re, b_im, c_re, c_im)
    nb, _, sw = bm.shape
    L = S5_CHUNK
    tri = (jnp.arange(L)[:, None] >= jnp.arange(L)[None, :]).astype(BF16)
    x_spec, mod_spec = _tile_specs(tm, d)
    return pl.pallas_call(
        functools.partial(_s5_kernel, tm),
        out_shape=jax.ShapeDtypeStruct(x.shape, F32),
        grid=(b, s // tm),
        in_specs=[x_spec, mod_spec, _const_spec((1, d)),
                  _const_spec((nb, LANES, sw)), _const_spec((nb, sw, LANES)),
                  _const_spec((nb, L, sw)), _const_spec((nb, L, sw)), _const_spec((nb, 1, sw)),
                  _const_spec((L, L)), _const_spec((1, d)), _const_spec((d, d))],
        out_specs=x_spec,
        scratch_shapes=[pltpu.VMEM((tm, d), F32), pltpu.VMEM((tm, d), F32), pltpu.VMEM((nb, sw), F32)],
        compiler_params=_params(2),
    )(x, mod, gain.reshape(1, d), bm, cm, e_tab, f_tab, a_tab, tri, d_skip.reshape(1, d),
      w_glu.astype(BF16))


def _conv_kernel(tm, x_ref, mod_ref, g_ref, win_ref, cw_ref, wout_ref, o_ref, cz_sc):
    t = pl.program_id(1)
    x = x_ref[0]
    d = x.shape[1]
    shift, scale, gate = _mod_rows(mod_ref, 1)
    h = _adaln(x, g_ref[...], shift, scale).astype(BF16)

    @pl.when(t == 0)
    def _():
        cz_sc[0:CONV_HALO, :] = jnp.zeros((CONV_HALO, d), F32)

    cz_sc[CONV_HALO:, :] = _dot(h, win_ref[:, d:2 * d]) * _dot(h, win_ref[:, 2 * d:3 * d])
    cw = cw_ref[...]
    conv = (cw[0:1] * cz_sc[CONV_HALO - 2:CONV_HALO - 2 + tm, :]
            + cw[1:2] * cz_sc[CONV_HALO - 1:CONV_HALO - 1 + tm, :]
            + cw[2:3] * cz_sc[CONV_HALO:, :])
    cz_sc[0:CONV_HALO, :] = cz_sc[tm:tm + CONV_HALO, :]
    gated = (_dot(h, win_ref[:, 0:d]) * conv).astype(BF16)
    o_ref[0] = x + (1.0 + gate) * _dot(gated, wout_ref[...])


def _conv_layer(x, mod, gain, w_in, conv_w, w_out, tm=512):
    b, s, d = x.shape
    kw = conv_w.shape[0]
    x_spec, mod_spec = _tile_specs(tm, d)
    return pl.pallas_call(
        functools.partial(_conv_kernel, tm),
        out_shape=jax.ShapeDtypeStruct(x.shape, F32),
        grid=(b, s // tm),
        in_specs=[x_spec, mod_spec, _const_spec((1, d)),
                  _const_spec((d, 3 * d)), _const_spec((kw, d)), _const_spec((d, d))],
        out_specs=x_spec,
        scratch_shapes=[pltpu.VMEM((tm + CONV_HALO, d), F32)],
        compiler_params=_params(2),
    )(x, mod, gain.reshape(1, d), w_in.astype(BF16), conv_w.reshape(kw, d), w_out.astype(BF16))


def kernel(x, c, ada_w, ada_b, norm_g, ffn_w_in, ffn_w_out, pool_w, pool_scale, fox_w_in, fox_b_f, fox_q_gain, fox_k_gain, fox_w_o, s5_lam_re, s5_lam_im, s5_log_dt, s5_b_re, s5_b_im, s5_c_re, s5_c_im, s5_d, s5_w_glu, conv_w_in, conv_w, conv_w_out):
    b, s, d = x.shape
    depth = ada_w.shape[0]
    mod_all = _modulation(c, ada_w, ada_b).reshape(depth, b, N_ADA, d)
    n_mixers = 4
    for i in range(depth):
        mod = mod_all[i]
        x = _ffn(x, mod, norm_g[i, 0], ffn_w_in[i, 0], ffn_w_out[i, 0], sub=0)
        m, r = i % n_mixers, i // n_mixers
        if m == 0:
            x = _pool_layer(x, mod, norm_g[i, 1], pool_w[r], pool_scale[r])
        elif m == 1:
            q, k, v, qa, ka = _fox_project(x, mod, norm_g[i, 1], fox_w_in[r], fox_b_f[r],
                                           fox_q_gain[r], fox_k_gain[r])
            o = _fox_attend(q, qa, k, ka, v)
            x = _proj_out(o, x, mod, fox_w_o[r])
        elif m == 2:
            x = _s5_layer(x, mod, norm_g[i, 1], s5_lam_re[r], s5_lam_im[r], s5_log_dt[r],
                          s5_b_re[r], s5_b_im[r], s5_c_re[r], s5_c_im[r], s5_d[r], s5_w_glu[r])
        else:
            x = _conv_layer(x, mod, norm_g[i, 1], conv_w_in[r], conv_w[r], conv_w_out[r])
        x = _ffn(x, mod, norm_g[i, 2], ffn_w_in[i, 1], ffn_w_out[i, 1], sub=2)
    return x
```

```python
import functools
import math

import jax
import jax.numpy as jnp
from jax import lax
from jax.experimental import pallas as pl
from jax.experimental.pallas import tpu as pltpu

F32 = jnp.float32
BF16 = jnp.bfloat16

NORM_EPS = 1e-6
N_SUBLAYERS = 3
N_ADA = 3 * N_SUBLAYERS
POOL_WINDOWS = (2, 4, 8, 16)
POOL_HALO = 16
FOX_HEADS = 16
FOX_HEAD_DIM = 64
S5_GROUP = 16
S5_STATE = 64
S5_CHUNK = 128
CONV_HALO = 8
LANES = 128
NEG_BIG = -1e30
LOG2E = math.log2(math.e)
AUG_LANES = 8
V_ROWS = FOX_HEAD_DIM + 16
FOX_TILE = 512

VMEM_LIMIT = 56 * 1024 * 1024


def _params(n_grid, vmem=VMEM_LIMIT):
    return pltpu.CompilerParams(dimension_semantics=("arbitrary",) * n_grid,
                                vmem_limit_bytes=vmem)


def _dot(a, b):
    return jnp.dot(a, b, preferred_element_type=F32)


def _adaln(x, gain, shift, scale):
    ms = jnp.mean(x * x, axis=-1, keepdims=True)
    y = x * lax.rsqrt(ms + NORM_EPS) * gain
    return y * (1.0 + scale) + shift


def _mod_rows(mod_ref, sub):
    m = mod_ref[0]
    return m[3 * sub:3 * sub + 1], m[3 * sub + 1:3 * sub + 2], m[3 * sub + 2:3 * sub + 3]


def _const_spec(shape):
    nd = len(shape)
    return pl.BlockSpec(shape, lambda *_: (0,) * nd, pipeline_mode=pl.Buffered(1))


def _tile_specs(tm, d):
    x_spec = pl.BlockSpec((1, tm, d), lambda b, t: (b, t, 0))
    mod_spec = pl.BlockSpec((1, N_ADA, d), lambda b, t: (b, 0, 0))
    return x_spec, mod_spec


def _mod_kernel(c_ref, w_ref, b_ref, o_ref):
    c = c_ref[...]
    cond = (c * jax.nn.sigmoid(c)).astype(BF16)
    o_ref[0] = _dot(cond, w_ref[0].astype(BF16)) + b_ref[0]


def _modulation(c, ada_w, ada_b):
    depth, d, n = ada_w.shape
    b = c.shape[0]
    tn = 1024
    return pl.pallas_call(
        _mod_kernel,
        out_shape=jax.ShapeDtypeStruct((depth, b, n), F32),
        grid=(depth, n // tn),
        in_specs=[pl.BlockSpec((b, d), lambda l, j: (0, 0)),
                  pl.BlockSpec((1, d, tn), lambda l, j: (l, 0, j)),
                  pl.BlockSpec((1, 1, tn), lambda l, j: (l, 0, j))],
        out_specs=pl.BlockSpec((1, b, tn), lambda l, j: (l, 0, j)),
        compiler_params=_params(2),
    )(c, ada_w, ada_b.reshape(depth, 1, n))


def _ffn_kernel(sub, tf, x_ref, mod_ref, g_ref, win_ref, wout_ref, o_ref, h_sc, a_sc):
    x = x_ref[0]
    shift, scale, gate = _mod_rows(mod_ref, sub)
    h_sc[...] = _adaln(x, g_ref[...], shift, scale).astype(BF16)
    dff = a_sc.shape[1]
    for j in range(dff // tf):
        h = h_sc[...]
        g = _dot(h, win_ref[:, j * tf:(j + 1) * tf])
        u = _dot(h, win_ref[:, dff + j * tf:dff + (j + 1) * tf])
        a_sc[:, j * tf:(j + 1) * tf] = (g * jax.nn.sigmoid(g) * u).astype(BF16)
    y = _dot(a_sc[...], wout_ref[...])
    o_ref[0] = x + (0.5 * (1.0 + gate)) * y


def _ffn(x, mod, gain, w_in, w_out, sub, tm=512, tf=256):
    b, s, d = x.shape
    dff = w_out.shape[0]
    x_spec, mod_spec = _tile_specs(tm, d)
    return pl.pallas_call(
        functools.partial(_ffn_kernel, sub, tf),
        out_shape=jax.ShapeDtypeStruct(x.shape, F32),
        grid=(b, s // tm),
        in_specs=[x_spec, mod_spec, _const_spec((1, d)),
                  _const_spec((d, 2 * dff)), _const_spec((dff, d))],
        out_specs=x_spec,
        scratch_shapes=[pltpu.VMEM((tm, d), BF16), pltpu.VMEM((tm, dff), BF16)],
        compiler_params=_params(2),
    )(x, mod, gain.reshape(1, d), w_in.astype(BF16), w_out.astype(BF16))


def _pool_kernel(tm, x_ref, mod_ref, g_ref, w_ref, sc_ref, o_ref, h_sc):
    t = pl.program_id(1)
    x = x_ref[0]
    d = x.shape[1]
    cg = d // len(POOL_WINDOWS)
    shift, scale, gate = _mod_rows(mod_ref, 1)

    @pl.when(t == 0)
    def _():
        h_sc[0:POOL_HALO, :] = jnp.zeros((POOL_HALO, d), F32)

    h_sc[POOL_HALO:, :] = _adaln(x, g_ref[...], shift, scale)
    pos = t * tm + lax.broadcasted_iota(jnp.int32, (tm, 1), 0)
    ys = []
    for gi, w in enumerate(POOL_WINDOWS):
        sl = slice(gi * cg, (gi + 1) * cg)
        cur = h_sc[POOL_HALO:, sl]
        acc = cur
        for k in range(1, w):
            acc = acc + h_sc[POOL_HALO - k:POOL_HALO - k + tm, sl]
        cnt = jnp.minimum(pos + 1, w).astype(F32)
        pooled = acc / cnt - cur
        ys.append(_dot(pooled.astype(BF16), w_ref[gi]))
    y = jnp.concatenate(ys, axis=1) * sc_ref[...]
    h_sc[0:POOL_HALO, :] = h_sc[tm:tm + POOL_HALO, :]
    o_ref[0] = x + (1.0 + gate) * y


def _pool_layer(x, mod, gain, w_grp, scale, tm=512):
    b, s, d = x.shape
    ng, cg, _ = w_grp.shape
    x_spec, mod_spec = _tile_specs(tm, d)
    return pl.pallas_call(
        functools.partial(_pool_kernel, tm),
        out_shape=jax.ShapeDtypeStruct(x.shape, F32),
        grid=(b, s // tm),
        in_specs=[x_spec, mod_spec, _const_spec((1, d)),
                  _const_spec((ng, cg, cg)), _const_spec((1, d))],
        out_specs=x_spec,
        scratch_shapes=[pltpu.VMEM((tm + POOL_HALO, d), F32)],
        compiler_params=_params(2),
    )(x, mod, gain.reshape(1, d), w_grp.astype(BF16), scale.reshape(1, d))


def _split3(v):
    p1 = v.astype(BF16)
    r1 = v - p1.astype(F32)
    p2 = r1.astype(BF16)
    p3 = (r1 - p2.astype(F32)).astype(BF16)
    return p1, p2, p3


def _head_rms(q, seg_ref, segt_ref):
    ms = _dot((q * q).astype(BF16), seg_ref[...])
    r = lax.rsqrt(ms + NORM_EPS)
    r_hi = r.astype(BF16)
    r_lo = (r - r_hi.astype(F32)).astype(BF16)
    rb = _dot(r_hi, segt_ref[...]) + _dot(r_lo, segt_ref[...])
    return q * rb


def _foxproj_kernel(x_ref, mod_ref, g_ref, wq_ref, wk_ref, wv_ref, wf_ref, bf_ref, qg_ref, kg_ref,
                    seg_ref, segt_ref, tri_ref, place_ref, ones_ref,
                    q_ref, k_ref, v_ref, qa_ref, ka_ref, carry_sc):
    t = pl.program_id(1)
    x = x_ref[0]
    shift, scale, _ = _mod_rows(mod_ref, 1)
    h = _adaln(x, g_ref[...], shift, scale).astype(BF16)
    q = _head_rms(_dot(h, wq_ref[...]), seg_ref, segt_ref)
    q_ref[0] = (q * qg_ref[...]).astype(BF16)
    k = _head_rms(_dot(h, wk_ref[...]), seg_ref, segt_ref)
    k_ref[0] = (k * kg_ref[...]).astype(BF16)
    v_ref[0, 0] = lax.dot_general(wv_ref[...], h, (((1,), (1,)), ((), ())),
                                  preferred_element_type=F32).astype(BF16)
    fl = _dot(h, wf_ref[...]) + bf_ref[...]
    logf = jnp.minimum(fl, 0.0) - jnp.log(1.0 + jnp.exp(-jnp.abs(fl)))

    @pl.when(t == 0)
    def _():
        carry_sc[...] = jnp.zeros_like(carry_sc)

    tri = tri_ref[...]
    cf = sum(_dot(tri, p) for p in _split3(logf)) + carry_sc[0:1, :]
    tm = cf.shape[0]
    carry_sc[0:1, :] = cf[tm - 1:tm, :]
    pieces = jnp.concatenate(_split3(cf * LOG2E), axis=1)
    aug = _dot(pieces, place_ref[...]) + ones_ref[...]
    qa_ref[0] = aug[:, :LANES].astype(BF16)
    ka_ref[0] = aug[:, LANES:].astype(BF16)


def _fox_project(x, mod, gain, w_in, b_f, q_gain, k_gain, tm=FOX_TILE):
    b, s, d = x.shape
    x_spec, mod_spec = _tile_specs(tm, d)
    w = w_in.astype(BF16)
    wf = jnp.zeros((d, LANES), BF16).at[:, :FOX_HEADS].set(w[:, 3 * d:])
    bf = jnp.zeros((1, LANES), F32).at[0, :FOX_HEADS].set(b_f)
    head_of = jnp.arange(d) // FOX_HEAD_DIM
    seg = (head_of[:, None] == jnp.arange(LANES)[None, :]).astype(F32)
    qg = jnp.tile(q_gain, FOX_HEADS).reshape(1, d) * (FOX_HEAD_DIM ** -0.5 * LOG2E)
    kg = jnp.tile(k_gain, FOX_HEADS).reshape(1, d)
    tri = (jnp.arange(tm)[:, None] >= jnp.arange(tm)[None, :]).astype(BF16)
    hh, ii = jnp.meshgrid(jnp.arange(FOX_HEADS), jnp.arange(3), indexing="ij")
    place = jnp.zeros((3 * LANES, 2 * LANES), F32)
    place = place.at[ii * LANES + hh, AUG_LANES * hh + 3 + ii].set(1.0)
    place = place.at[ii * LANES + hh, LANES + AUG_LANES * hh + ii].set(-1.0)
    ones = jnp.zeros((1, 2 * LANES), F32)
    ones = ones.at[0, AUG_LANES * hh + ii].set(1.0).at[0, LANES + AUG_LANES * hh + 3 + ii].set(1.0)
    act = jax.ShapeDtypeStruct((b, s, d), BF16)
    aug = jax.ShapeDtypeStruct((b, s, LANES), BF16)
    act_t = jax.ShapeDtypeStruct((b, s // tm, d, tm), BF16)
    return pl.pallas_call(
        _foxproj_kernel,
        out_shape=(act, act, act_t, aug, aug),
        grid=(b, s // tm),
        in_specs=[x_spec, mod_spec, _const_spec((1, d)),
                  _const_spec((d, d)), _const_spec((d, d)), _const_spec((d, d)),
                  _const_spec((d, LANES)), _const_spec((1, LANES)),
                  _const_spec((1, d)), _const_spec((1, d)),
                  _const_spec((d, LANES)), _const_spec((LANES, d)), _const_spec((tm, tm)),
                  _const_spec((3 * LANES, 2 * LANES)), _const_spec((1, 2 * LANES))],
        out_specs=(pl.BlockSpec((1, tm, d), lambda b_, t: (b_, t, 0)),) * 2
        + (pl.BlockSpec((1, 1, d, tm), lambda b_, t: (b_, t, 0, 0)),)
        + (pl.BlockSpec((1, tm, LANES), lambda b_, t: (b_, t, 0)),) * 2,
        scratch_shapes=[pltpu.VMEM((8, LANES), F32)],
        compiler_params=_params(2),
    )(x, mod, gain.reshape(1, d), w[:, :d], w[:, d:2 * d], w[:, 2 * d:3 * d].T, wf, bf, qg, kg,
      (seg / FOX_HEAD_DIM).astype(BF16), seg.T.astype(BF16), tri, place.astype(BF16), ones)


def _foxattn_kernel(T, q_ref, qa_ref, k_ref, ka_ref, vt_ref, o_ref,
                    sa_sc, sb_sc, qt_sc, m_sc, acc_sc, diff_sc):
    pair = pl.program_id(1)
    qi = pl.program_id(2)
    lane = lax.broadcasted_iota(jnp.int32, (1, LANES), 1)
    q, qa = q_ref[0], qa_ref[0]
    zero = jnp.zeros_like(q)
    for hd in range(2):
        own = (lane < FOX_HEAD_DIM) == (hd == 0)
        bias_lo = AUG_LANES * (2 * pair + hd)
        own_bias = (lane >= bias_lo) & (lane < bias_lo + AUG_LANES)
        qh = jnp.concatenate([jnp.where(own, q, zero), jnp.where(own_bias, qa, zero)], axis=1)
        qt_sc[hd] = qh.astype(F32).T.astype(BF16)
        m_sc[hd] = jnp.full((8, T), NEG_BIG, F32)
        acc_sc[hd] = jnp.zeros((V_ROWS, T), F32)
    diff_sc[...] = (lax.broadcasted_iota(jnp.int32, (T, T), 0)
                    - lax.broadcasted_iota(jnp.int32, (T, T), 1))

    def scores(j, s_sc):
        rows = pl.ds(pl.multiple_of(j * T, T), T)
        kj = jnp.concatenate([k_ref[0, rows, :], ka_ref[0, rows, :]], axis=1)
        for hd in range(2):
            s_sc[hd] = _dot(kj, qt_sc[hd])

    def absorb(j, s_sc, masked):
        vt = vt_ref[0, j]
        ones = jnp.ones((V_ROWS - FOX_HEAD_DIM, T), BF16)
        if masked:
            visible = diff_sc[...] <= (qi - j) * T
        for hd in range(2):
            s = s_sc[hd]
            if masked:
                s = jnp.where(visible, s, NEG_BIG)
            m_prev = m_sc[hd]
            m_new = jnp.maximum(m_prev, jnp.max(s, axis=0, keepdims=True))
            p = jnp.exp2(s - m_new[0:1, :]).astype(BF16)
            v_aug = jnp.concatenate([vt[hd * FOX_HEAD_DIM:(hd + 1) * FOX_HEAD_DIM, :], ones], axis=0)
            alpha = jnp.exp2(m_prev - m_new)[0:1, :]
            acc_sc[hd] = alpha * acc_sc[hd] + _dot(v_aug, p)
            m_sc[hd] = m_new

    scores(0, sa_sc)

    def two_tiles(jj, carry):
        j = 2 * jj
        scores(j + 1, sb_sc)
        absorb(j, sa_sc, False)
        scores(j + 2, sa_sc)
        absorb(j + 1, sb_sc, False)
        return carry

    lax.fori_loop(0, qi // 2, two_tiles, 0)

    @pl.when(qi % 2 == 0)
    def _():
        absorb(qi, sa_sc, True)

    @pl.when(qi % 2 == 1)
    def _():
        scores(qi, sb_sc)
        absorb(qi - 1, sa_sc, False)
        absorb(qi, sb_sc, True)

    out_t = jnp.concatenate(
        [acc_sc[hd, 0:FOX_HEAD_DIM, :] / acc_sc[hd, FOX_HEAD_DIM:FOX_HEAD_DIM + 1, :] for hd in range(2)],
        axis=0)
    o_ref[0] = out_t.T.astype(o_ref.dtype)


def _fox_attend(q, qa, k, ka, vt, tile=FOX_TILE):
    b, s, d = q.shape
    pairs = d // LANES
    n_tiles = s // tile
    assert vt.shape == (b, n_tiles, d, tile)
    return pl.pallas_call(
        functools.partial(_foxattn_kernel, tile),
        out_shape=jax.ShapeDtypeStruct((b, s, d), BF16),
        grid=(b, pairs, n_tiles),
        in_specs=[pl.BlockSpec((1, tile, LANES), lambda b_, p, i: (b_, i, p)),
                  pl.BlockSpec((1, tile, LANES), lambda b_, p, i: (b_, i, 0)),
                  pl.BlockSpec((1, s, LANES), lambda b_, p, i: (b_, 0, p)),
                  pl.BlockSpec((1, s, LANES), lambda b_, p, i: (b_, 0, 0)),
                  pl.BlockSpec((1, n_tiles, LANES, tile), lambda b_, p, i: (b_, 0, p, 0))],
        out_specs=pl.BlockSpec((1, tile, LANES), lambda b_, p, i: (b_, i, p)),
        scratch_shapes=[pltpu.VMEM((2, tile, tile), F32), pltpu.VMEM((2, tile, tile), F32),
                        pltpu.VMEM((2, 2 * LANES, tile), BF16),
                        pltpu.VMEM((2, 8, tile), F32), pltpu.VMEM((2, V_ROWS, tile), F32),
                        pltpu.VMEM((tile, tile), jnp.int32)],
        compiler_params=_params(3),
    )(q, qa, k, ka, vt)


def _proj_out_kernel(a_ref, x_ref, mod_ref, w_ref, o_ref):
    _, _, gate = _mod_rows(mod_ref, 1)
    o_ref[0] = x_ref[0] + (1.0 + gate) * _dot(a_ref[0], w_ref[...])


def _proj_out(a, x, mod, w, tm=512):
    b, s, d = x.shape
    x_spec, mod_spec = _tile_specs(tm, d)
    return pl.pallas_call(
        _proj_out_kernel,
        out_shape=jax.ShapeDtypeStruct(x.shape, F32),
        grid=(b, s // tm),
        in_specs=[x_spec, x_spec, mod_spec, _const_spec((d, d))],
        out_specs=x_spec,
        compiler_params=_params(2),
    )(a, x, mod, w.astype(BF16))


def _cmul(a, t):
    w = a.shape[1] // 2
    ar, ai, tr, ti = a[:, :w], a[:, w:], t[:, :w], t[:, w:]
    return jnp.concatenate([ar * tr - ai * ti, ar * ti + ai * tr], axis=1)


def _s5_kernel(tm, x_ref, mod_ref, g_ref, bm_ref, cm_ref, e_ref, f_ref, a_ref, tri_ref, dskip_ref,
               wglu_ref, o_ref, u_sc, y_sc, s_sc):
    t = pl.program_id(1)
    x = x_ref[0]
    shift, scale, gate = _mod_rows(mod_ref, 1)
    u_sc[...] = _adaln(x, g_ref[...], shift, scale)
    L = S5_CHUNK
    n_blocks = bm_ref.shape[0]

    @pl.when(t == 0)
    def _():
        s_sc[...] = jnp.zeros_like(s_sc)

    tri = tri_ref[...]
    chunks = [slice(c * L, (c + 1) * L) for c in range(tm // L)]
    for kb in range(n_blocks):
        cols = slice(kb * LANES, (kb + 1) * LANES)
        bu = _dot(u_sc[:, cols].astype(BF16), bm_ref[kb])
        sums = [_dot(tri, _cmul(bu[rows], e_ref[kb]).astype(BF16)) for rows in chunks]
        f_last = f_ref[kb, L - 1:L, :]
        carried = [s_sc[kb:kb + 1, :]]
        for p in sums:
            carried.append(_cmul(_cmul(p[L - 1:L, :] + carried[-1], f_last), a_ref[kb]))
        s_sc[kb:kb + 1, :] = carried[-1]
        xs = jnp.concatenate([_cmul(p + s, f_ref[kb]).astype(BF16) for p, s in zip(sums, carried)],
                             axis=0)
        y_sc[:, cols] = _dot(xs, cm_ref[kb])
    y = y_sc[...] + dskip_ref[...] * u_sc[...]
    g = jax.nn.gelu(y, approximate=True)
    out = g * jax.nn.sigmoid(_dot(g.astype(BF16), wglu_ref[...]))
    o_ref[0] = x + (1.0 + gate) * out


def _s5_tables(lam_re, lam_im, log_dt, b_re, b_im, c_re, c_im):
    g_, n_ = lam_re.shape
    i_ = b_re.shape[2]
    gl = LANES // i_
    nb = g_ // gl
    dt = jnp.exp(log_dt)[:, None]
    ar, ai = lam_re, lam_im
    mag = jnp.exp(ar * dt)
    lb_re, lb_im = mag * jnp.cos(ai * dt), mag * jnp.sin(ai * dt)
    den = ar * ar + ai * ai
    nr, ni = lb_re - 1.0, lb_im
    k_re = (nr * ar + ni * ai) / den
    k_im = (ni * ar - nr * ai) / den
    bb_re = k_re[..., None] * b_re - k_im[..., None] * b_im
    bb_im = k_re[..., None] * b_im + k_im[..., None] * b_re
    eye = jnp.eye(gl, dtype=F32)
    bb = jnp.stack([bb_re, bb_im]).reshape(2, nb, gl, n_, i_)
    bm = jnp.einsum('gh,pkgni->kgiphn', eye, bb).reshape(nb, gl * i_, 2 * gl * n_)
    cc = jnp.stack([c_re, -c_im]).reshape(2, nb, gl, i_, n_)
    cm = jnp.einsum('gh,pkgin->kpgnhi', eye, cc).reshape(nb, 2 * gl * n_, gl * i_)

    def lbar_pow(p):
        e = jnp.exp(p[:, None, None] * (ar * dt)[None])
        th = p[:, None, None] * (ai * dt)[None]
        t = jnp.stack([e * jnp.cos(th), e * jnp.sin(th)], axis=1)
        return t.reshape(-1, 2, nb, gl, n_).transpose(2, 0, 1, 3, 4).reshape(nb, -1, 2 * gl * n_)

    mid = S5_CHUNK // 2
    j = jnp.arange(S5_CHUNK, dtype=F32) - mid
    return (bm.astype(BF16), cm.astype(BF16), lbar_pow(-j), lbar_pow(j),
            lbar_pow(jnp.full((1,), mid + 1.0, F32)))


def _s5_layer(x, mod, gain, lam_re, lam_im, log_dt, b_re, b_im, c_re, c_im, d_skip, w_glu, tm=512):
    b, s, d = x.shape
    bm, cm, e_tab, f_tab, a_tab = _s5_tables(lam_re, lam_im, log_dt, b_re, b_im, c_re, c_im)
    nb, _, sw = bm.shape
    L = S5_CHUNK
    tri = (jnp.arange(L)[:, None] >= jnp.arange(L)[None, :]).astype(BF16)
    x_spec, mod_spec = _tile_specs(tm, d)
    return pl.pallas_call(
        functools.partial(_s5_kernel, tm),
        out_shape=jax.ShapeDtypeStruct(x.shape, F32),
        grid=(b, s // tm),
        in_specs=[x_spec, mod_spec, _const_spec((1, d)),
                  _const_spec((nb, LANES, sw)), _const_spec((nb, sw, LANES)),
                  _const_spec((nb, L, sw)), _const_spec((nb, L, sw)), _const_spec((nb, 1, sw)),
                  _const_spec((L, L)), _const_spec((1, d)), _const_spec((d, d))],
        out_specs=x_spec,
        scratch_shapes=[pltpu.VMEM((tm, d), F32), pltpu.VMEM((tm, d), F32), pltpu.VMEM((nb, sw), F32)],
        compiler_params=_params(2),
    )(x, mod, gain.reshape(1, d), bm, cm, e_tab, f_tab, a_tab, tri, d_skip.reshape(1, d),
      w_glu.astype(BF16))


def _conv_kernel(tm, x_ref, mod_ref, g_ref, win_ref, cw_ref, wout_ref, o_ref, cz_sc):
    t = pl.program_id(1)
    x = x_ref[0]
    d = x.shape[1]
    shift, scale, gate = _mod_rows(mod_ref, 1)
    h = _adaln(x, g_ref[...], shift, scale).astype(BF16)

    @pl.when(t == 0)
    def _():
        cz_sc[0:CONV_HALO, :] = jnp.zeros((CONV_HALO, d), F32)

    cz_sc[CONV_HALO:, :] = _dot(h, win_ref[:, d:2 * d]) * _dot(h, win_ref[:, 2 * d:3 * d])
    cw = cw_ref[...]
    conv = (cw[0:1] * cz_sc[CONV_HALO - 2:CONV_HALO - 2 + tm, :]
            + cw[1:2] * cz_sc[CONV_HALO - 1:CONV_HALO - 1 + tm, :]
            + cw[2:3] * cz_sc[CONV_HALO:, :])
    cz_sc[0:CONV_HALO, :] = cz_sc[tm:tm + CONV_HALO, :]
    gated = (_dot(h, win_ref[:, 0:d]) * conv).astype(BF16)
    o_ref[0] = x + (1.0 + gate) * _dot(gated, wout_ref[...])


def _conv_layer(x, mod, gain, w_in, conv_w, w_out, tm=512):
    b, s, d = x.shape
    kw = conv_w.shape[0]
    x_spec, mod_spec = _tile_specs(tm, d)
    return pl.pallas_call(
        functools.partial(_conv_kernel, tm),
        out_shape=jax.ShapeDtypeStruct(x.shape, F32),
        grid=(b, s // tm),
        in_specs=[x_spec, mod_spec, _const_spec((1, d)),
                  _const_spec((d, 3 * d)), _const_spec((kw, d)), _const_spec((d, d))],
        out_specs=x_spec,
        scratch_shapes=[pltpu.VMEM((tm + CONV_HALO, d), F32)],
        compiler_params=_params(2),
    )(x, mod, gain.reshape(1, d), w_in.astype(BF16), conv_w.reshape(kw, d), w_out.astype(BF16))


def kernel(x, c, ada_w, ada_b, norm_g, ffn_w_in, ffn_w_out, pool_w, pool_scale, fox_w_in, fox_b_f, fox_q_gain, fox_k_gain, fox_w_o, s5_lam_re, s5_lam_im, s5_log_dt, s5_b_re, s5_b_im, s5_c_re, s5_c_im, s5_d, s5_w_glu, conv_w_in, conv_w, conv_w_out):
    b, s, d = x.shape
    depth = ada_w.shape[0]
    mod_all = _modulation(c, ada_w, ada_b).reshape(depth, b, N_ADA, d)
    n_mixers = 4
    for i in range(depth):
        mod = mod_all[i]
        x = _ffn(x, mod, norm_g[i, 0], ffn_w_in[i, 0], ffn_w_out[i, 0], sub=0)
        m, r = i % n_mixers, i // n_mixers
        if m == 0:
            x = _pool_layer(x, mod, norm_g[i, 1], pool_w[r], pool_scale[r])
        elif m == 1:
            q, k, vt, qa, ka = _fox_project(x, mod, norm_g[i, 1], fox_w_in[r], fox_b_f[r],
                                            fox_q_gain[r], fox_k_gain[r])
            o = _fox_attend(q, qa, k, ka, vt)
            x = _proj_out(o, x, mod, fox_w_o[r])
        elif m == 2:
            x = _s5_layer(x, mod, norm_g[i, 1], s5_lam_re[r], s5_lam_im[r], s5_log_dt[r],
                          s5_b_re[r], s5_b_im[r], s5_c_re[r], s5_c_im[r], s5_d[r], s5_w_glu[r])
        else:
            x = _conv_layer(x, mod, norm_g[i, 1], conv_w_in[r], conv_w[r], conv_w_out[r])
        x = _ffn(x, mod, norm_g[i, 2], ffn_w_in[i, 1], ffn_w_out[i, 1], sub=2)
    return x
```

```python
import functools
import math

import jax
import jax.numpy as jnp
from jax import lax
from jax.experimental import pallas as pl
from jax.experimental.pallas import tpu as pltpu

F32 = jnp.float32
BF16 = jnp.bfloat16

NORM_EPS = 1e-6
N_SUBLAYERS = 3
N_ADA = 3 * N_SUBLAYERS
POOL_WINDOWS = (2, 4, 8, 16)
POOL_HALO = 16
FOX_HEADS = 16
FOX_HEAD_DIM = 64
S5_GROUP = 16
S5_STATE = 64
S5_CHUNK = 128
CONV_HALO = 8
LANES = 128
NEG_BIG = -1e30
LOG2E = math.log2(math.e)
AUG_LANES = 8
V_ROWS = FOX_HEAD_DIM + 16
FOX_TILE = 512

VMEM_LIMIT = 56 * 1024 * 1024


def _params(n_grid, vmem=VMEM_LIMIT):
    return pltpu.CompilerParams(dimension_semantics=("arbitrary",) * n_grid,
                                vmem_limit_bytes=vmem)


def _dot(a, b):
    return jnp.dot(a, b, preferred_element_type=F32)


def _adaln(x, gain, shift, scale):
    ms = jnp.mean(x * x, axis=-1, keepdims=True)
    y = x * lax.rsqrt(ms + NORM_EPS) * gain
    return y * (1.0 + scale) + shift


def _mod_rows(mod_ref, sub):
    m = mod_ref[0]
    return m[3 * sub:3 * sub + 1], m[3 * sub + 1:3 * sub + 2], m[3 * sub + 2:3 * sub + 3]


def _const_spec(shape):
    nd = len(shape)
    return pl.BlockSpec(shape, lambda *_: (0,) * nd, pipeline_mode=pl.Buffered(1))


def _tile_specs(tm, d):
    x_spec = pl.BlockSpec((1, tm, d), lambda b, t: (b, t, 0))
    mod_spec = pl.BlockSpec((1, N_ADA, d), lambda b, t: (b, 0, 0))
    return x_spec, mod_spec


def _mod_kernel(c_ref, w_ref, b_ref, o_ref):
    c = c_ref[...]
    cond = (c * jax.nn.sigmoid(c)).astype(BF16)
    o_ref[0] = _dot(cond, w_ref[0].astype(BF16)) + b_ref[0]


def _modulation(c, ada_w, ada_b):
    depth, d, n = ada_w.shape
    b = c.shape[0]
    tn = 1024
    return pl.pallas_call(
        _mod_kernel,
        out_shape=jax.ShapeDtypeStruct((depth, b, n), F32),
        grid=(depth, n // tn),
        in_specs=[pl.BlockSpec((b, d), lambda l, j: (0, 0)),
                  pl.BlockSpec((1, d, tn), lambda l, j: (l, 0, j)),
                  pl.BlockSpec((1, 1, tn), lambda l, j: (l, 0, j))],
        out_specs=pl.BlockSpec((1, b, tn), lambda l, j: (l, 0, j)),
        compiler_params=_params(2),
    )(c, ada_w, ada_b.reshape(depth, 1, n))


def _ffn_kernel(sub, tf, x_ref, mod_ref, g_ref, win_ref, wout_ref, o_ref, h_sc, a_sc):
    x = x_ref[0]
    shift, scale, gate = _mod_rows(mod_ref, sub)
    h_sc[...] = _adaln(x, g_ref[...], shift, scale).astype(BF16)
    dff = a_sc.shape[1]
    for j in range(dff // tf):
        h = h_sc[...]
        g = _dot(h, win_ref[:, j * tf:(j + 1) * tf])
        u = _dot(h, win_ref[:, dff + j * tf:dff + (j + 1) * tf])
        a_sc[:, j * tf:(j + 1) * tf] = (g * jax.nn.sigmoid(g) * u).astype(BF16)
    y = _dot(a_sc[...], wout_ref[...])
    o_ref[0] = x + (0.5 * (1.0 + gate)) * y


def _ffn(x, mod, gain, w_in, w_out, sub, tm=512, tf=256):
    b, s, d = x.shape
    dff = w_out.shape[0]
    x_spec, mod_spec = _tile_specs(tm, d)
    return pl.pallas_call(
        functools.partial(_ffn_kernel, sub, tf),
        out_shape=jax.ShapeDtypeStruct(x.shape, F32),
        grid=(b, s // tm),
        in_specs=[x_spec, mod_spec, _const_spec((1, d)),
                  _const_spec((d, 2 * dff)), _const_spec((dff, d))],
        out_specs=x_spec,
        scratch_shapes=[pltpu.VMEM((tm, d), BF16), pltpu.VMEM((tm, dff), BF16)],
        compiler_params=_params(2),
    )(x, mod, gain.reshape(1, d), w_in.astype(BF16), w_out.astype(BF16))


def _pool_kernel(tm, x_ref, mod_ref, g_ref, w_ref, sc_ref, o_ref, h_sc):
    t = pl.program_id(1)
    x = x_ref[0]
    d = x.shape[1]
    cg = d // len(POOL_WINDOWS)
    shift, scale, gate = _mod_rows(mod_ref, 1)

    @pl.when(t == 0)
    def _():
        h_sc[0:POOL_HALO, :] = jnp.zeros((POOL_HALO, d), F32)

    h_sc[POOL_HALO:, :] = _adaln(x, g_ref[...], shift, scale)
    pos = t * tm + lax.broadcasted_iota(jnp.int32, (tm, 1), 0)
    ys = []
    for gi, w in enumerate(POOL_WINDOWS):
        sl = slice(gi * cg, (gi + 1) * cg)
        cur = h_sc[POOL_HALO:, sl]
        acc = cur
        for k in range(1, w):
            acc = acc + h_sc[POOL_HALO - k:POOL_HALO - k + tm, sl]
        cnt = jnp.minimum(pos + 1, w).astype(F32)
        pooled = acc / cnt - cur
        ys.append(_dot(pooled.astype(BF16), w_ref[gi]))
    y = jnp.concatenate(ys, axis=1) * sc_ref[...]
    h_sc[0:POOL_HALO, :] = h_sc[tm:tm + POOL_HALO, :]
    o_ref[0] = x + (1.0 + gate) * y


def _pool_layer(x, mod, gain, w_grp, scale, tm=512):
    b, s, d = x.shape
    ng, cg, _ = w_grp.shape
    x_spec, mod_spec = _tile_specs(tm, d)
    return pl.pallas_call(
        functools.partial(_pool_kernel, tm),
        out_shape=jax.ShapeDtypeStruct(x.shape, F32),
        grid=(b, s // tm),
        in_specs=[x_spec, mod_spec, _const_spec((1, d)),
                  _const_spec((ng, cg, cg)), _const_spec((1, d))],
        out_specs=x_spec,
        scratch_shapes=[pltpu.VMEM((tm + POOL_HALO, d), F32)],
        compiler_params=_params(2),
    )(x, mod, gain.reshape(1, d), w_grp.astype(BF16), scale.reshape(1, d))


def _split3(v):
    p1 = v.astype(BF16)
    r1 = v - p1.astype(F32)
    p2 = r1.astype(BF16)
    p3 = (r1 - p2.astype(F32)).astype(BF16)
    return p1, p2, p3


def _head_rms(q, seg_ref, segt_ref):
    ms = _dot((q * q).astype(BF16), seg_ref[...])
    r = lax.rsqrt(ms + NORM_EPS)
    r_hi = r.astype(BF16)
    r_lo = (r - r_hi.astype(F32)).astype(BF16)
    rb = _dot(r_hi, segt_ref[...]) + _dot(r_lo, segt_ref[...])
    return q * rb


def _foxproj_kernel(x_ref, mod_ref, g_ref, wq_ref, wk_ref, wv_ref, wf_ref, bf_ref, qg_ref, kg_ref,
                    seg_ref, segt_ref, tri_ref, place_ref, ones_ref,
                    q_ref, k_ref, v_ref, qa_ref, ka_ref, carry_sc):
    t = pl.program_id(1)
    x = x_ref[0]
    shift, scale, _ = _mod_rows(mod_ref, 1)
    h = _adaln(x, g_ref[...], shift, scale).astype(BF16)
    q = _head_rms(_dot(h, wq_ref[...]), seg_ref, segt_ref)
    q_ref[0] = (q * qg_ref[...]).astype(BF16)
    k = _head_rms(_dot(h, wk_ref[...]), seg_ref, segt_ref)
    k_ref[0] = (k * kg_ref[...]).astype(BF16)
    v_ref[0, 0] = lax.dot_general(wv_ref[...], h, (((1,), (1,)), ((), ())),
                                  preferred_element_type=F32).astype(BF16)
    fl = _dot(h, wf_ref[...]) + bf_ref[...]
    logf = jnp.minimum(fl, 0.0) - jnp.log(1.0 + jnp.exp(-jnp.abs(fl)))

    @pl.when(t == 0)
    def _():
        carry_sc[...] = jnp.zeros_like(carry_sc)

    tri = tri_ref[...]
    cf = sum(_dot(tri, p) for p in _split3(logf)) + carry_sc[0:1, :]
    tm = cf.shape[0]
    carry_sc[0:1, :] = cf[tm - 1:tm, :]
    pieces = jnp.concatenate(_split3(cf * LOG2E), axis=1)
    aug = _dot(pieces, place_ref[...]) + ones_ref[...]
    qa_ref[0] = aug[:, :LANES].astype(BF16)
    ka_ref[0] = aug[:, LANES:].astype(BF16)


def _fox_project(x, mod, gain, w_in, b_f, q_gain, k_gain, tm=FOX_TILE):
    b, s, d = x.shape
    x_spec, mod_spec = _tile_specs(tm, d)
    w = w_in.astype(BF16)
    wf = jnp.zeros((d, LANES), BF16).at[:, :FOX_HEADS].set(w[:, 3 * d:])
    bf = jnp.zeros((1, LANES), F32).at[0, :FOX_HEADS].set(b_f)
    head_of = jnp.arange(d) // FOX_HEAD_DIM
    seg = (head_of[:, None] == jnp.arange(LANES)[None, :]).astype(F32)
    qg = jnp.tile(q_gain, FOX_HEADS).reshape(1, d) * (FOX_HEAD_DIM ** -0.5 * LOG2E)
    kg = jnp.tile(k_gain, FOX_HEADS).reshape(1, d)
    tri = (jnp.arange(tm)[:, None] >= jnp.arange(tm)[None, :]).astype(BF16)
    hh, ii = jnp.meshgrid(jnp.arange(FOX_HEADS), jnp.arange(3), indexing="ij")
    place = jnp.zeros((3 * LANES, 2 * LANES), F32)
    place = place.at[ii * LANES + hh, AUG_LANES * hh + 3 + ii].set(1.0)
    place = place.at[ii * LANES + hh, LANES + AUG_LANES * hh + ii].set(-1.0)
    ones = jnp.zeros((1, 2 * LANES), F32)
    ones = ones.at[0, AUG_LANES * hh + ii].set(1.0).at[0, LANES + AUG_LANES * hh + 3 + ii].set(1.0)
    act = jax.ShapeDtypeStruct((b, s, d), BF16)
    aug = jax.ShapeDtypeStruct((b, s, LANES), BF16)
    act_t = jax.ShapeDtypeStruct((b, s // tm, d, tm), BF16)
    return pl.pallas_call(
        _foxproj_kernel,
        out_shape=(act, act, act_t, aug, aug),
        grid=(b, s // tm),
        in_specs=[x_spec, mod_spec, _const_spec((1, d)),
                  _const_spec((d, d)), _const_spec((d, d)), _const_spec((d, d)),
                  _const_spec((d, LANES)), _const_spec((1, LANES)),
                  _const_spec((1, d)), _const_spec((1, d)),
                  _const_spec((d, LANES)), _const_spec((LANES, d)), _const_spec((tm, tm)),
                  _const_spec((3 * LANES, 2 * LANES)), _const_spec((1, 2 * LANES))],
        out_specs=(pl.BlockSpec((1, tm, d), lambda b_, t: (b_, t, 0)),) * 2
        + (pl.BlockSpec((1, 1, d, tm), lambda b_, t: (b_, t, 0, 0)),)
        + (pl.BlockSpec((1, tm, LANES), lambda b_, t: (b_, t, 0)),) * 2,
        scratch_shapes=[pltpu.VMEM((8, LANES), F32)],
        compiler_params=_params(2),
    )(x, mod, gain.reshape(1, d), w[:, :d], w[:, d:2 * d], w[:, 2 * d:3 * d].T, wf, bf, qg, kg,
      (seg / FOX_HEAD_DIM).astype(BF16), seg.T.astype(BF16), tri, place.astype(BF16), ones)


def _foxattn_kernel(TK, TQ, q_ref, qa_ref, k_ref, ka_ref, vt_ref, o_ref,
                    sa_sc, sb_sc, qt_sc, m_sc, acc_sc, diff_sc):
    first_step = (pl.program_id(0) == 0) & (pl.program_id(1) == 0) & (pl.program_id(2) == 0)
    pair = pl.program_id(1)
    qi = pl.program_id(2)
    lane = lax.broadcasted_iota(jnp.int32, (1, LANES), 1)
    q, qa = q_ref[0], qa_ref[0]
    zero = jnp.zeros_like(q)
    for hd in range(2):
        own = (lane < FOX_HEAD_DIM) == (hd == 0)
        bias_lo = AUG_LANES * (2 * pair + hd)
        own_bias = (lane >= bias_lo) & (lane < bias_lo + AUG_LANES)
        qh = jnp.concatenate([jnp.where(own, q, zero), jnp.where(own_bias, qa, zero)], axis=1)
        qt_sc[hd] = qh.astype(F32).T.astype(BF16)
        m_sc[hd] = jnp.full((8, TQ), NEG_BIG, F32)
        acc_sc[hd] = jnp.zeros((V_ROWS, TQ), F32)

    @pl.when(first_step)
    def _():
        diff_sc[...] = (lax.broadcasted_iota(jnp.int32, (TK, TQ), 0)
                        - lax.broadcasted_iota(jnp.int32, (TK, TQ), 1))

    every = slice(0, TQ)
    early, late = slice(0, TK), slice(TK, TQ)

    def scores(j, s_sc, qs=every):
        rows = pl.ds(pl.multiple_of(j * TK, TK), TK)
        kj = jnp.concatenate([k_ref[0, rows, :], ka_ref[0, rows, :]], axis=1)
        for hd in range(2):
            s_sc[hd, :, qs] = _dot(kj, qt_sc[hd, :, qs])

    def absorb(j, s_sc, masked, qs=every):
        vt = vt_ref[0, j]
        ones = jnp.ones((V_ROWS - FOX_HEAD_DIM, TK), BF16)
        if masked:
            visible = diff_sc[:, qs] <= qi * TQ - j * TK
        for hd in range(2):
            s = s_sc[hd, :, qs]
            if masked:
                s = jnp.where(visible, s, NEG_BIG)
            m_prev = m_sc[hd, :, qs]
            m_new = jnp.maximum(m_prev, jnp.max(s, axis=0, keepdims=True))
            p = jnp.exp2(s - m_new[0:1, :]).astype(BF16)
            v_aug = jnp.concatenate([vt[hd * FOX_HEAD_DIM:(hd + 1) * FOX_HEAD_DIM, :], ones], axis=0)
            alpha = jnp.exp2(m_prev - m_new)[0:1, :]
            acc_sc[hd, :, qs] = alpha * acc_sc[hd, :, qs] + _dot(v_aug, p)
            m_sc[hd, :, qs] = m_new

    scores(0, sa_sc)

    def two_tiles(jj, carry):
        j = 2 * jj
        scores(j + 1, sb_sc)
        absorb(j, sa_sc, False)
        scores(j + 2, sa_sc)
        absorb(j + 1, sb_sc, False)
        return carry

    lax.fori_loop(0, qi, two_tiles, 0)
    scores(2 * qi + 1, sb_sc, late)
    absorb(2 * qi, sa_sc, True, early)
    absorb(2 * qi, sa_sc, False, late)
    absorb(2 * qi + 1, sb_sc, True, late)

    out_t = jnp.concatenate(
        [acc_sc[hd, 0:FOX_HEAD_DIM, :] / acc_sc[hd, FOX_HEAD_DIM:FOX_HEAD_DIM + 1, :] for hd in range(2)],
        axis=0)
    o_ref[0] = out_t.T.astype(o_ref.dtype)


def _fox_attend(q, qa, k, ka, vt, tk=FOX_TILE):
    b, s, d = q.shape
    pairs = d // LANES
    tq = 2 * tk
    assert vt.shape == (b, s // tk, d, tk)
    return pl.pallas_call(
        functools.partial(_foxattn_kernel, tk, tq),
        out_shape=jax.ShapeDtypeStruct((b, s, d), BF16),
        grid=(b, pairs, s // tq),
        in_specs=[pl.BlockSpec((1, tq, LANES), lambda b_, p, i: (b_, i, p)),
                  pl.BlockSpec((1, tq, LANES), lambda b_, p, i: (b_, i, 0)),
                  pl.BlockSpec((1, s, LANES), lambda b_, p, i: (b_, 0, p)),
                  pl.BlockSpec((1, s, LANES), lambda b_, p, i: (b_, 0, 0)),
                  pl.BlockSpec((1, s // tk, LANES, tk), lambda b_, p, i: (b_, 0, p, 0))],
        out_specs=pl.BlockSpec((1, tq, LANES), lambda b_, p, i: (b_, i, p)),
        scratch_shapes=[pltpu.VMEM((2, tk, tq), F32), pltpu.VMEM((2, tk, tq), F32),
                        pltpu.VMEM((2, 2 * LANES, tq), BF16),
                        pltpu.VMEM((2, 8, tq), F32), pltpu.VMEM((2, V_ROWS, tq), F32),
                        pltpu.VMEM((tk, tq), jnp.int32)],
        compiler_params=_params(3),
    )(q, qa, k, ka, vt)


def _proj_out_kernel(a_ref, x_ref, mod_ref, w_ref, o_ref):
    _, _, gate = _mod_rows(mod_ref, 1)
    o_ref[0] = x_ref[0] + (1.0 + gate) * _dot(a_ref[0], w_ref[...])


def _proj_out(a, x, mod, w, tm=512):
    b, s, d = x.shape
    x_spec, mod_spec = _tile_specs(tm, d)
    return pl.pallas_call(
        _proj_out_kernel,
        out_shape=jax.ShapeDtypeStruct(x.shape, F32),
        grid=(b, s // tm),
        in_specs=[x_spec, x_spec, mod_spec, _const_spec((d, d))],
        out_specs=x_spec,
        compiler_params=_params(2),
    )(a, x, mod, w.astype(BF16))


def _cmul(a, t):
    w = a.shape[1] // 2
    ar, ai, tr, ti = a[:, :w], a[:, w:], t[:, :w], t[:, w:]
    return jnp.concatenate([ar * tr - ai * ti, ar * ti + ai * tr], axis=1)


def _s5_kernel(tm, x_ref, mod_ref, g_ref, bm_ref, cm_ref, e_ref, f_ref, a_ref, tri_ref, dskip_ref,
               wglu_ref, o_ref, u_sc, y_sc, s_sc):
    t = pl.program_id(1)
    x = x_ref[0]
    shift, scale, gate = _mod_rows(mod_ref, 1)
    u_sc[...] = _adaln(x, g_ref[...], shift, scale)
    L = S5_CHUNK
    n_blocks = bm_ref.shape[0]

    @pl.when(t == 0)
    def _():
        s_sc[...] = jnp.zeros_like(s_sc)

    tri = tri_ref[...]
    chunks = [slice(c * L, (c + 1) * L) for c in range(tm // L)]
    for kb in range(n_blocks):
        cols = slice(kb * LANES, (kb + 1) * LANES)
        bu = _dot(u_sc[:, cols].astype(BF16), bm_ref[kb])
        sums = [_dot(tri, _cmul(bu[rows], e_ref[kb]).astype(BF16)) for rows in chunks]
        f_last = f_ref[kb, L - 1:L, :]
        carried = [s_sc[kb:kb + 1, :]]
        for p in sums:
            carried.append(_cmul(_cmul(p[L - 1:L, :] + carried[-1], f_last), a_ref[kb]))
        s_sc[kb:kb + 1, :] = carried[-1]
        xs = jnp.concatenate([_cmul(p + s, f_ref[kb]).astype(BF16) for p, s in zip(sums, carried)],
                             axis=0)
        y_sc[:, cols] = _dot(xs, cm_ref[kb])
    y = y_sc[...] + dskip_ref[...] * u_sc[...]
    g = jax.nn.gelu(y, approximate=True)
    out = g * jax.nn.sigmoid(_dot(g.astype(BF16), wglu_ref[...]))
    o_ref[0] = x + (1.0 + gate) * out


def _s5_tables(lam_re, lam_im, log_dt, b_re, b_im, c_re, c_im):
    g_, n_ = lam_re.shape
    i_ = b_re.shape[2]
    gl = LANES // i_
    nb = g_ // gl
    dt = jnp.exp(log_dt)[:, None]
    ar, ai = lam_re, lam_im
    mag = jnp.exp(ar * dt)
    lb_re, lb_im = mag * jnp.cos(ai * dt), mag * jnp.sin(ai * dt)
    den = ar * ar + ai * ai
    nr, ni = lb_re - 1.0, lb_im
    k_re = (nr * ar + ni * ai) / den
    k_im = (ni * ar - nr * ai) / den
    bb_re = k_re[..., None] * b_re - k_im[..., None] * b_im
    bb_im = k_re[..., None] * b_im + k_im[..., None] * b_re
    eye = jnp.eye(gl, dtype=F32)
    bb = jnp.stack([bb_re, bb_im]).reshape(2, nb, gl, n_, i_)
    bm = jnp.einsum('gh,pkgni->kgiphn', eye, bb).reshape(nb, gl * i_, 2 * gl * n_)
    cc = jnp.stack([c_re, -c_im]).reshape(2, nb, gl, i_, n_)
    cm = jnp.einsum('gh,pkgin->kpgnhi', eye, cc).reshape(nb, 2 * gl * n_, gl * i_)

    def lbar_pow(p):
        e = jnp.exp(p[:, None, None] * (ar * dt)[None])
        th = p[:, None, None] * (ai * dt)[None]
        t = jnp.stack([e * jnp.cos(th), e * jnp.sin(th)], axis=1)
        return t.reshape(-1, 2, nb, gl, n_).transpose(2, 0, 1, 3, 4).reshape(nb, -1, 2 * gl * n_)

    mid = S5_CHUNK // 2
    j = jnp.arange(S5_CHUNK, dtype=F32) - mid
    return (bm.astype(BF16), cm.astype(BF16), lbar_pow(-j), lbar_pow(j),
            lbar_pow(jnp.full((1,), mid + 1.0, F32)))


def _s5_layer(x, mod, gain, lam_re, lam_im, log_dt, b_re, b_im, c_re, c_im, d_skip, w_glu, tm=512):
    b, s, d = x.shape
    bm, cm, e_tab, f_tab, a_tab = _s5_tables(lam_re, lam_im, log_dt, b_re, b_im, c_re, c_im)
    nb, _, sw = bm.shape
    L = S5_CHUNK
    tri = (jnp.arange(L)[:, None] >= jnp.arange(L)[None, :]).astype(BF16)
    x_spec, mod_spec = _tile_specs(tm, d)
    return pl.pallas_call(
        functools.partial(_s5_kernel, tm),
        out_shape=jax.ShapeDtypeStruct(x.shape, F32),
        grid=(b, s // tm),
        in_specs=[x_spec, mod_spec, _const_spec((1, d)),
                  _const_spec((nb, LANES, sw)), _const_spec((nb, sw, LANES)),
                  _const_spec((nb, L, sw)), _const_spec((nb, L, sw)), _const_spec((nb, 1, sw)),
                  _const_spec((L, L)), _const_spec((1, d)), _const_spec((d, d))],
        out_specs=x_spec,
        scratch_shapes=[pltpu.VMEM((tm, d), F32), pltpu.VMEM((tm, d), F32), pltpu.VMEM((nb, sw), F32)],
        compiler_params=_params(2),
    )(x, mod, gain.reshape(1, d), bm, cm, e_tab, f_tab, a_tab, tri, d_skip.reshape(1, d),
      w_glu.astype(BF16))


def _conv_kernel(tm, x_ref, mod_ref, g_ref, win_ref, cw_ref, wout_ref, o_ref, cz_sc):
    t = pl.program_id(1)
    x = x_ref[0]
    d = x.shape[1]
    shift, scale, gate = _mod_rows(mod_ref, 1)
    h = _adaln(x, g_ref[...], shift, scale).astype(BF16)

    @pl.when(t == 0)
    def _():
        cz_sc[0:CONV_HALO, :] = jnp.zeros((CONV_HALO, d), F32)

    cz_sc[CONV_HALO:, :] = _dot(h, win_ref[:, d:2 * d]) * _dot(h, win_ref[:, 2 * d:3 * d])
    cw = cw_ref[...]
    conv = (cw[0:1] * cz_sc[CONV_HALO - 2:CONV_HALO - 2 + tm, :]
            + cw[1:2] * cz_sc[CONV_HALO - 1:CONV_HALO - 1 + tm, :]
            + cw[2:3] * cz_sc[CONV_HALO:, :])
    cz_sc[0:CONV_HALO, :] = cz_sc[tm:tm + CONV_HALO, :]
    gated = (_dot(h, win_ref[:, 0:d]) * conv).astype(BF16)
    o_ref[0] = x + (1.0 + gate) * _dot(gated, wout_ref[...])


def _conv_layer(x, mod, gain, w_in, conv_w, w_out, tm=512):
    b, s, d = x.shape
    kw = conv_w.shape[0]
    x_spec, mod_spec = _tile_specs(tm, d)
    return pl.pallas_call(
        functools.partial(_conv_kernel, tm),
        out_shape=jax.ShapeDtypeStruct(x.shape, F32),
        grid=(b, s // tm),
        in_specs=[x_spec, mod_spec, _const_spec((1, d)),
                  _const_spec((d, 3 * d)), _const_spec((kw, d)), _const_spec((d, d))],
        out_specs=x_spec,
        scratch_shapes=[pltpu.VMEM((tm + CONV_HALO, d), F32)],
        compiler_params=_params(2),
    )(x, mod, gain.reshape(1, d), w_in.astype(BF16), conv_w.reshape(kw, d), w_out.astype(BF16))


def kernel(x, c, ada_w, ada_b, norm_g, ffn_w_in, ffn_w_out, pool_w, pool_scale, fox_w_in, fox_b_f, fox_q_gain, fox_k_gain, fox_w_o, s5_lam_re, s5_lam_im, s5_log_dt, s5_b_re, s5_b_im, s5_c_re, s5_c_im, s5_d, s5_w_glu, conv_w_in, conv_w, conv_w_out):
    b, s, d = x.shape
    depth = ada_w.shape[0]
    mod_all = _modulation(c, ada_w, ada_b).reshape(depth, b, N_ADA, d)
    n_mixers = 4
    for i in range(depth):
        mod = mod_all[i]
        x = _ffn(x, mod, norm_g[i, 0], ffn_w_in[i, 0], ffn_w_out[i, 0], sub=0)
        m, r = i % n_mixers, i // n_mixers
        if m == 0:
            x = _pool_layer(x, mod, norm_g[i, 1], pool_w[r], pool_scale[r])
        elif m == 1:
            q, k, vt, qa, ka = _fox_project(x, mod, norm_g[i, 1], fox_w_in[r], fox_b_f[r],
                                            fox_q_gain[r], fox_k_gain[r])
            o = _fox_attend(q, qa, k, ka, vt)
            x = _proj_out(o, x, mod, fox_w_o[r])
        elif m == 2:
            x = _s5_layer(x, mod, norm_g[i, 1], s5_lam_re[r], s5_lam_im[r], s5_log_dt[r],
                          s5_b_re[r], s5_b_im[r], s5_c_re[r], s5_c_im[r], s5_d[r], s5_w_glu[r])
        else:
            x = _conv_layer(x, mod, norm_g[i, 1], conv_w_in[r], conv_w[r], conv_w_out[r])
        x = _ffn(x, mod, norm_g[i, 2], ffn_w_in[i, 1], ffn_w_out[i, 1], sub=2)
    return x
```

```python
import functools
import math

import jax
import jax.numpy as jnp
from jax import lax
from jax.experimental import pallas as pl
from jax.experimental.pallas import tpu as pltpu

F32 = jnp.float32
BF16 = jnp.bfloat16

NORM_EPS = 1e-6
N_SUBLAYERS = 3
N_ADA = 3 * N_SUBLAYERS
POOL_WINDOWS = (2, 4, 8, 16)
POOL_HALO = 16
FOX_HEADS = 16
FOX_HEAD_DIM = 64
S5_GROUP = 16
S5_STATE = 64
S5_CHUNK = 128
CONV_HALO = 8
LANES = 128
NEG_BIG = -1e30
LOG2E = math.log2(math.e)
AUG_LANES = 8
V_ROWS = FOX_HEAD_DIM + 16
FOX_TILE = 512

VMEM_LIMIT = 56 * 1024 * 1024


def _params(n_grid, vmem=VMEM_LIMIT):
    return pltpu.CompilerParams(dimension_semantics=("arbitrary",) * n_grid,
                                vmem_limit_bytes=vmem)


def _dot(a, b):
    return jnp.dot(a, b, preferred_element_type=F32)


def _adaln(x, gain, shift, scale):
    ms = jnp.mean(x * x, axis=-1, keepdims=True)
    y = x * lax.rsqrt(ms + NORM_EPS) * gain
    return y * (1.0 + scale) + shift


def _mod_rows(mod_ref, sub):
    m = mod_ref[0]
    return m[3 * sub:3 * sub + 1], m[3 * sub + 1:3 * sub + 2], m[3 * sub + 2:3 * sub + 3]


def _const_spec(shape):
    nd = len(shape)
    return pl.BlockSpec(shape, lambda *_: (0,) * nd, pipeline_mode=pl.Buffered(1))


def _tile_specs(tm, d):
    x_spec = pl.BlockSpec((1, tm, d), lambda b, t: (b, t, 0))
    mod_spec = pl.BlockSpec((1, N_ADA, d), lambda b, t: (b, 0, 0))
    return x_spec, mod_spec


def _mod_kernel(c_ref, w_ref, b_ref, o_ref):
    c = c_ref[...]
    cond = (c * jax.nn.sigmoid(c)).astype(BF16)
    o_ref[0] = _dot(cond, w_ref[0].astype(BF16)) + b_ref[0]


def _modulation(c, ada_w, ada_b):
    depth, d, n = ada_w.shape
    b = c.shape[0]
    tn = 1024
    return pl.pallas_call(
        _mod_kernel,
        out_shape=jax.ShapeDtypeStruct((depth, b, n), F32),
        grid=(depth, n // tn),
        in_specs=[pl.BlockSpec((b, d), lambda l, j: (0, 0)),
                  pl.BlockSpec((1, d, tn), lambda l, j: (l, 0, j)),
                  pl.BlockSpec((1, 1, tn), lambda l, j: (l, 0, j))],
        out_specs=pl.BlockSpec((1, b, tn), lambda l, j: (l, 0, j)),
        compiler_params=_params(2),
    )(c, ada_w, ada_b.reshape(depth, 1, n))


def _ffn_kernel(sub, tf, x_ref, mod_ref, g_ref, win_ref, wout_ref, o_ref, h_sc, a_sc):
    x = x_ref[0]
    shift, scale, gate = _mod_rows(mod_ref, sub)
    h_sc[...] = _adaln(x, g_ref[...], shift, scale).astype(BF16)
    dff = a_sc.shape[1]
    for j in range(dff // tf):
        h = h_sc[...]
        g = _dot(h, win_ref[:, j * tf:(j + 1) * tf])
        u = _dot(h, win_ref[:, dff + j * tf:dff + (j + 1) * tf])
        a_sc[:, j * tf:(j + 1) * tf] = (g * jax.nn.sigmoid(g) * u).astype(BF16)
    y = _dot(a_sc[...], wout_ref[...])
    o_ref[0] = x + (0.5 * (1.0 + gate)) * y


def _ffn(x, mod, gain, w_in_all, w_out_all, layer, half, sub, tm=512, tf=256):
    b, s, d = x.shape
    dff = w_out_all.shape[2]
    x_spec, mod_spec = _tile_specs(tm, d)

    def pick(rows, cols):
        return pl.BlockSpec((None, None, rows, cols), lambda b_, t: (layer, half, 0, 0),
                            pipeline_mode=pl.Buffered(1))

    return pl.pallas_call(
        functools.partial(_ffn_kernel, sub, tf),
        out_shape=jax.ShapeDtypeStruct(x.shape, F32),
        grid=(b, s // tm),
        in_specs=[x_spec, mod_spec, _const_spec((1, d)), pick(d, 2 * dff), pick(dff, d)],
        out_specs=x_spec,
        scratch_shapes=[pltpu.VMEM((tm, d), BF16), pltpu.VMEM((tm, dff), BF16)],
        compiler_params=_params(2),
    )(x, mod, gain.reshape(1, d), w_in_all, w_out_all)


def _pool_kernel(tm, x_ref, mod_ref, g_ref, w_ref, sc_ref, o_ref, h_sc):
    t = pl.program_id(1)
    x = x_ref[0]
    d = x.shape[1]
    cg = d // len(POOL_WINDOWS)
    shift, scale, gate = _mod_rows(mod_ref, 1)

    @pl.when(t == 0)
    def _():
        h_sc[0:POOL_HALO, :] = jnp.zeros((POOL_HALO, d), F32)

    h_sc[POOL_HALO:, :] = _adaln(x, g_ref[...], shift, scale)
    pos = t * tm + lax.broadcasted_iota(jnp.int32, (tm, 1), 0)
    ys = []
    for gi, w in enumerate(POOL_WINDOWS):
        sl = slice(gi * cg, (gi + 1) * cg)
        cur = h_sc[POOL_HALO:, sl]
        acc = cur
        for k in range(1, w):
            acc = acc + h_sc[POOL_HALO - k:POOL_HALO - k + tm, sl]
        cnt = jnp.minimum(pos + 1, w).astype(F32)
        pooled = acc / cnt - cur
        ys.append(_dot(pooled.astype(BF16), w_ref[gi]))
    y = jnp.concatenate(ys, axis=1) * sc_ref[...]
    h_sc[0:POOL_HALO, :] = h_sc[tm:tm + POOL_HALO, :]
    o_ref[0] = x + (1.0 + gate) * y


def _pool_layer(x, mod, gain, w_grp, scale, tm=512):
    b, s, d = x.shape
    ng, cg, _ = w_grp.shape
    x_spec, mod_spec = _tile_specs(tm, d)
    return pl.pallas_call(
        functools.partial(_pool_kernel, tm),
        out_shape=jax.ShapeDtypeStruct(x.shape, F32),
        grid=(b, s // tm),
        in_specs=[x_spec, mod_spec, _const_spec((1, d)),
                  _const_spec((ng, cg, cg)), _const_spec((1, d))],
        out_specs=x_spec,
        scratch_shapes=[pltpu.VMEM((tm + POOL_HALO, d), F32)],
        compiler_params=_params(2),
    )(x, mod, gain.reshape(1, d), w_grp.astype(BF16), scale.reshape(1, d))


def _split3(v):
    p1 = v.astype(BF16)
    r1 = v - p1.astype(F32)
    p2 = r1.astype(BF16)
    p3 = (r1 - p2.astype(F32)).astype(BF16)
    return p1, p2, p3


def _head_rms(q, seg_ref, segt_ref):
    ms = _dot((q * q).astype(BF16), seg_ref[...])
    r = lax.rsqrt(ms + NORM_EPS)
    r_hi = r.astype(BF16)
    r_lo = (r - r_hi.astype(F32)).astype(BF16)
    rb = _dot(r_hi, segt_ref[...]) + _dot(r_lo, segt_ref[...])
    return q * rb


def _foxproj_kernel(x_ref, mod_ref, g_ref, wq_ref, wk_ref, wv_ref, wf_ref, bf_ref, qg_ref, kg_ref,
                    seg_ref, segt_ref, tri_ref, place_ref, ones_ref,
                    q_ref, k_ref, v_ref, qa_ref, ka_ref, carry_sc):
    t = pl.program_id(1)
    x = x_ref[0]
    shift, scale, _ = _mod_rows(mod_ref, 1)
    h = _adaln(x, g_ref[...], shift, scale).astype(BF16)
    q = _head_rms(_dot(h, wq_ref[...]), seg_ref, segt_ref)
    q_ref[0] = (q * qg_ref[...]).astype(BF16)
    k = _head_rms(_dot(h, wk_ref[...]), seg_ref, segt_ref)
    k_ref[0] = (k * kg_ref[...]).astype(BF16)
    v_ref[0, 0] = lax.dot_general(wv_ref[...], h, (((1,), (1,)), ((), ())),
                                  preferred_element_type=F32).astype(BF16)
    fl = _dot(h, wf_ref[...]) + bf_ref[...]
    logf = jnp.minimum(fl, 0.0) - jnp.log(1.0 + jnp.exp(-jnp.abs(fl)))

    @pl.when(t == 0)
    def _():
        carry_sc[...] = jnp.zeros_like(carry_sc)

    tri = tri_ref[...]
    cf = sum(_dot(tri, p) for p in _split3(logf)) + carry_sc[0:1, :]
    tm = cf.shape[0]
    carry_sc[0:1, :] = cf[tm - 1:tm, :]
    pieces = jnp.concatenate(_split3(cf * LOG2E), axis=1)
    aug = _dot(pieces, place_ref[...]) + ones_ref[...]
    qa_ref[0] = aug[:, :LANES].astype(BF16)
    ka_ref[0] = aug[:, LANES:].astype(BF16)


def _fox_project(x, mod, gain, w_in, b_f, q_gain, k_gain, tm=FOX_TILE):
    b, s, d = x.shape
    x_spec, mod_spec = _tile_specs(tm, d)
    w = w_in.astype(BF16)
    wf = jnp.zeros((d, LANES), BF16).at[:, :FOX_HEADS].set(w[:, 3 * d:])
    bf = jnp.zeros((1, LANES), F32).at[0, :FOX_HEADS].set(b_f)
    head_of = jnp.arange(d) // FOX_HEAD_DIM
    seg = (head_of[:, None] == jnp.arange(LANES)[None, :]).astype(F32)
    qg = jnp.tile(q_gain, FOX_HEADS).reshape(1, d) * (FOX_HEAD_DIM ** -0.5 * LOG2E)
    kg = jnp.tile(k_gain, FOX_HEADS).reshape(1, d)
    tri = (jnp.arange(tm)[:, None] >= jnp.arange(tm)[None, :]).astype(BF16)
    hh, ii = jnp.meshgrid(jnp.arange(FOX_HEADS), jnp.arange(3), indexing="ij")
    place = jnp.zeros((3 * LANES, 2 * LANES), F32)
    place = place.at[ii * LANES + hh, AUG_LANES * hh + 3 + ii].set(1.0)
    place = place.at[ii * LANES + hh, LANES + AUG_LANES * hh + ii].set(-1.0)
    ones = jnp.zeros((1, 2 * LANES), F32)
    ones = ones.at[0, AUG_LANES * hh + ii].set(1.0).at[0, LANES + AUG_LANES * hh + 3 + ii].set(1.0)
    act = jax.ShapeDtypeStruct((b, s, d), BF16)
    aug = jax.ShapeDtypeStruct((b, s, LANES), BF16)
    act_t = jax.ShapeDtypeStruct((b, s // tm, d, tm), BF16)
    return pl.pallas_call(
        _foxproj_kernel,
        out_shape=(act, act, act_t, aug, aug),
        grid=(b, s // tm),
        in_specs=[x_spec, mod_spec, _const_spec((1, d)),
                  _const_spec((d, d)), _const_spec((d, d)), _const_spec((d, d)),
                  _const_spec((d, LANES)), _const_spec((1, LANES)),
                  _const_spec((1, d)), _const_spec((1, d)),
                  _const_spec((d, LANES)), _const_spec((LANES, d)), _const_spec((tm, tm)),
                  _const_spec((3 * LANES, 2 * LANES)), _const_spec((1, 2 * LANES))],
        out_specs=(pl.BlockSpec((1, tm, d), lambda b_, t: (b_, t, 0)),) * 2
        + (pl.BlockSpec((1, 1, d, tm), lambda b_, t: (b_, t, 0, 0)),)
        + (pl.BlockSpec((1, tm, LANES), lambda b_, t: (b_, t, 0)),) * 2,
        scratch_shapes=[pltpu.VMEM((8, LANES), F32)],
        compiler_params=_params(2),
    )(x, mod, gain.reshape(1, d), w[:, :d], w[:, d:2 * d], w[:, 2 * d:3 * d].T, wf, bf, qg, kg,
      (seg / FOX_HEAD_DIM).astype(BF16), seg.T.astype(BF16), tri, place.astype(BF16), ones)


def _foxattn_kernel(TK, TQ, q_ref, qa_ref, k_ref, ka_ref, vt_ref, o_ref,
                    sa_sc, sb_sc, qt_sc, m_sc, acc_sc, diff_sc):
    first_step = (pl.program_id(0) == 0) & (pl.program_id(1) == 0) & (pl.program_id(2) == 0)
    pair = pl.program_id(1)
    qi = pl.program_id(2)
    lane = lax.broadcasted_iota(jnp.int32, (1, LANES), 1)
    q, qa = q_ref[0], qa_ref[0]
    zero = jnp.zeros_like(q)
    for hd in range(2):
        own = (lane < FOX_HEAD_DIM) == (hd == 0)
        bias_lo = AUG_LANES * (2 * pair + hd)
        own_bias = (lane >= bias_lo) & (lane < bias_lo + AUG_LANES)
        qh = jnp.concatenate([jnp.where(own, q, zero), jnp.where(own_bias, qa, zero)], axis=1)
        qt_sc[hd] = qh.astype(F32).T.astype(BF16)
        m_sc[hd] = jnp.full((8, TQ), NEG_BIG, F32)
        acc_sc[hd] = jnp.zeros((V_ROWS, TQ), F32)

    @pl.when(first_step)
    def _():
        diff_sc[...] = (lax.broadcasted_iota(jnp.int32, (TK, TQ), 0)
                        - lax.broadcasted_iota(jnp.int32, (TK, TQ), 1))

    every = slice(0, TQ)
    early, late = slice(0, TK), slice(TK, TQ)

    def scores(j, s_sc, qs=every):
        rows = pl.ds(pl.multiple_of(j * TK, TK), TK)
        kj = jnp.concatenate([k_ref[0, rows, :], ka_ref[0, rows, :]], axis=1)
        for hd in range(2):
            s_sc[hd, :, qs] = _dot(kj, qt_sc[hd, :, qs])

    def absorb(j, s_sc, masked, qs=every):
        vt = vt_ref[0, j]
        ones = jnp.ones((V_ROWS - FOX_HEAD_DIM, TK), BF16)
        if masked:
            visible = diff_sc[:, qs] <= qi * TQ - j * TK
        for hd in range(2):
            s = s_sc[hd, :, qs]
            if masked:
                s = jnp.where(visible, s, NEG_BIG)
            m_prev = m_sc[hd, :, qs]
            m_new = jnp.maximum(m_prev, jnp.max(s, axis=0, keepdims=True))
            p = jnp.exp2(s - m_new[0:1, :]).astype(BF16)
            v_aug = jnp.concatenate([vt[hd * FOX_HEAD_DIM:(hd + 1) * FOX_HEAD_DIM, :], ones], axis=0)
            alpha = jnp.exp2(m_prev - m_new)[0:1, :]
            acc_sc[hd, :, qs] = alpha * acc_sc[hd, :, qs] + _dot(v_aug, p)
            m_sc[hd, :, qs] = m_new

    scores(0, sa_sc)

    def two_tiles(jj, carry):
        j = 2 * jj
        scores(j + 1, sb_sc)
        absorb(j, sa_sc, False)
        scores(j + 2, sa_sc)
        absorb(j + 1, sb_sc, False)
        return carry

    lax.fori_loop(0, qi, two_tiles, 0)
    scores(2 * qi + 1, sb_sc, late)
    absorb(2 * qi, sa_sc, True, early)
    absorb(2 * qi, sa_sc, False, late)
    absorb(2 * qi + 1, sb_sc, True, late)

    out_t = jnp.concatenate(
        [acc_sc[hd, 0:FOX_HEAD_DIM, :] / acc_sc[hd, FOX_HEAD_DIM:FOX_HEAD_DIM + 1, :] for hd in range(2)],
        axis=0)
    o_ref[0] = out_t.T.astype(o_ref.dtype)


def _fox_attend(q, qa, k, ka, vt, tk=FOX_TILE):
    b, s, d = q.shape
    pairs = d // LANES
    tq = 2 * tk
    assert vt.shape == (b, s // tk, d, tk)
    return pl.pallas_call(
        functools.partial(_foxattn_kernel, tk, tq),
        out_shape=jax.ShapeDtypeStruct((b, s, d), BF16),
        grid=(b, pairs, s // tq),
        in_specs=[pl.BlockSpec((1, tq, LANES), lambda b_, p, i: (b_, i, p)),
                  pl.BlockSpec((1, tq, LANES), lambda b_, p, i: (b_, i, 0)),
                  pl.BlockSpec((1, s, LANES), lambda b_, p, i: (b_, 0, p)),
                  pl.BlockSpec((1, s, LANES), lambda b_, p, i: (b_, 0, 0)),
                  pl.BlockSpec((1, s // tk, LANES, tk), lambda b_, p, i: (b_, 0, p, 0))],
        out_specs=pl.BlockSpec((1, tq, LANES), lambda b_, p, i: (b_, i, p)),
        scratch_shapes=[pltpu.VMEM((2, tk, tq), F32), pltpu.VMEM((2, tk, tq), F32),
                        pltpu.VMEM((2, 2 * LANES, tq), BF16),
                        pltpu.VMEM((2, 8, tq), F32), pltpu.VMEM((2, V_ROWS, tq), F32),
                        pltpu.VMEM((tk, tq), jnp.int32)],
        compiler_params=_params(3),
    )(q, qa, k, ka, vt)


def _proj_out_kernel(a_ref, x_ref, mod_ref, w_ref, o_ref):
    _, _, gate = _mod_rows(mod_ref, 1)
    o_ref[0] = x_ref[0] + (1.0 + gate) * _dot(a_ref[0], w_ref[...])


def _proj_out(a, x, mod, w, tm=512):
    b, s, d = x.shape
    x_spec, mod_spec = _tile_specs(tm, d)
    return pl.pallas_call(
        _proj_out_kernel,
        out_shape=jax.ShapeDtypeStruct(x.shape, F32),
        grid=(b, s // tm),
        in_specs=[x_spec, x_spec, mod_spec, _const_spec((d, d))],
        out_specs=x_spec,
        compiler_params=_params(2),
    )(a, x, mod, w.astype(BF16))


def _cmul(a, t):
    w = a.shape[1] // 2
    ar, ai, tr, ti = a[:, :w], a[:, w:], t[:, :w], t[:, w:]
    return jnp.concatenate([ar * tr - ai * ti, ar * ti + ai * tr], axis=1)


def _s5_kernel(tm, x_ref, mod_ref, g_ref, bm_ref, cm_ref, e_ref, f_ref, a_ref, tri_ref, dskip_ref,
               wglu_ref, o_ref, u_sc, y_sc, s_sc):
    t = pl.program_id(1)
    x = x_ref[0]
    shift, scale, gate = _mod_rows(mod_ref, 1)
    u_sc[...] = _adaln(x, g_ref[...], shift, scale)
    L = S5_CHUNK
    n_blocks = bm_ref.shape[0]

    @pl.when(t == 0)
    def _():
        s_sc[...] = jnp.zeros_like(s_sc)

    tri = tri_ref[...]
    chunks = [slice(c * L, (c + 1) * L) for c in range(tm // L)]
    for kb in range(n_blocks):
        cols = slice(kb * LANES, (kb + 1) * LANES)
        bu = _dot(u_sc[:, cols].astype(BF16), bm_ref[kb]).astype(BF16)
        sums = [_dot(tri, _cmul(bu[rows], e_ref[kb])) for rows in chunks]
        carried = [s_sc[kb:kb + 1, :]]
        for p in sums:
            carried.append(_cmul(p[L - 1:L, :] + carried[-1], a_ref[kb]))
        s_sc[kb:kb + 1, :] = carried[-1]
        xs = jnp.concatenate([_cmul((p + s).astype(BF16), f_ref[kb]) for p, s in zip(sums, carried)],
                             axis=0)
        y_sc[:, cols] = _dot(xs, cm_ref[kb])
    y = y_sc[...] + dskip_ref[...] * u_sc[...]
    g = jax.nn.gelu(y, approximate=True)
    out = g * jax.nn.sigmoid(_dot(g.astype(BF16), wglu_ref[...]))
    o_ref[0] = x + (1.0 + gate) * out


def _s5_tables(lam_re, lam_im, log_dt, b_re, b_im, c_re, c_im):
    g_, n_ = lam_re.shape
    i_ = b_re.shape[2]
    gl = LANES // i_
    nb = g_ // gl
    dt = jnp.exp(log_dt)[:, None]
    ar, ai = lam_re, lam_im
    mag = jnp.exp(ar * dt)
    lb_re, lb_im = mag * jnp.cos(ai * dt), mag * jnp.sin(ai * dt)
    den = ar * ar + ai * ai
    nr, ni = lb_re - 1.0, lb_im
    k_re = (nr * ar + ni * ai) / den
    k_im = (ni * ar - nr * ai) / den
    bb_re = k_re[..., None] * b_re - k_im[..., None] * b_im
    bb_im = k_re[..., None] * b_im + k_im[..., None] * b_re
    eye = jnp.eye(gl, dtype=F32)
    bb = jnp.stack([bb_re, bb_im]).reshape(2, nb, gl, n_, i_)
    bm = jnp.einsum('gh,pkgni->kgiphn', eye, bb).reshape(nb, gl * i_, 2 * gl * n_)
    cc = jnp.stack([c_re, -c_im]).reshape(2, nb, gl, i_, n_)
    cm = jnp.einsum('gh,pkgin->kpgnhi', eye, cc).reshape(nb, 2 * gl * n_, gl * i_)

    def lbar_pow(p):
        e = jnp.exp(p[:, None, None] * (ar * dt)[None])
        th = p[:, None, None] * (ai * dt)[None]
        t = jnp.stack([e * jnp.cos(th), e * jnp.sin(th)], axis=1)
        return t.reshape(-1, 2, nb, gl, n_).transpose(2, 0, 1, 3, 4).reshape(nb, -1, 2 * gl * n_)

    mid = S5_CHUNK // 2
    j = jnp.arange(S5_CHUNK, dtype=F32) - mid
    return (bm.astype(BF16), cm.astype(BF16), lbar_pow(-j).astype(BF16), lbar_pow(j).astype(BF16),
            lbar_pow(jnp.full((1,), float(S5_CHUNK), F32)))


def _s5_layer(x, mod, gain, lam_re, lam_im, log_dt, b_re, b_im, c_re, c_im, d_skip, w_glu, tm=512):
    b, s, d = x.shape
    bm, cm, e_tab, f_tab, a_tab = _s5_tables(lam_re, lam_im, log_dt, b_re, b_im, c_re, c_im)
    nb, _, sw = bm.shape
    L = S5_CHUNK
    tri = (jnp.arange(L)[:, None] >= jnp.arange(L)[None, :]).astype(BF16)
    x_spec, mod_spec = _tile_specs(tm, d)
    return pl.pallas_call(
        functools.partial(_s5_kernel, tm),
        out_shape=jax.ShapeDtypeStruct(x.shape, F32),
        grid=(b, s // tm),
        in_specs=[x_spec, mod_spec, _const_spec((1, d)),
                  _const_spec((nb, LANES, sw)), _const_spec((nb, sw, LANES)),
                  _const_spec((nb, L, sw)), _const_spec((nb, L, sw)), _const_spec((nb, 1, sw)),
                  _const_spec((L, L)), _const_spec((1, d)), _const_spec((d, d))],
        out_specs=x_spec,
        scratch_shapes=[pltpu.VMEM((tm, d), F32), pltpu.VMEM((tm, d), F32), pltpu.VMEM((nb, sw), F32)],
        compiler_params=_params(2),
    )(x, mod, gain.reshape(1, d), bm, cm, e_tab, f_tab, a_tab, tri, d_skip.reshape(1, d),
      w_glu.astype(BF16))


def _conv_kernel(tm, x_ref, mod_ref, g_ref, win_ref, cw_ref, wout_ref, o_ref, cz_sc):
    t = pl.program_id(1)
    x = x_ref[0]
    d = x.shape[1]
    shift, scale, gate = _mod_rows(mod_ref, 1)
    h = _adaln(x, g_ref[...], shift, scale).astype(BF16)

    @pl.when(t == 0)
    def _():
        cz_sc[0:CONV_HALO, :] = jnp.zeros((CONV_HALO, d), F32)

    cz_sc[CONV_HALO:, :] = _dot(h, win_ref[:, d:2 * d]) * _dot(h, win_ref[:, 2 * d:3 * d])
    cw = cw_ref[...]
    conv = (cw[0:1] * cz_sc[CONV_HALO - 2:CONV_HALO - 2 + tm, :]
            + cw[1:2] * cz_sc[CONV_HALO - 1:CONV_HALO - 1 + tm, :]
            + cw[2:3] * cz_sc[CONV_HALO:, :])
    cz_sc[0:CONV_HALO, :] = cz_sc[tm:tm + CONV_HALO, :]
    gated = (_dot(h, win_ref[:, 0:d]) * conv).astype(BF16)
    o_ref[0] = x + (1.0 + gate) * _dot(gated, wout_ref[...])


def _conv_layer(x, mod, gain, w_in, conv_w, w_out, tm=512):
    b, s, d = x.shape
    kw = conv_w.shape[0]
    x_spec, mod_spec = _tile_specs(tm, d)
    return pl.pallas_call(
        functools.partial(_conv_kernel, tm),
        out_shape=jax.ShapeDtypeStruct(x.shape, F32),
        grid=(b, s // tm),
        in_specs=[x_spec, mod_spec, _const_spec((1, d)),
                  _const_spec((d, 3 * d)), _const_spec((kw, d)), _const_spec((d, d))],
        out_specs=x_spec,
        scratch_shapes=[pltpu.VMEM((tm + CONV_HALO, d), F32)],
        compiler_params=_params(2),
    )(x, mod, gain.reshape(1, d), w_in.astype(BF16), conv_w.reshape(kw, d), w_out.astype(BF16))


def kernel(x, c, ada_w, ada_b, norm_g, ffn_w_in, ffn_w_out, pool_w, pool_scale, fox_w_in, fox_b_f, fox_q_gain, fox_k_gain, fox_w_o, s5_lam_re, s5_lam_im, s5_log_dt, s5_b_re, s5_b_im, s5_c_re, s5_c_im, s5_d, s5_w_glu, conv_w_in, conv_w, conv_w_out):
    b, s, d = x.shape
    depth = ada_w.shape[0]
    mod_all = _modulation(c, ada_w, ada_b).reshape(depth, b, N_ADA, d)
    n_mixers = 4
    w_in_all, w_out_all = ffn_w_in.astype(BF16), ffn_w_out.astype(BF16)
    for i in range(depth):
        mod = mod_all[i]
        x = _ffn(x, mod, norm_g[i, 0], w_in_all, w_out_all, i, 0, sub=0)
        m, r = i % n_mixers, i // n_mixers
        if m == 0:
            x = _pool_layer(x, mod, norm_g[i, 1], pool_w[r], pool_scale[r])
        elif m == 1:
            q, k, vt, qa, ka = _fox_project(x, mod, norm_g[i, 1], fox_w_in[r], fox_b_f[r],
                                            fox_q_gain[r], fox_k_gain[r])
            o = _fox_attend(q, qa, k, ka, vt)
            x = _proj_out(o, x, mod, fox_w_o[r])
        elif m == 2:
            x = _s5_layer(x, mod, norm_g[i, 1], s5_lam_re[r], s5_lam_im[r], s5_log_dt[r],
                          s5_b_re[r], s5_b_im[r], s5_c_re[r], s5_c_im[r], s5_d[r], s5_w_glu[r])
        else:
            x = _conv_layer(x, mod, norm_g[i, 1], conv_w_in[r], conv_w[r], conv_w_out[r])
        x = _ffn(x, mod, norm_g[i, 2], w_in_all, w_out_all, i, 1, sub=2)
    return x
```

```python
import functools
import math

import jax
import jax.numpy as jnp
from jax import lax
from jax.experimental import pallas as pl
from jax.experimental.pallas import tpu as pltpu

F32 = jnp.float32
BF16 = jnp.bfloat16

NORM_EPS = 1e-6
N_SUBLAYERS = 3
N_ADA = 3 * N_SUBLAYERS
POOL_WINDOWS = (2, 4, 8, 16)
POOL_HALO = 16
FOX_HEADS = 16
FOX_HEAD_DIM = 64
S5_GROUP = 16
S5_STATE = 64
S5_CHUNK = 32
CONV_HALO = 8
LANES = 128
NEG_BIG = -1e30
LOG2E = math.log2(math.e)
AUG_LANES = 8
V_ROWS = FOX_HEAD_DIM + 16
FOX_TILE = 512

VMEM_LIMIT = 56 * 1024 * 1024


def _params(n_grid, vmem=VMEM_LIMIT):
    return pltpu.CompilerParams(dimension_semantics=("arbitrary",) * n_grid,
                                vmem_limit_bytes=vmem)


def _dot(a, b):
    return jnp.dot(a, b, preferred_element_type=F32)


def _adaln(x, gain, shift, scale):
    ms = jnp.mean(x * x, axis=-1, keepdims=True)
    y = x * lax.rsqrt(ms + NORM_EPS) * gain
    return y * (1.0 + scale) + shift


def _mod_rows(mod_ref, sub):
    m = mod_ref[0]
    return m[3 * sub:3 * sub + 1], m[3 * sub + 1:3 * sub + 2], m[3 * sub + 2:3 * sub + 3]


def _const_spec(shape):
    nd = len(shape)
    return pl.BlockSpec(shape, lambda *_: (0,) * nd, pipeline_mode=pl.Buffered(1))


def _tile_specs(tm, d):
    x_spec = pl.BlockSpec((1, tm, d), lambda b, t: (b, t, 0))
    mod_spec = pl.BlockSpec((1, N_ADA, d), lambda b, t: (b, 0, 0))
    return x_spec, mod_spec


def _mod_kernel(c_ref, w_ref, b_ref, o_ref):
    c = c_ref[...]
    cond = (c * jax.nn.sigmoid(c)).astype(BF16)
    o_ref[0] = _dot(cond, w_ref[0].astype(BF16)) + b_ref[0]


def _modulation(c, ada_w, ada_b):
    depth, d, n = ada_w.shape
    b = c.shape[0]
    tn = 1024
    return pl.pallas_call(
        _mod_kernel,
        out_shape=jax.ShapeDtypeStruct((depth, b, n), F32),
        grid=(depth, n // tn),
        in_specs=[pl.BlockSpec((b, d), lambda l, j: (0, 0)),
                  pl.BlockSpec((1, d, tn), lambda l, j: (l, 0, j)),
                  pl.BlockSpec((1, 1, tn), lambda l, j: (l, 0, j))],
        out_specs=pl.BlockSpec((1, b, tn), lambda l, j: (l, 0, j)),
        compiler_params=_params(2),
    )(c, ada_w, ada_b.reshape(depth, 1, n))


def _ffn_kernel(sub, tf, x_ref, mod_ref, g_ref, win_ref, wout_ref, o_ref, h_sc, a_sc):
    x = x_ref[0]
    shift, scale, gate = _mod_rows(mod_ref, sub)
    h_sc[...] = _adaln(x, g_ref[...], shift, scale).astype(BF16)
    dff = a_sc.shape[1]
    for j in range(dff // tf):
        h = h_sc[...]
        g = _dot(h, win_ref[:, j * tf:(j + 1) * tf])
        u = _dot(h, win_ref[:, dff + j * tf:dff + (j + 1) * tf])
        a_sc[:, j * tf:(j + 1) * tf] = (g * jax.nn.sigmoid(g) * u).astype(BF16)
    y = _dot(a_sc[...], wout_ref[...])
    o_ref[0] = x + (0.5 * (1.0 + gate)) * y


def _ffn(x, mod, gain, w_in_all, w_out_all, layer, half, sub, tm=512, tf=256):
    b, s, d = x.shape
    dff = w_out_all.shape[2]
    x_spec, mod_spec = _tile_specs(tm, d)

    def pick(rows, cols):
        return pl.BlockSpec((None, None, rows, cols), lambda b_, t: (layer, half, 0, 0),
                            pipeline_mode=pl.Buffered(1))

    return pl.pallas_call(
        functools.partial(_ffn_kernel, sub, tf),
        out_shape=jax.ShapeDtypeStruct(x.shape, F32),
        grid=(b, s // tm),
        in_specs=[x_spec, mod_spec, _const_spec((1, d)), pick(d, 2 * dff), pick(dff, d)],
        out_specs=x_spec,
        scratch_shapes=[pltpu.VMEM((tm, d), BF16), pltpu.VMEM((tm, dff), BF16)],
        compiler_params=_params(2),
    )(x, mod, gain.reshape(1, d), w_in_all, w_out_all)


def _pool_kernel(tm, x_ref, mod_ref, g_ref, w_ref, sc_ref, o_ref, h_sc):
    t = pl.program_id(1)
    x = x_ref[0]
    d = x.shape[1]
    cg = d // len(POOL_WINDOWS)
    shift, scale, gate = _mod_rows(mod_ref, 1)

    @pl.when(t == 0)
    def _():
        h_sc[0:POOL_HALO, :] = jnp.zeros((POOL_HALO, d), F32)

    h_sc[POOL_HALO:, :] = _adaln(x, g_ref[...], shift, scale)
    pos = t * tm + lax.broadcasted_iota(jnp.int32, (tm, 1), 0)
    ys = []
    for gi, w in enumerate(POOL_WINDOWS):
        sl = slice(gi * cg, (gi + 1) * cg)
        cur = h_sc[POOL_HALO:, sl]
        acc = cur
        for k in range(1, w):
            acc = acc + h_sc[POOL_HALO - k:POOL_HALO - k + tm, sl]
        cnt = jnp.minimum(pos + 1, w).astype(F32)
        pooled = acc / cnt - cur
        ys.append(_dot(pooled.astype(BF16), w_ref[gi]))
    y = jnp.concatenate(ys, axis=1) * sc_ref[...]
    h_sc[0:POOL_HALO, :] = h_sc[tm:tm + POOL_HALO, :]
    o_ref[0] = x + (1.0 + gate) * y


def _pool_layer(x, mod, gain, w_grp, scale, tm=512):
    b, s, d = x.shape
    ng, cg, _ = w_grp.shape
    x_spec, mod_spec = _tile_specs(tm, d)
    return pl.pallas_call(
        functools.partial(_pool_kernel, tm),
        out_shape=jax.ShapeDtypeStruct(x.shape, F32),
        grid=(b, s // tm),
        in_specs=[x_spec, mod_spec, _const_spec((1, d)),
                  _const_spec((ng, cg, cg)), _const_spec((1, d))],
        out_specs=x_spec,
        scratch_shapes=[pltpu.VMEM((tm + POOL_HALO, d), F32)],
        compiler_params=_params(2),
    )(x, mod, gain.reshape(1, d), w_grp.astype(BF16), scale.reshape(1, d))


def _split3(v):
    p1 = v.astype(BF16)
    r1 = v - p1.astype(F32)
    p2 = r1.astype(BF16)
    p3 = (r1 - p2.astype(F32)).astype(BF16)
    return p1, p2, p3


def _head_rms(q, seg_ref, segt_ref):
    ms = _dot((q * q).astype(BF16), seg_ref[...])
    r = lax.rsqrt(ms + NORM_EPS)
    r_hi = r.astype(BF16)
    r_lo = (r - r_hi.astype(F32)).astype(BF16)
    rb = _dot(r_hi, segt_ref[...]) + _dot(r_lo, segt_ref[...])
    return q * rb


def _foxproj_kernel(x_ref, mod_ref, g_ref, wq_ref, wk_ref, wv_ref, wf_ref, bf_ref, qg_ref, kg_ref,
                    seg_ref, segt_ref, tri_ref, place_ref, ones_ref,
                    q_ref, k_ref, v_ref, qa_ref, ka_ref, carry_sc):
    t = pl.program_id(1)
    x = x_ref[0]
    shift, scale, _ = _mod_rows(mod_ref, 1)
    h = _adaln(x, g_ref[...], shift, scale).astype(BF16)
    q = _head_rms(_dot(h, wq_ref[...]), seg_ref, segt_ref)
    q_ref[0] = (q * qg_ref[...]).astype(BF16)
    k = _head_rms(_dot(h, wk_ref[...]), seg_ref, segt_ref)
    k_ref[0] = (k * kg_ref[...]).astype(BF16)
    v_ref[0, 0] = lax.dot_general(wv_ref[...], h, (((1,), (1,)), ((), ())),
                                  preferred_element_type=F32).astype(BF16)
    fl = _dot(h, wf_ref[...]) + bf_ref[...]
    logf = jnp.minimum(fl, 0.0) - jnp.log(1.0 + jnp.exp(-jnp.abs(fl)))

    @pl.when(t == 0)
    def _():
        carry_sc[...] = jnp.zeros_like(carry_sc)

    tri = tri_ref[...]
    cf = sum(_dot(tri, p) for p in _split3(logf)) + carry_sc[0:1, :]
    tm = cf.shape[0]
    carry_sc[0:1, :] = cf[tm - 1:tm, :]
    pieces = jnp.concatenate(_split3(cf * LOG2E), axis=1)
    aug = _dot(pieces, place_ref[...]) + ones_ref[...]
    qa_ref[0] = aug[:, :LANES].astype(BF16)
    ka_ref[0] = aug[:, LANES:].astype(BF16)


def _fox_project(x, mod, gain, w_in, b_f, q_gain, k_gain, tm=FOX_TILE):
    b, s, d = x.shape
    x_spec, mod_spec = _tile_specs(tm, d)
    w = w_in.astype(BF16)
    wf = jnp.zeros((d, LANES), BF16).at[:, :FOX_HEADS].set(w[:, 3 * d:])
    bf = jnp.zeros((1, LANES), F32).at[0, :FOX_HEADS].set(b_f)
    head_of = jnp.arange(d) // FOX_HEAD_DIM
    seg = (head_of[:, None] == jnp.arange(LANES)[None, :]).astype(F32)
    qg = jnp.tile(q_gain, FOX_HEADS).reshape(1, d) * (FOX_HEAD_DIM ** -0.5 * LOG2E)
    kg = jnp.tile(k_gain, FOX_HEADS).reshape(1, d)
    tri = (jnp.arange(tm)[:, None] >= jnp.arange(tm)[None, :]).astype(BF16)
    hh, ii = jnp.meshgrid(jnp.arange(FOX_HEADS), jnp.arange(3), indexing="ij")
    place = jnp.zeros((3 * LANES, 2 * LANES), F32)
    place = place.at[ii * LANES + hh, AUG_LANES * hh + 3 + ii].set(1.0)
    place = place.at[ii * LANES + hh, LANES + AUG_LANES * hh + ii].set(-1.0)
    ones = jnp.zeros((1, 2 * LANES), F32)
    ones = ones.at[0, AUG_LANES * hh + ii].set(1.0).at[0, LANES + AUG_LANES * hh + 3 + ii].set(1.0)
    act = jax.ShapeDtypeStruct((b, s, d), BF16)
    aug = jax.ShapeDtypeStruct((b, s, LANES), BF16)
    act_t = jax.ShapeDtypeStruct((b, s // tm, d, tm), BF16)
    return pl.pallas_call(
        _foxproj_kernel,
        out_shape=(act, act, act_t, aug, aug),
        grid=(b, s // tm),
        in_specs=[x_spec, mod_spec, _const_spec((1, d)),
                  _const_spec((d, d)), _const_spec((d, d)), _const_spec((d, d)),
                  _const_spec((d, LANES)), _const_spec((1, LANES)),
                  _const_spec((1, d)), _const_spec((1, d)),
                  _const_spec((d, LANES)), _const_spec((LANES, d)), _const_spec((tm, tm)),
                  _const_spec((3 * LANES, 2 * LANES)), _const_spec((1, 2 * LANES))],
        out_specs=(pl.BlockSpec((1, tm, d), lambda b_, t: (b_, t, 0)),) * 2
        + (pl.BlockSpec((1, 1, d, tm), lambda b_, t: (b_, t, 0, 0)),)
        + (pl.BlockSpec((1, tm, LANES), lambda b_, t: (b_, t, 0)),) * 2,
        scratch_shapes=[pltpu.VMEM((8, LANES), F32)],
        compiler_params=_params(2),
    )(x, mod, gain.reshape(1, d), w[:, :d], w[:, d:2 * d], w[:, 2 * d:3 * d].T, wf, bf, qg, kg,
      (seg / FOX_HEAD_DIM).astype(BF16), seg.T.astype(BF16), tri, place.astype(BF16), ones)


def _foxattn_kernel(TK, TQ, q_ref, qa_ref, k_ref, ka_ref, vt_ref, o_ref,
                    sa_sc, sb_sc, qt_sc, m_sc, acc_sc, diff_sc):
    first_step = (pl.program_id(0) == 0) & (pl.program_id(1) == 0) & (pl.program_id(2) == 0)
    pair = pl.program_id(1)
    qi = pl.program_id(2)
    lane = lax.broadcasted_iota(jnp.int32, (1, LANES), 1)
    q, qa = q_ref[0], qa_ref[0]
    zero = jnp.zeros_like(q)
    for hd in range(2):
        own = (lane < FOX_HEAD_DIM) == (hd == 0)
        bias_lo = AUG_LANES * (2 * pair + hd)
        own_bias = (lane >= bias_lo) & (lane < bias_lo + AUG_LANES)
        qh = jnp.concatenate([jnp.where(own, q, zero), jnp.where(own_bias, qa, zero)], axis=1)
        qt_sc[hd] = qh.T
        m_sc[hd] = jnp.full((8, TQ), NEG_BIG, F32)
        acc_sc[hd] = jnp.zeros((V_ROWS, TQ), F32)

    @pl.when(first_step)
    def _():
        diff_sc[...] = (lax.broadcasted_iota(jnp.int32, (TK, TQ), 0)
                        - lax.broadcasted_iota(jnp.int32, (TK, TQ), 1))

    every = slice(0, TQ)
    early, late = slice(0, TK), slice(TK, TQ)

    def scores(j, s_sc, qs=every):
        rows = pl.ds(pl.multiple_of(j * TK, TK), TK)
        kj = jnp.concatenate([k_ref[0, rows, :], ka_ref[0, rows, :]], axis=1)
        for hd in range(2):
            s_sc[hd, :, qs] = _dot(kj, qt_sc[hd, :, qs])

    def absorb(j, s_sc, masked, qs=every):
        vt = vt_ref[0, j]
        ones = jnp.ones((V_ROWS - FOX_HEAD_DIM, TK), BF16)
        if masked:
            visible = diff_sc[:, qs] <= qi * TQ - j * TK
        for hd in range(2):
            s = s_sc[hd, :, qs]
            if masked:
                s = jnp.where(visible, s, NEG_BIG)
            m_prev = m_sc[hd, :, qs]
            m_new = jnp.maximum(m_prev, jnp.max(s, axis=0, keepdims=True))
            p = jnp.exp2(s - m_new[0:1, :]).astype(BF16)
            v_aug = jnp.concatenate([vt[hd * FOX_HEAD_DIM:(hd + 1) * FOX_HEAD_DIM, :], ones], axis=0)
            alpha = jnp.exp2(m_prev - m_new)[0:1, :]
            acc_sc[hd, :, qs] = alpha * acc_sc[hd, :, qs] + _dot(v_aug, p)
            m_sc[hd, :, qs] = m_new

    scores(0, sa_sc)

    def two_tiles(jj, carry):
        j = 2 * jj
        scores(j + 1, sb_sc)
        absorb(j, sa_sc, False)
        scores(j + 2, sa_sc)
        absorb(j + 1, sb_sc, False)
        return carry

    lax.fori_loop(0, qi, two_tiles, 0)
    scores(2 * qi + 1, sb_sc, late)
    absorb(2 * qi, sa_sc, True, early)
    absorb(2 * qi, sa_sc, False, late)
    absorb(2 * qi + 1, sb_sc, True, late)

    out_t = jnp.concatenate(
        [acc_sc[hd, 0:FOX_HEAD_DIM, :] / acc_sc[hd, FOX_HEAD_DIM:FOX_HEAD_DIM + 1, :] for hd in range(2)],
        axis=0)
    o_ref[0] = out_t.T.astype(o_ref.dtype)


def _fox_attend(q, qa, k, ka, vt, tk=FOX_TILE):
    b, s, d = q.shape
    pairs = d // LANES
    tq = 2 * tk
    assert vt.shape == (b, s // tk, d, tk)
    return pl.pallas_call(
        functools.partial(_foxattn_kernel, tk, tq),
        out_shape=jax.ShapeDtypeStruct((b, s, d), BF16),
        grid=(b, pairs, s // tq),
        in_specs=[pl.BlockSpec((1, tq, LANES), lambda b_, p, i: (b_, i, p)),
                  pl.BlockSpec((1, tq, LANES), lambda b_, p, i: (b_, i, 0)),
                  pl.BlockSpec((1, s, LANES), lambda b_, p, i: (b_, 0, p)),
                  pl.BlockSpec((1, s, LANES), lambda b_, p, i: (b_, 0, 0)),
                  pl.BlockSpec((1, s // tk, LANES, tk), lambda b_, p, i: (b_, 0, p, 0))],
        out_specs=pl.BlockSpec((1, tq, LANES), lambda b_, p, i: (b_, i, p)),
        scratch_shapes=[pltpu.VMEM((2, tk, tq), F32), pltpu.VMEM((2, tk, tq), F32),
                        pltpu.VMEM((2, 2 * LANES, tq), BF16),
                        pltpu.VMEM((2, 8, tq), F32), pltpu.VMEM((2, V_ROWS, tq), F32),
                        pltpu.VMEM((tk, tq), jnp.int32)],
        compiler_params=_params(3),
    )(q, qa, k, ka, vt)


def _proj_out_kernel(a_ref, x_ref, mod_ref, w_ref, o_ref):
    _, _, gate = _mod_rows(mod_ref, 1)
    o_ref[0] = x_ref[0] + (1.0 + gate) * _dot(a_ref[0], w_ref[...])


def _proj_out(a, x, mod, w, tm=512):
    b, s, d = x.shape
    x_spec, mod_spec = _tile_specs(tm, d)
    return pl.pallas_call(
        _proj_out_kernel,
        out_shape=jax.ShapeDtypeStruct(x.shape, F32),
        grid=(b, s // tm),
        in_specs=[x_spec, x_spec, mod_spec, _const_spec((d, d))],
        out_specs=x_spec,
        compiler_params=_params(2),
    )(a, x, mod, w.astype(BF16))


def _cmul(a, t):
    w = a.shape[1] // 2
    ar, ai, tr, ti = a[:, :w], a[:, w:], t[:, :w], t[:, w:]
    return jnp.concatenate([ar * tr - ai * ti, ar * ti + ai * tr], axis=1)


def _s5_kernel(tm, x_ref, mod_ref, g_ref, bm_ref, cm_ref, e_ref, f_ref, a_ref, tri_ref, dskip_ref,
               wglu_ref, o_ref, u_sc, y_sc, s_sc):
    t = pl.program_id(1)
    x = x_ref[0]
    shift, scale, gate = _mod_rows(mod_ref, 1)
    u_sc[...] = _adaln(x, g_ref[...], shift, scale)
    L = S5_CHUNK
    n_blocks = bm_ref.shape[0]

    @pl.when(t == 0)
    def _():
        s_sc[...] = jnp.zeros_like(s_sc)

    tri = tri_ref[...]
    chunks = [slice(c * L, (c + 1) * L) for c in range(tm // L)]
    for kb in range(n_blocks):
        cols = slice(kb * LANES, (kb + 1) * LANES)
        bu = _dot(u_sc[:, cols].astype(BF16), bm_ref[kb]).astype(BF16)
        sums = [_dot(tri, _cmul(bu[rows], e_ref[kb])) for rows in chunks]
        carried = [s_sc[kb:kb + 1, :]]
        for p in sums:
            carried.append(_cmul(p[L - 1:L, :] + carried[-1], a_ref[kb]))
        s_sc[kb:kb + 1, :] = carried[-1]
        xs = jnp.concatenate([_cmul((p + s).astype(BF16), f_ref[kb]) for p, s in zip(sums, carried)],
                             axis=0)
        y_sc[:, cols] = _dot(xs, cm_ref[kb])
    y = y_sc[...] + dskip_ref[...] * u_sc[...]
    g = jax.nn.gelu(y, approximate=True)
    out = g * jax.nn.sigmoid(_dot(g.astype(BF16), wglu_ref[...]))
    o_ref[0] = x + (1.0 + gate) * out


def _s5_tables(lam_re, lam_im, log_dt, b_re, b_im, c_re, c_im):
    g_, n_ = lam_re.shape
    i_ = b_re.shape[2]
    gl = LANES // i_
    nb = g_ // gl
    dt = jnp.exp(log_dt)[:, None]
    ar, ai = lam_re, lam_im
    mag = jnp.exp(ar * dt)
    lb_re, lb_im = mag * jnp.cos(ai * dt), mag * jnp.sin(ai * dt)
    den = ar * ar + ai * ai
    nr, ni = lb_re - 1.0, lb_im
    k_re = (nr * ar + ni * ai) / den
    k_im = (ni * ar - nr * ai) / den
    bb_re = k_re[..., None] * b_re - k_im[..., None] * b_im
    bb_im = k_re[..., None] * b_im + k_im[..., None] * b_re
    eye = jnp.eye(gl, dtype=F32)
    bb = jnp.stack([bb_re, bb_im]).reshape(2, nb, gl, n_, i_)
    bm = jnp.einsum('gh,pkgni->kgiphn', eye, bb).reshape(nb, gl * i_, 2 * gl * n_)
    cc = jnp.stack([c_re, -c_im]).reshape(2, nb, gl, i_, n_)
    cm = jnp.einsum('gh,pkgin->kpgnhi', eye, cc).reshape(nb, 2 * gl * n_, gl * i_)

    def lbar_pow(p):
        e = jnp.exp(p[:, None, None] * (ar * dt)[None])
        th = p[:, None, None] * (ai * dt)[None]
        t = jnp.stack([e * jnp.cos(th), e * jnp.sin(th)], axis=1)
        return t.reshape(-1, 2, nb, gl, n_).transpose(2, 0, 1, 3, 4).reshape(nb, -1, 2 * gl * n_)

    mid = S5_CHUNK // 2
    j = jnp.arange(S5_CHUNK, dtype=F32) - mid
    return (bm.astype(BF16), cm.astype(BF16), lbar_pow(-j).astype(BF16), lbar_pow(j).astype(BF16),
            lbar_pow(jnp.full((1,), float(S5_CHUNK), F32)))


def _s5_layer(x, mod, gain, lam_re, lam_im, log_dt, b_re, b_im, c_re, c_im, d_skip, w_glu, tm=512):
    b, s, d = x.shape
    bm, cm, e_tab, f_tab, a_tab = _s5_tables(lam_re, lam_im, log_dt, b_re, b_im, c_re, c_im)
    nb, _, sw = bm.shape
    L = S5_CHUNK
    tri = (jnp.arange(L)[:, None] >= jnp.arange(L)[None, :]).astype(BF16)
    x_spec, mod_spec = _tile_specs(tm, d)
    return pl.pallas_call(
        functools.partial(_s5_kernel, tm),
        out_shape=jax.ShapeDtypeStruct(x.shape, F32),
        grid=(b, s // tm),
        in_specs=[x_spec, mod_spec, _const_spec((1, d)),
                  _const_spec((nb, LANES, sw)), _const_spec((nb, sw, LANES)),
                  _const_spec((nb, L, sw)), _const_spec((nb, L, sw)), _const_spec((nb, 1, sw)),
                  _const_spec((L, L)), _const_spec((1, d)), _const_spec((d, d))],
        out_specs=x_spec,
        scratch_shapes=[pltpu.VMEM((tm, d), F32), pltpu.VMEM((tm, d), F32), pltpu.VMEM((nb, sw), F32)],
        compiler_params=_params(2),
    )(x, mod, gain.reshape(1, d), bm, cm, e_tab, f_tab, a_tab, tri, d_skip.reshape(1, d),
      w_glu.astype(BF16))


def _conv_kernel(tm, x_ref, mod_ref, g_ref, win_ref, cw_ref, wout_ref, o_ref, cz_sc):
    t = pl.program_id(1)
    x = x_ref[0]
    d = x.shape[1]
    shift, scale, gate = _mod_rows(mod_ref, 1)
    h = _adaln(x, g_ref[...], shift, scale).astype(BF16)

    @pl.when(t == 0)
    def _():
        cz_sc[0:CONV_HALO, :] = jnp.zeros((CONV_HALO, d), F32)

    cz_sc[CONV_HALO:, :] = _dot(h, win_ref[:, d:2 * d]) * _dot(h, win_ref[:, 2 * d:3 * d])
    cw = cw_ref[...]
    conv = (cw[0:1] * cz_sc[CONV_HALO - 2:CONV_HALO - 2 + tm, :]
            + cw[1:2] * cz_sc[CONV_HALO - 1:CONV_HALO - 1 + tm, :]
            + cw[2:3] * cz_sc[CONV_HALO:, :])
    cz_sc[0:CONV_HALO, :] = cz_sc[tm:tm + CONV_HALO, :]
    gated = (_dot(h, win_ref[:, 0:d]) * conv).astype(BF16)
    o_ref[0] = x + (1.0 + gate) * _dot(gated, wout_ref[...])


def _conv_layer(x, mod, gain, w_in, conv_w, w_out, tm=512):
    b, s, d = x.shape
    kw = conv_w.shape[0]
    x_spec, mod_spec = _tile_specs(tm, d)
    return pl.pallas_call(
        functools.partial(_conv_kernel, tm),
        out_shape=jax.ShapeDtypeStruct(x.shape, F32),
        grid=(b, s // tm),
        in_specs=[x_spec, mod_spec, _const_spec((1, d)),
                  _const_spec((d, 3 * d)), _const_spec((kw, d)), _const_spec((d, d))],
        out_specs=x_spec,
        scratch_shapes=[pltpu.VMEM((tm + CONV_HALO, d), F32)],
        compiler_params=_params(2),
    )(x, mod, gain.reshape(1, d), w_in.astype(BF16), conv_w.reshape(kw, d), w_out.astype(BF16))


def kernel(x, c, ada_w, ada_b, norm_g, ffn_w_in, ffn_w_out, pool_w, pool_scale, fox_w_in, fox_b_f, fox_q_gain, fox_k_gain, fox_w_o, s5_lam_re, s5_lam_im, s5_log_dt, s5_b_re, s5_b_im, s5_c_re, s5_c_im, s5_d, s5_w_glu, conv_w_in, conv_w, conv_w_out):
    b, s, d = x.shape
    depth = ada_w.shape[0]
    mod_all = _modulation(c, ada_w, ada_b).reshape(depth, b, N_ADA, d)
    n_mixers = 4
    w_in_all, w_out_all = ffn_w_in.astype(BF16), ffn_w_out.astype(BF16)
    for i in range(depth):
        mod = mod_all[i]
        x = _ffn(x, mod, norm_g[i, 0], w_in_all, w_out_all, i, 0, sub=0)
        m, r = i % n_mixers, i // n_mixers
        if m == 0:
            x = _pool_layer(x, mod, norm_g[i, 1], pool_w[r], pool_scale[r])
        elif m == 1:
            q, k, vt, qa, ka = _fox_project(x, mod, norm_g[i, 1], fox_w_in[r], fox_b_f[r],
                                            fox_q_gain[r], fox_k_gain[r])
            o = _fox_attend(q, qa, k, ka, vt)
            x = _proj_out(o, x, mod, fox_w_o[r])
        elif m == 2:
            x = _s5_layer(x, mod, norm_g[i, 1], s5_lam_re[r], s5_lam_im[r], s5_log_dt[r],
                          s5_b_re[r], s5_b_im[r], s5_c_re[r], s5_c_im[r], s5_d[r], s5_w_glu[r])
        else:
            x = _conv_layer(x, mod, norm_g[i, 1], conv_w_in[r], conv_w[r], conv_w_out[r])
        x = _ffn(x, mod, norm_g[i, 2], w_in_all, w_out_all, i, 1, sub=2)
    return x
```

```python
import functools
import math

import jax
import jax.numpy as jnp
from jax import lax
from jax.experimental import pallas as pl
from jax.experimental.pallas import tpu as pltpu

F32 = jnp.float32
BF16 = jnp.bfloat16

NORM_EPS = 1e-6
N_SUBLAYERS = 3
N_ADA = 3 * N_SUBLAYERS
POOL_WINDOWS = (2, 4, 8, 16)
POOL_HALO = 16
FOX_HEADS = 16
FOX_HEAD_DIM = 64
S5_GROUP = 16
S5_STATE = 64
S5_CHUNK = 32
CONV_HALO = 8
LANES = 128
NEG_BIG = -1e30
LOG2E = math.log2(math.e)
AUG_LANES = 8
V_ROWS = FOX_HEAD_DIM + 16
FOX_TILE = 512

VMEM_LIMIT = 56 * 1024 * 1024


def _params(n_grid, vmem=VMEM_LIMIT):
    return pltpu.CompilerParams(dimension_semantics=("arbitrary",) * n_grid,
                                vmem_limit_bytes=vmem)


def _dot(a, b):
    return jnp.dot(a, b, preferred_element_type=F32)


def _adaln(x, gain, shift, scale):
    ms = jnp.mean(x * x, axis=-1, keepdims=True)
    y = x * lax.rsqrt(ms + NORM_EPS) * gain
    return y * (1.0 + scale) + shift


def _mod_rows(mod_ref, sub):
    m = mod_ref[0]
    return m[3 * sub:3 * sub + 1], m[3 * sub + 1:3 * sub + 2], m[3 * sub + 2:3 * sub + 3]


def _const_spec(shape):
    nd = len(shape)
    return pl.BlockSpec(shape, lambda *_: (0,) * nd, pipeline_mode=pl.Buffered(1))


def _tile_specs(tm, d):
    x_spec = pl.BlockSpec((1, tm, d), lambda b, t: (b, t, 0))
    mod_spec = pl.BlockSpec((1, N_ADA, d), lambda b, t: (b, 0, 0))
    return x_spec, mod_spec


def _mod_kernel(c_ref, w_ref, b_ref, o_ref):
    c = c_ref[...]
    cond = (c * jax.nn.sigmoid(c)).astype(BF16)
    o_ref[0] = _dot(cond, w_ref[0].astype(BF16)) + b_ref[0]


def _modulation(c, ada_w, ada_b):
    depth, d, n = ada_w.shape
    b = c.shape[0]
    tn = 1024
    return pl.pallas_call(
        _mod_kernel,
        out_shape=jax.ShapeDtypeStruct((depth, b, n), F32),
        grid=(depth, n // tn),
        in_specs=[pl.BlockSpec((b, d), lambda l, j: (0, 0)),
                  pl.BlockSpec((1, d, tn), lambda l, j: (l, 0, j)),
                  pl.BlockSpec((1, 1, tn), lambda l, j: (l, 0, j))],
        out_specs=pl.BlockSpec((1, b, tn), lambda l, j: (l, 0, j)),
        compiler_params=_params(2),
    )(c, ada_w, ada_b.reshape(depth, 1, n))


def _ffn_kernel(sub, tf, x_ref, mod_ref, g_ref, win_ref, wout_ref, o_ref, h_sc, a_sc):
    x = x_ref[0]
    shift, scale, gate = _mod_rows(mod_ref, sub)
    h_sc[...] = _adaln(x, g_ref[...], shift, scale).astype(BF16)
    dff = a_sc.shape[1]
    for j in range(dff // tf):
        h = h_sc[...]
        g = _dot(h, win_ref[:, j * tf:(j + 1) * tf].astype(BF16))
        u = _dot(h, win_ref[:, dff + j * tf:dff + (j + 1) * tf].astype(BF16))
        a_sc[:, j * tf:(j + 1) * tf] = (g * jax.nn.sigmoid(g) * u).astype(BF16)
    y = _dot(a_sc[...], wout_ref[...].astype(BF16))
    o_ref[0] = x + (0.5 * (1.0 + gate)) * y


def _ffn(x, mod, gain, w_in_all, w_out_all, layer, half, sub, tm=512, tf=256):
    b, s, d = x.shape
    dff = w_out_all.shape[2]
    x_spec, mod_spec = _tile_specs(tm, d)

    def pick(rows, cols):
        return pl.BlockSpec((None, None, rows, cols), lambda b_, t: (layer, half, 0, 0),
                            pipeline_mode=pl.Buffered(1))

    return pl.pallas_call(
        functools.partial(_ffn_kernel, sub, tf),
        out_shape=jax.ShapeDtypeStruct(x.shape, F32),
        grid=(b, s // tm),
        in_specs=[x_spec, mod_spec, _const_spec((1, d)), pick(d, 2 * dff), pick(dff, d)],
        out_specs=x_spec,
        scratch_shapes=[pltpu.VMEM((tm, d), BF16), pltpu.VMEM((tm, dff), BF16)],
        compiler_params=_params(2),
    )(x, mod, gain.reshape(1, d), w_in_all, w_out_all)


def _pool_kernel(tm, x_ref, mod_ref, g_ref, w_ref, sc_ref, o_ref, h_sc):
    t = pl.program_id(1)
    x = x_ref[0]
    d = x.shape[1]
    cg = d // len(POOL_WINDOWS)
    shift, scale, gate = _mod_rows(mod_ref, 1)

    @pl.when(t == 0)
    def _():
        h_sc[0:POOL_HALO, :] = jnp.zeros((POOL_HALO, d), F32)

    h_sc[POOL_HALO:, :] = _adaln(x, g_ref[...], shift, scale)
    pos = t * tm + lax.broadcasted_iota(jnp.int32, (tm, 1), 0)
    ys = []
    for gi, w in enumerate(POOL_WINDOWS):
        sl = slice(gi * cg, (gi + 1) * cg)
        cur = h_sc[POOL_HALO:, sl]
        acc = cur
        for k in range(1, w):
            acc = acc + h_sc[POOL_HALO - k:POOL_HALO - k + tm, sl]
        cnt = jnp.minimum(pos + 1, w).astype(F32)
        pooled = acc / cnt - cur
        ys.append(_dot(pooled.astype(BF16), w_ref[gi]))
    y = jnp.concatenate(ys, axis=1) * sc_ref[...]
    h_sc[0:POOL_HALO, :] = h_sc[tm:tm + POOL_HALO, :]
    o_ref[0] = x + (1.0 + gate) * y


def _pool_layer(x, mod, gain, w_grp, scale, tm=512):
    b, s, d = x.shape
    ng, cg, _ = w_grp.shape
    x_spec, mod_spec = _tile_specs(tm, d)
    return pl.pallas_call(
        functools.partial(_pool_kernel, tm),
        out_shape=jax.ShapeDtypeStruct(x.shape, F32),
        grid=(b, s // tm),
        in_specs=[x_spec, mod_spec, _const_spec((1, d)),
                  _const_spec((ng, cg, cg)), _const_spec((1, d))],
        out_specs=x_spec,
        scratch_shapes=[pltpu.VMEM((tm + POOL_HALO, d), F32)],
        compiler_params=_params(2),
    )(x, mod, gain.reshape(1, d), w_grp.astype(BF16), scale.reshape(1, d))


def _split3(v):
    p1 = v.astype(BF16)
    r1 = v - p1.astype(F32)
    p2 = r1.astype(BF16)
    p3 = (r1 - p2.astype(F32)).astype(BF16)
    return p1, p2, p3


def _head_rms(q, seg_ref, segt_ref):
    ms = _dot((q * q).astype(BF16), seg_ref[...])
    r = lax.rsqrt(ms + NORM_EPS)
    r_hi = r.astype(BF16)
    r_lo = (r - r_hi.astype(F32)).astype(BF16)
    rb = _dot(r_hi, segt_ref[...]) + _dot(r_lo, segt_ref[...])
    return q * rb


def _foxproj_kernel(x_ref, mod_ref, g_ref, wq_ref, wk_ref, wv_ref, wf_ref, bf_ref, qg_ref, kg_ref,
                    seg_ref, segt_ref, tri_ref, place_ref, ones_ref,
                    q_ref, k_ref, v_ref, qa_ref, ka_ref, carry_sc):
    t = pl.program_id(1)
    x = x_ref[0]
    shift, scale, _ = _mod_rows(mod_ref, 1)
    h = _adaln(x, g_ref[...], shift, scale).astype(BF16)
    q = _head_rms(_dot(h, wq_ref[...]), seg_ref, segt_ref)
    q_ref[0] = (q * qg_ref[...]).astype(BF16)
    k = _head_rms(_dot(h, wk_ref[...]), seg_ref, segt_ref)
    k_ref[0] = (k * kg_ref[...]).astype(BF16)
    v_ref[0, 0] = lax.dot_general(wv_ref[...], h, (((1,), (1,)), ((), ())),
                                  preferred_element_type=F32).astype(BF16)
    fl = _dot(h, wf_ref[...]) + bf_ref[...]
    logf = jnp.minimum(fl, 0.0) - jnp.log(1.0 + jnp.exp(-jnp.abs(fl)))

    @pl.when(t == 0)
    def _():
        carry_sc[...] = jnp.zeros_like(carry_sc)

    tri = tri_ref[...]
    cf = sum(_dot(tri, p) for p in _split3(logf)) + carry_sc[0:1, :]
    tm = cf.shape[0]
    carry_sc[0:1, :] = cf[tm - 1:tm, :]
    pieces = jnp.concatenate(_split3(cf * LOG2E), axis=1)
    aug = _dot(pieces, place_ref[...]) + ones_ref[...]
    qa_ref[0] = aug[:, :LANES].astype(BF16)
    ka_ref[0] = aug[:, LANES:].astype(BF16)


def _fox_project(x, mod, gain, w_in, b_f, q_gain, k_gain, tm=FOX_TILE):
    b, s, d = x.shape
    x_spec, mod_spec = _tile_specs(tm, d)
    w = w_in.astype(BF16)
    wf = jnp.zeros((d, LANES), BF16).at[:, :FOX_HEADS].set(w[:, 3 * d:])
    bf = jnp.zeros((1, LANES), F32).at[0, :FOX_HEADS].set(b_f)
    head_of = jnp.arange(d) // FOX_HEAD_DIM
    seg = (head_of[:, None] == jnp.arange(LANES)[None, :]).astype(F32)
    qg = jnp.tile(q_gain, FOX_HEADS).reshape(1, d) * (FOX_HEAD_DIM ** -0.5 * LOG2E)
    kg = jnp.tile(k_gain, FOX_HEADS).reshape(1, d)
    tri = (jnp.arange(tm)[:, None] >= jnp.arange(tm)[None, :]).astype(BF16)
    hh, ii = jnp.meshgrid(jnp.arange(FOX_HEADS), jnp.arange(3), indexing="ij")
    place = jnp.zeros((3 * LANES, 2 * LANES), F32)
    place = place.at[ii * LANES + hh, AUG_LANES * hh + 3 + ii].set(1.0)
    place = place.at[ii * LANES + hh, LANES + AUG_LANES * hh + ii].set(-1.0)
    ones = jnp.zeros((1, 2 * LANES), F32)
    ones = ones.at[0, AUG_LANES * hh + ii].set(1.0).at[0, LANES + AUG_LANES * hh + 3 + ii].set(1.0)
    act = jax.ShapeDtypeStruct((b, s, d), BF16)
    aug = jax.ShapeDtypeStruct((b, s, LANES), BF16)
    act_t = jax.ShapeDtypeStruct((b, s // tm, d, tm), BF16)
    return pl.pallas_call(
        _foxproj_kernel,
        out_shape=(act, act, act_t, aug, aug),
        grid=(b, s // tm),
        in_specs=[x_spec, mod_spec, _const_spec((1, d)),
                  _const_spec((d, d)), _const_spec((d, d)), _const_spec((d, d)),
                  _const_spec((d, LANES)), _const_spec((1, LANES)),
                  _const_spec((1, d)), _const_spec((1, d)),
                  _const_spec((d, LANES)), _const_spec((LANES, d)), _const_spec((tm, tm)),
                  _const_spec((3 * LANES, 2 * LANES)), _const_spec((1, 2 * LANES))],
        out_specs=(pl.BlockSpec((1, tm, d), lambda b_, t: (b_, t, 0)),) * 2
        + (pl.BlockSpec((1, 1, d, tm), lambda b_, t: (b_, t, 0, 0)),)
        + (pl.BlockSpec((1, tm, LANES), lambda b_, t: (b_, t, 0)),) * 2,
        scratch_shapes=[pltpu.VMEM((8, LANES), F32)],
        compiler_params=_params(2),
    )(x, mod, gain.reshape(1, d), w[:, :d], w[:, d:2 * d], w[:, 2 * d:3 * d].T, wf, bf, qg, kg,
      (seg / FOX_HEAD_DIM).astype(BF16), seg.T.astype(BF16), tri, place.astype(BF16), ones)


def _foxattn_kernel(TK, TQ, q_ref, qa_ref, k_ref, ka_ref, vt_ref, o_ref,
                    sa_sc, sb_sc, qt_sc, m_sc, acc_sc, diff_sc):
    first_step = (pl.program_id(0) == 0) & (pl.program_id(1) == 0) & (pl.program_id(2) == 0)
    pair = pl.program_id(1)
    qi = pl.program_id(2)
    lane = lax.broadcasted_iota(jnp.int32, (1, LANES), 1)
    q, qa = q_ref[0], qa_ref[0]
    zero = jnp.zeros_like(q)
    for hd in range(2):
        own = (lane < FOX_HEAD_DIM) == (hd == 0)
        bias_lo = AUG_LANES * (2 * pair + hd)
        own_bias = (lane >= bias_lo) & (lane < bias_lo + AUG_LANES)
        qh = jnp.concatenate([jnp.where(own, q, zero), jnp.where(own_bias, qa, zero)], axis=1)
        qt_sc[hd] = qh.T
        m_sc[hd] = jnp.full((8, TQ), NEG_BIG, F32)
        acc_sc[hd] = jnp.zeros((V_ROWS, TQ), F32)

    @pl.when(first_step)
    def _():
        diff_sc[...] = (lax.broadcasted_iota(jnp.int32, (TK, TQ), 0)
                        - lax.broadcasted_iota(jnp.int32, (TK, TQ), 1))

    every = slice(0, TQ)
    early, late = slice(0, TK), slice(TK, TQ)

    def scores(j, s_sc, qs=every):
        rows = pl.ds(pl.multiple_of(j * TK, TK), TK)
        kj = jnp.concatenate([k_ref[0, rows, :], ka_ref[0, rows, :]], axis=1)
        for hd in range(2):
            s_sc[hd, :, qs] = _dot(kj, qt_sc[hd, :, qs])

    def absorb(j, s_sc, masked, qs=every):
        vt = vt_ref[0, j]
        ones = jnp.ones((V_ROWS - FOX_HEAD_DIM, TK), BF16)
        if masked:
            visible = diff_sc[:, qs] <= qi * TQ - j * TK
        for hd in range(2):
            s = s_sc[hd, :, qs]
            if masked:
                s = jnp.where(visible, s, NEG_BIG)
            m_prev = m_sc[hd, :, qs]
            m_new = jnp.maximum(m_prev, jnp.max(s, axis=0, keepdims=True))
            p = jnp.exp2(s - m_new[0:1, :]).astype(BF16)
            v_aug = jnp.concatenate([vt[hd * FOX_HEAD_DIM:(hd + 1) * FOX_HEAD_DIM, :], ones], axis=0)
            alpha = jnp.exp2(m_prev - m_new)[0:1, :]
            acc_sc[hd, :, qs] = alpha * acc_sc[hd, :, qs] + _dot(v_aug, p)
            m_sc[hd, :, qs] = m_new

    scores(0, sa_sc)

    def two_tiles(jj, carry):
        j = 2 * jj
        scores(j + 1, sb_sc)
        absorb(j, sa_sc, False)
        scores(j + 2, sa_sc)
        absorb(j + 1, sb_sc, False)
        return carry

    lax.fori_loop(0, qi, two_tiles, 0)
    scores(2 * qi + 1, sb_sc, late)
    absorb(2 * qi, sa_sc, True, early)
    absorb(2 * qi, sa_sc, False, late)
    absorb(2 * qi + 1, sb_sc, True, late)

    out_t = jnp.concatenate(
        [acc_sc[hd, 0:FOX_HEAD_DIM, :] / acc_sc[hd, FOX_HEAD_DIM:FOX_HEAD_DIM + 1, :] for hd in range(2)],
        axis=0)
    o_ref[0] = out_t.T.astype(o_ref.dtype)


def _fox_attend(q, qa, k, ka, vt, tk=FOX_TILE):
    b, s, d = q.shape
    pairs = d // LANES
    tq = 2 * tk
    assert vt.shape == (b, s // tk, d, tk)
    return pl.pallas_call(
        functools.partial(_foxattn_kernel, tk, tq),
        out_shape=jax.ShapeDtypeStruct((b, s, d), BF16),
        grid=(b, pairs, s // tq),
        in_specs=[pl.BlockSpec((1, tq, LANES), lambda b_, p, i: (b_, i, p)),
                  pl.BlockSpec((1, tq, LANES), lambda b_, p, i: (b_, i, 0)),
                  pl.BlockSpec((1, s, LANES), lambda b_, p, i: (b_, 0, p)),
                  pl.BlockSpec((1, s, LANES), lambda b_, p, i: (b_, 0, 0)),
                  pl.BlockSpec((1, s // tk, LANES, tk), lambda b_, p, i: (b_, 0, p, 0))],
        out_specs=pl.BlockSpec((1, tq, LANES), lambda b_, p, i: (b_, i, p)),
        scratch_shapes=[pltpu.VMEM((2, tk, tq), F32), pltpu.VMEM((2, tk, tq), F32),
                        pltpu.VMEM((2, 2 * LANES, tq), BF16),
                        pltpu.VMEM((2, 8, tq), F32), pltpu.VMEM((2, V_ROWS, tq), F32),
                        pltpu.VMEM((tk, tq), jnp.int32)],
        compiler_params=_params(3),
    )(q, qa, k, ka, vt)


def _proj_out_kernel(a_ref, x_ref, mod_ref, w_ref, o_ref):
    _, _, gate = _mod_rows(mod_ref, 1)
    o_ref[0] = x_ref[0] + (1.0 + gate) * _dot(a_ref[0], w_ref[...])


def _proj_out(a, x, mod, w, tm=512):
    b, s, d = x.shape
    x_spec, mod_spec = _tile_specs(tm, d)
    return pl.pallas_call(
        _proj_out_kernel,
        out_shape=jax.ShapeDtypeStruct(x.shape, F32),
        grid=(b, s // tm),
        in_specs=[x_spec, x_spec, mod_spec, _const_spec((d, d))],
        out_specs=x_spec,
        compiler_params=_params(2),
    )(a, x, mod, w.astype(BF16))


def _cmul(a, t):
    w = a.shape[1] // 2
    ar, ai, tr, ti = a[:, :w], a[:, w:], t[:, :w], t[:, w:]
    return jnp.concatenate([ar * tr - ai * ti, ar * ti + ai * tr], axis=1)


def _s5_kernel(tm, x_ref, mod_ref, g_ref, bm_ref, cm_ref, e_ref, f_ref, a_ref, tri_ref, dskip_ref,
               wglu_ref, o_ref, u_sc, y_sc, s_sc):
    t = pl.program_id(1)
    x = x_ref[0]
    shift, scale, gate = _mod_rows(mod_ref, 1)
    u_sc[...] = _adaln(x, g_ref[...], shift, scale)
    L = S5_CHUNK
    n_blocks = bm_ref.shape[0]

    @pl.when(t == 0)
    def _():
        s_sc[...] = jnp.zeros_like(s_sc)

    tri = tri_ref[...]
    chunks = [slice(c * L, (c + 1) * L) for c in range(tm // L)]
    for kb in range(n_blocks):
        cols = slice(kb * LANES, (kb + 1) * LANES)
        bu = _dot(u_sc[:, cols].astype(BF16), bm_ref[kb]).astype(BF16)
        sums = [_dot(tri, _cmul(bu[rows], e_ref[kb])) for rows in chunks]
        carried = [s_sc[kb:kb + 1, :]]
        for p in sums:
            carried.append(_cmul(p[L - 1:L, :] + carried[-1], a_ref[kb]))
        s_sc[kb:kb + 1, :] = carried[-1]
        xs = jnp.concatenate([_cmul((p + s).astype(BF16), f_ref[kb]) for p, s in zip(sums, carried)],
                             axis=0)
        y_sc[:, cols] = _dot(xs, cm_ref[kb])
    y = y_sc[...] + dskip_ref[...] * u_sc[...]
    g = jax.nn.gelu(y, approximate=True)
    out = g * jax.nn.sigmoid(_dot(g.astype(BF16), wglu_ref[...]))
    o_ref[0] = x + (1.0 + gate) * out


def _s5_tables(lam_re, lam_im, log_dt, b_re, b_im, c_re, c_im):
    g_, n_ = lam_re.shape
    i_ = b_re.shape[2]
    gl = LANES // i_
    nb = g_ // gl
    dt = jnp.exp(log_dt)[:, None]
    ar, ai = lam_re, lam_im
    mag = jnp.exp(ar * dt)
    lb_re, lb_im = mag * jnp.cos(ai * dt), mag * jnp.sin(ai * dt)
    den = ar * ar + ai * ai
    nr, ni = lb_re - 1.0, lb_im
    k_re = (nr * ar + ni * ai) / den
    k_im = (ni * ar - nr * ai) / den
    bb_re = k_re[..., None] * b_re - k_im[..., None] * b_im
    bb_im = k_re[..., None] * b_im + k_im[..., None] * b_re
    eye = jnp.eye(gl, dtype=F32)
    bb = jnp.stack([bb_re, bb_im]).reshape(2, nb, gl, n_, i_)
    bm = jnp.einsum('gh,pkgni->kgiphn', eye, bb).reshape(nb, gl * i_, 2 * gl * n_)
    cc = jnp.stack([c_re, -c_im]).reshape(2, nb, gl, i_, n_)
    cm = jnp.einsum('gh,pkgin->kpgnhi', eye, cc).reshape(nb, 2 * gl * n_, gl * i_)

    def lbar_pow(p):
        e = jnp.exp(p[:, None, None] * (ar * dt)[None])
        th = p[:, None, None] * (ai * dt)[None]
        t = jnp.stack([e * jnp.cos(th), e * jnp.sin(th)], axis=1)
        return t.reshape(-1, 2, nb, gl, n_).transpose(2, 0, 1, 3, 4).reshape(nb, -1, 2 * gl * n_)

    mid = S5_CHUNK // 2
    j = jnp.arange(S5_CHUNK, dtype=F32) - mid
    return (bm.astype(BF16), cm.astype(BF16), lbar_pow(-j).astype(BF16), lbar_pow(j).astype(BF16),
            lbar_pow(jnp.full((1,), float(S5_CHUNK), F32)))


def _s5_layer(x, mod, gain, lam_re, lam_im, log_dt, b_re, b_im, c_re, c_im, d_skip, w_glu, tm=512):
    b, s, d = x.shape
    bm, cm, e_tab, f_tab, a_tab = _s5_tables(lam_re, lam_im, log_dt, b_re, b_im, c_re, c_im)
    nb, _, sw = bm.shape
    L = S5_CHUNK
    tri = (jnp.arange(L)[:, None] >= jnp.arange(L)[None, :]).astype(BF16)
    x_spec, mod_spec = _tile_specs(tm, d)
    return pl.pallas_call(
        functools.partial(_s5_kernel, tm),
        out_shape=jax.ShapeDtypeStruct(x.shape, F32),
        grid=(b, s // tm),
        in_specs=[x_spec, mod_spec, _const_spec((1, d)),
                  _const_spec((nb, LANES, sw)), _const_spec((nb, sw, LANES)),
                  _const_spec((nb, L, sw)), _const_spec((nb, L, sw)), _const_spec((nb, 1, sw)),
                  _const_spec((L, L)), _const_spec((1, d)), _const_spec((d, d))],
        out_specs=x_spec,
        scratch_shapes=[pltpu.VMEM((tm, d), F32), pltpu.VMEM((tm, d), F32), pltpu.VMEM((nb, sw), F32)],
        compiler_params=_params(2),
    )(x, mod, gain.reshape(1, d), bm, cm, e_tab, f_tab, a_tab, tri, d_skip.reshape(1, d),
      w_glu.astype(BF16))


def _conv_kernel(tm, x_ref, mod_ref, g_ref, win_ref, cw_ref, wout_ref, o_ref, cz_sc):
    t = pl.program_id(1)
    x = x_ref[0]
    d = x.shape[1]
    shift, scale, gate = _mod_rows(mod_ref, 1)
    h = _adaln(x, g_ref[...], shift, scale).astype(BF16)

    @pl.when(t == 0)
    def _():
        cz_sc[0:CONV_HALO, :] = jnp.zeros((CONV_HALO, d), F32)

    cz_sc[CONV_HALO:, :] = _dot(h, win_ref[:, d:2 * d]) * _dot(h, win_ref[:, 2 * d:3 * d])
    cw = cw_ref[...]
    conv = (cw[0:1] * cz_sc[CONV_HALO - 2:CONV_HALO - 2 + tm, :]
            + cw[1:2] * cz_sc[CONV_HALO - 1:CONV_HALO - 1 + tm, :]
            + cw[2:3] * cz_sc[CONV_HALO:, :])
    cz_sc[0:CONV_HALO, :] = cz_sc[tm:tm + CONV_HALO, :]
    gated = (_dot(h, win_ref[:, 0:d]) * conv).astype(BF16)
    o_ref[0] = x + (1.0 + gate) * _dot(gated, wout_ref[...])


def _conv_layer(x, mod, gain, w_in, conv_w, w_out, tm=512):
    b, s, d = x.shape
    kw = conv_w.shape[0]
    x_spec, mod_spec = _tile_specs(tm, d)
    return pl.pallas_call(
        functools.partial(_conv_kernel, tm),
        out_shape=jax.ShapeDtypeStruct(x.shape, F32),
        grid=(b, s // tm),
        in_specs=[x_spec, mod_spec, _const_spec((1, d)),
                  _const_spec((d, 3 * d)), _const_spec((kw, d)), _const_spec((d, d))],
        out_specs=x_spec,
        scratch_shapes=[pltpu.VMEM((tm + CONV_HALO, d), F32)],
        compiler_params=_params(2),
    )(x, mod, gain.reshape(1, d), w_in.astype(BF16), conv_w.reshape(kw, d), w_out.astype(BF16))


def kernel(x, c, ada_w, ada_b, norm_g, ffn_w_in, ffn_w_out, pool_w, pool_scale, fox_w_in, fox_b_f, fox_q_gain, fox_k_gain, fox_w_o, s5_lam_re, s5_lam_im, s5_log_dt, s5_b_re, s5_b_im, s5_c_re, s5_c_im, s5_d, s5_w_glu, conv_w_in, conv_w, conv_w_out):
    b, s, d = x.shape
    depth = ada_w.shape[0]
    mod_all = _modulation(c, ada_w, ada_b).reshape(depth, b, N_ADA, d)
    n_mixers = 4
    w_in_all, w_out_all = ffn_w_in, ffn_w_out
    for i in range(depth):
        mod = mod_all[i]
        x = _ffn(x, mod, norm_g[i, 0], w_in_all, w_out_all, i, 0, sub=0)
        m, r = i % n_mixers, i // n_mixers
        if m == 0:
            x = _pool_layer(x, mod, norm_g[i, 1], pool_w[r], pool_scale[r])
        elif m == 1:
            q, k, vt, qa, ka = _fox_project(x, mod, norm_g[i, 1], fox_w_in[r], fox_b_f[r],
                                            fox_q_gain[r], fox_k_gain[r])
            o = _fox_attend(q, qa, k, ka, vt)
            x = _proj_out(o, x, mod, fox_w_o[r])
        elif m == 2:
            x = _s5_layer(x, mod, norm_g[i, 1], s5_lam_re[r], s5_lam_im[r], s5_log_dt[r],
                          s5_b_re[r], s5_b_im[r], s5_c_re[r], s5_c_im[r], s5_d[r], s5_w_glu[r])
        else:
            x = _conv_layer(x, mod, norm_g[i, 1], conv_w_in[r], conv_w[r], conv_w_out[r])
        x = _ffn(x, mod, norm_g[i, 2], w_in_all, w_out_all, i, 1, sub=2)
    return x
```

```python
import functools
import math

import jax
import jax.numpy as jnp
from jax import lax
from jax.experimental import pallas as pl
from jax.experimental.pallas import tpu as pltpu

F32 = jnp.float32
BF16 = jnp.bfloat16

NORM_EPS = 1e-6
N_SUBLAYERS = 3
N_ADA = 3 * N_SUBLAYERS
POOL_WINDOWS = (2, 4, 8, 16)
POOL_HALO = 16
FOX_HEADS = 16
FOX_HEAD_DIM = 64
S5_GROUP = 16
S5_STATE = 64
S5_CHUNK = 32
CONV_HALO = 8
LANES = 128
NEG_BIG = -1e30
LOG2E = math.log2(math.e)
AUG_LANES = 8
V_ROWS = FOX_HEAD_DIM + 16
FOX_TILE = 512

VMEM_LIMIT = 56 * 1024 * 1024


def _params(n_grid, vmem=VMEM_LIMIT):
    return pltpu.CompilerParams(dimension_semantics=("arbitrary",) * n_grid,
                                vmem_limit_bytes=vmem)


def _dot(a, b):
    return jnp.dot(a, b, preferred_element_type=F32)


def _adaln(x, gain, shift, scale):
    ms = jnp.mean(x * x, axis=-1, keepdims=True)
    y = x * lax.rsqrt(ms + NORM_EPS) * gain
    return y * (1.0 + scale) + shift


def _mod_rows(mod_ref, sub):
    m = mod_ref[0]
    return m[3 * sub:3 * sub + 1], m[3 * sub + 1:3 * sub + 2], m[3 * sub + 2:3 * sub + 3]


def _const_spec(shape):
    nd = len(shape)
    return pl.BlockSpec(shape, lambda *_: (0,) * nd, pipeline_mode=pl.Buffered(1))


def _tile_specs(tm, d):
    x_spec = pl.BlockSpec((1, tm, d), lambda b, t: (b, t, 0))
    mod_spec = pl.BlockSpec((1, N_ADA, d), lambda b, t: (b, 0, 0))
    return x_spec, mod_spec


def _mod_kernel(c_ref, w_ref, b_ref, o_ref):
    c = c_ref[...]
    cond = (c * jax.nn.sigmoid(c)).astype(BF16)
    o_ref[0] = _dot(cond, w_ref[0].astype(BF16)) + b_ref[0]


def _modulation(c, ada_w, ada_b):
    depth, d, n = ada_w.shape
    b = c.shape[0]
    tn = 1024
    return pl.pallas_call(
        _mod_kernel,
        out_shape=jax.ShapeDtypeStruct((depth, b, n), F32),
        grid=(depth, n // tn),
        in_specs=[pl.BlockSpec((b, d), lambda l, j: (0, 0)),
                  pl.BlockSpec((1, d, tn), lambda l, j: (l, 0, j)),
                  pl.BlockSpec((1, 1, tn), lambda l, j: (l, 0, j))],
        out_specs=pl.BlockSpec((1, b, tn), lambda l, j: (l, 0, j)),
        compiler_params=_params(2),
    )(c, ada_w, ada_b.reshape(depth, 1, n))


def _ffn_kernel(sub, tf, x_ref, mod_ref, g_ref, win_ref, wout_ref, o_ref, h_sc, a_sc):
    x = x_ref[0]
    shift, scale, gate = _mod_rows(mod_ref, sub)
    h_sc[...] = _adaln(x, g_ref[...], shift, scale).astype(BF16)
    dff = a_sc.shape[1]
    for j in range(dff // tf):
        h = h_sc[...]
        g = _dot(h, win_ref[:, j * tf:(j + 1) * tf].astype(BF16))
        u = _dot(h, win_ref[:, dff + j * tf:dff + (j + 1) * tf].astype(BF16))
        a_sc[:, j * tf:(j + 1) * tf] = (g * jax.nn.sigmoid(g) * u).astype(BF16)
    y = _dot(a_sc[...], wout_ref[...].astype(BF16))
    o_ref[0] = x + (0.5 * (1.0 + gate)) * y


def _ffn(x, mod, gain, w_in_all, w_out_all, layer, half, sub, tm=512, tf=256):
    b, s, d = x.shape
    dff = w_out_all.shape[2]
    x_spec, mod_spec = _tile_specs(tm, d)

    def pick(rows, cols):
        return pl.BlockSpec((None, None, rows, cols), lambda b_, t: (layer, half, 0, 0),
                            pipeline_mode=pl.Buffered(1))

    return pl.pallas_call(
        functools.partial(_ffn_kernel, sub, tf),
        out_shape=jax.ShapeDtypeStruct(x.shape, F32),
        grid=(b, s // tm),
        in_specs=[x_spec, mod_spec, _const_spec((1, d)), pick(d, 2 * dff), pick(dff, d)],
        out_specs=x_spec,
        scratch_shapes=[pltpu.VMEM((tm, d), BF16), pltpu.VMEM((tm, dff), BF16)],
        compiler_params=_params(2),
    )(x, mod, gain.reshape(1, d), w_in_all, w_out_all)


def _pool_kernel(tm, x_ref, mod_ref, g_ref, w_ref, sc_ref, o_ref, h_sc):
    t = pl.program_id(1)
    x = x_ref[0]
    d = x.shape[1]
    cg = d // len(POOL_WINDOWS)
    shift, scale, gate = _mod_rows(mod_ref, 1)

    @pl.when(t == 0)
    def _():
        h_sc[0:POOL_HALO, :] = jnp.zeros((POOL_HALO, d), F32)

    h_sc[POOL_HALO:, :] = _adaln(x, g_ref[...], shift, scale)
    pos = t * tm + lax.broadcasted_iota(jnp.int32, (tm, 1), 0)
    ys = []
    for gi, w in enumerate(POOL_WINDOWS):
        sl = slice(gi * cg, (gi + 1) * cg)
        cur = h_sc[POOL_HALO:, sl]
        acc = cur
        for k in range(1, w):
            acc = acc + h_sc[POOL_HALO - k:POOL_HALO - k + tm, sl]
        cnt = jnp.minimum(pos + 1, w).astype(F32)
        pooled = acc / cnt - cur
        ys.append(_dot(pooled.astype(BF16), w_ref[gi]))
    y = jnp.concatenate(ys, axis=1) * sc_ref[...]
    h_sc[0:POOL_HALO, :] = h_sc[tm:tm + POOL_HALO, :]
    o_ref[0] = x + (1.0 + gate) * y


def _pool_layer(x, mod, gain, w_grp, scale, tm=512):
    b, s, d = x.shape
    ng, cg, _ = w_grp.shape
    x_spec, mod_spec = _tile_specs(tm, d)
    return pl.pallas_call(
        functools.partial(_pool_kernel, tm),
        out_shape=jax.ShapeDtypeStruct(x.shape, F32),
        grid=(b, s // tm),
        in_specs=[x_spec, mod_spec, _const_spec((1, d)),
                  _const_spec((ng, cg, cg)), _const_spec((1, d))],
        out_specs=x_spec,
        scratch_shapes=[pltpu.VMEM((tm + POOL_HALO, d), F32)],
        compiler_params=_params(2),
    )(x, mod, gain.reshape(1, d), w_grp.astype(BF16), scale.reshape(1, d))


def _split3(v):
    p1 = v.astype(BF16)
    r1 = v - p1.astype(F32)
    p2 = r1.astype(BF16)
    p3 = (r1 - p2.astype(F32)).astype(BF16)
    return p1, p2, p3


def _head_rms(q, seg_ref, segt_ref):
    ms = _dot((q * q).astype(BF16), seg_ref[...])
    r = lax.rsqrt(ms + NORM_EPS)
    r_hi = r.astype(BF16)
    r_lo = (r - r_hi.astype(F32)).astype(BF16)
    rb = _dot(r_hi, segt_ref[...]) + _dot(r_lo, segt_ref[...])
    return q * rb


def _foxproj_kernel(x_ref, mod_ref, g_ref, wq_ref, wk_ref, wv_ref, wf_ref, bf_ref, qg_ref, kg_ref,
                    seg_ref, segt_ref, tri_ref, place_ref, ones_ref,
                    q_ref, k_ref, v_ref, qa_ref, ka_ref, carry_sc):
    t = pl.program_id(1)
    x = x_ref[0]
    shift, scale, _ = _mod_rows(mod_ref, 1)
    h = _adaln(x, g_ref[...], shift, scale).astype(BF16)
    nt_dims = (((1,), (1,)), ((), ()))
    qt = lax.dot_general(wq_ref[...], h, nt_dims, preferred_element_type=F32)
    for hd in range(FOX_HEADS):
        rows = slice(hd * FOX_HEAD_DIM, (hd + 1) * FOX_HEAD_DIM)
        qh = qt[rows, :]
        r = lax.rsqrt(jnp.mean(qh * qh, axis=0, keepdims=True) + NORM_EPS)
        q_ref[0, 0, rows, :] = (qh * r * qg_ref[rows, :]).astype(BF16)
    k = _head_rms(_dot(h, wk_ref[...]), seg_ref, segt_ref)
    k_ref[0] = (k * kg_ref[...]).astype(BF16)
    v_ref[0, 0] = lax.dot_general(wv_ref[...], h, nt_dims, preferred_element_type=F32).astype(BF16)
    fl = _dot(h, wf_ref[...]) + bf_ref[...]
    logf = jnp.minimum(fl, 0.0) - jnp.log(1.0 + jnp.exp(-jnp.abs(fl)))

    @pl.when(t == 0)
    def _():
        carry_sc[...] = jnp.zeros_like(carry_sc)

    tri = tri_ref[...]
    cf = sum(_dot(tri, p) for p in _split3(logf)) + carry_sc[0:1, :]
    tm = cf.shape[0]
    carry_sc[0:1, :] = cf[tm - 1:tm, :]
    pieces = jnp.concatenate(_split3(cf * LOG2E), axis=1)
    aug = _dot(pieces, place_ref[...]) + ones_ref[...]
    qa_ref[0, 0] = aug[:, :LANES].T.astype(BF16)
    ka_ref[0] = aug[:, LANES:].astype(BF16)


def _fox_project(x, mod, gain, w_in, b_f, q_gain, k_gain, tm=FOX_TILE):
    b, s, d = x.shape
    x_spec, mod_spec = _tile_specs(tm, d)
    w = w_in.astype(BF16)
    wf = jnp.zeros((d, LANES), BF16).at[:, :FOX_HEADS].set(w[:, 3 * d:])
    bf = jnp.zeros((1, LANES), F32).at[0, :FOX_HEADS].set(b_f)
    head_of = jnp.arange(d) // FOX_HEAD_DIM
    seg = (head_of[:, None] == jnp.arange(LANES)[None, :]).astype(F32)
    qg = jnp.tile(q_gain, FOX_HEADS).reshape(d, 1) * (FOX_HEAD_DIM ** -0.5 * LOG2E)
    kg = jnp.tile(k_gain, FOX_HEADS).reshape(1, d)
    tri = (jnp.arange(tm)[:, None] >= jnp.arange(tm)[None, :]).astype(BF16)
    hh, ii = jnp.meshgrid(jnp.arange(FOX_HEADS), jnp.arange(3), indexing="ij")
    place = jnp.zeros((3 * LANES, 2 * LANES), F32)
    place = place.at[ii * LANES + hh, AUG_LANES * hh + 3 + ii].set(1.0)
    place = place.at[ii * LANES + hh, LANES + AUG_LANES * hh + ii].set(-1.0)
    ones = jnp.zeros((1, 2 * LANES), F32)
    ones = ones.at[0, AUG_LANES * hh + ii].set(1.0).at[0, LANES + AUG_LANES * hh + 3 + ii].set(1.0)
    act = jax.ShapeDtypeStruct((b, s, d), BF16)
    aug = jax.ShapeDtypeStruct((b, s, LANES), BF16)
    act_t = jax.ShapeDtypeStruct((b, s // tm, d, tm), BF16)
    aug_t = jax.ShapeDtypeStruct((b, s // tm, LANES, tm), BF16)
    token_major = lambda width: pl.BlockSpec((1, tm, width), lambda b_, t: (b_, t, 0))
    channel_major = lambda rows: pl.BlockSpec((1, 1, rows, tm), lambda b_, t: (b_, t, 0, 0))
    return pl.pallas_call(
        _foxproj_kernel,
        out_shape=(act_t, act, act_t, aug_t, aug),
        grid=(b, s // tm),
        in_specs=[x_spec, mod_spec, _const_spec((1, d)),
                  _const_spec((d, d)), _const_spec((d, d)), _const_spec((d, d)),
                  _const_spec((d, LANES)), _const_spec((1, LANES)),
                  _const_spec((d, 1)), _const_spec((1, d)),
                  _const_spec((d, LANES)), _const_spec((LANES, d)), _const_spec((tm, tm)),
                  _const_spec((3 * LANES, 2 * LANES)), _const_spec((1, 2 * LANES))],
        out_specs=(channel_major(d), token_major(d), channel_major(d),
                   channel_major(LANES), token_major(LANES)),
        scratch_shapes=[pltpu.VMEM((8, LANES), F32)],
        compiler_params=_params(2),
    )(x, mod, gain.reshape(1, d), w[:, :d].T, w[:, d:2 * d], w[:, 2 * d:3 * d].T, wf, bf, qg, kg,
      (seg / FOX_HEAD_DIM).astype(BF16), seg.T.astype(BF16), tri, place.astype(BF16), ones)


def _foxattn_kernel(TK, TQ, q_ref, qa_ref, qn_ref, qan_ref, k_ref, ka_ref, vt_ref, o_ref,
                    sa_sc, sb_sc, qt_sc, qtn_sc, m_sc, acc_sc, diff_sc):
    first_step = (pl.program_id(0) == 0) & (pl.program_id(1) == 0) & (pl.program_id(2) == 0)
    pair = pl.program_id(1)
    qi = pl.program_id(2)
    chan = lax.broadcasted_iota(jnp.int32, (LANES, 1), 0)
    zero = jnp.zeros((LANES, TK), BF16)
    for hd in range(2):
        own = (chan < FOX_HEAD_DIM) == (hd == 0)
        bias_lo = AUG_LANES * (2 * pair + hd)
        own_bias = (chan >= bias_lo) & (chan < bias_lo + AUG_LANES)
        for src, src_a, dst in ((q_ref, qa_ref, qt_sc), (qn_ref, qan_ref, qtn_sc)):
            for i in range(TQ // TK):
                cols = slice(i * TK, (i + 1) * TK)
                dst[hd, 0:LANES, cols] = jnp.where(own, src[0, i], zero)
                dst[hd, LANES:2 * LANES, cols] = jnp.where(own_bias, src_a[0, i], zero)
        m_sc[hd] = jnp.full((8, TQ), NEG_BIG, F32)
        acc_sc[hd] = jnp.zeros((V_ROWS, TQ), F32)

    @pl.when(first_step)
    def _():
        diff_sc[...] = (lax.broadcasted_iota(jnp.int32, (TK, TQ), 0)
                        - lax.broadcasted_iota(jnp.int32, (TK, TQ), 1))

    every = slice(0, TQ)
    early, late = slice(0, TK), slice(TK, TQ)

    def scores(j, s_sc, qs=every, queries=qt_sc):
        rows = pl.ds(pl.multiple_of(j * TK, TK), TK)
        kj = jnp.concatenate([k_ref[0, rows, :], ka_ref[0, rows, :]], axis=1)
        for hd in range(2):
            s_sc[hd, :, qs] = _dot(kj, queries[hd, :, qs])

    def absorb(j, s_sc, masked, qs=every):
        vt = vt_ref[0, j]
        ones = jnp.ones((V_ROWS - FOX_HEAD_DIM, TK), BF16)
        if masked:
            visible = diff_sc[:, qs] <= qi * TQ - j * TK
        for hd in range(2):
            s = s_sc[hd, :, qs]
            if masked:
                s = jnp.where(visible, s, NEG_BIG)
            m_prev = m_sc[hd, :, qs]
            m_new = jnp.maximum(m_prev, jnp.max(s, axis=0, keepdims=True))
            p = jnp.exp2(s - m_new[0:1, :]).astype(BF16)
            v_aug = jnp.concatenate([vt[hd * FOX_HEAD_DIM:(hd + 1) * FOX_HEAD_DIM, :], ones], axis=0)
            alpha = jnp.exp2(m_prev - m_new)[0:1, :]
            acc_sc[hd, :, qs] = alpha * acc_sc[hd, :, qs] + _dot(v_aug, p)
            m_sc[hd, :, qs] = m_new

    @pl.when(qi == 0)
    def _():
        scores(0, sa_sc)

    def two_tiles(jj, carry):
        j = 2 * jj
        scores(j + 1, sb_sc)
        absorb(j, sa_sc, False)
        scores(j + 2, sa_sc)
        absorb(j + 1, sb_sc, False)
        return carry

    lax.fori_loop(0, qi, two_tiles, 0)
    scores(2 * qi + 1, sb_sc, late)
    absorb(2 * qi, sa_sc, True, early)
    scores(0, sa_sc, early, qtn_sc)
    absorb(2 * qi, sa_sc, False, late)
    scores(0, sa_sc, late, qtn_sc)
    absorb(2 * qi + 1, sb_sc, True, late)

    out_t = jnp.concatenate(
        [acc_sc[hd, 0:FOX_HEAD_DIM, :] / acc_sc[hd, FOX_HEAD_DIM:FOX_HEAD_DIM + 1, :] for hd in range(2)],
        axis=0)
    o_ref[0] = out_t.T.astype(o_ref.dtype)


def _fox_attend(qt, qat, k, ka, vt, tk=FOX_TILE):
    b, s, d = k.shape
    pairs = d // LANES
    tq = 2 * tk
    nq = s // tq
    assert qt.shape == vt.shape == (b, s // tk, d, tk)
    return pl.pallas_call(
        functools.partial(_foxattn_kernel, tk, tq),
        out_shape=jax.ShapeDtypeStruct((b, s, d), BF16),
        grid=(b, pairs, s // tq),
        in_specs=[pl.BlockSpec((1, tq // tk, LANES, tk), lambda b_, p, i: (b_, i, p, 0)),
                  pl.BlockSpec((1, tq // tk, LANES, tk), lambda b_, p, i: (b_, i, 0, 0)),
                  pl.BlockSpec((1, tq // tk, LANES, tk), lambda b_, p, i: (b_, jnp.minimum(i + 1, nq - 1), p, 0)),
                  pl.BlockSpec((1, tq // tk, LANES, tk), lambda b_, p, i: (b_, jnp.minimum(i + 1, nq - 1), 0, 0)),
                  pl.BlockSpec((1, s, LANES), lambda b_, p, i: (b_, 0, p)),
                  pl.BlockSpec((1, s, LANES), lambda b_, p, i: (b_, 0, 0)),
                  pl.BlockSpec((1, s // tk, LANES, tk), lambda b_, p, i: (b_, 0, p, 0))],
        out_specs=pl.BlockSpec((1, tq, LANES), lambda b_, p, i: (b_, i, p)),
        scratch_shapes=[pltpu.VMEM((2, tk, tq), F32), pltpu.VMEM((2, tk, tq), F32),
                        pltpu.VMEM((2, 2 * LANES, tq), BF16), pltpu.VMEM((2, 2 * LANES, tq), BF16),
                        pltpu.VMEM((2, 8, tq), F32), pltpu.VMEM((2, V_ROWS, tq), F32),
                        pltpu.VMEM((tk, tq), jnp.int32)],
        compiler_params=_params(3),
    )(qt, qat, qt, qat, k, ka, vt)


def _proj_out_kernel(a_ref, x_ref, mod_ref, w_ref, o_ref):
    _, _, gate = _mod_rows(mod_ref, 1)
    o_ref[0] = x_ref[0] + (1.0 + gate) * _dot(a_ref[0], w_ref[...])


def _proj_out(a, x, mod, w, tm=512):
    b, s, d = x.shape
    x_spec, mod_spec = _tile_specs(tm, d)
    return pl.pallas_call(
        _proj_out_kernel,
        out_shape=jax.ShapeDtypeStruct(x.shape, F32),
        grid=(b, s // tm),
        in_specs=[x_spec, x_spec, mod_spec, _const_spec((d, d))],
        out_specs=x_spec,
        compiler_params=_params(2),
    )(a, x, mod, w.astype(BF16))


def _cmul(a, t):
    w = a.shape[1] // 2
    ar, ai, tr, ti = a[:, :w], a[:, w:], t[:, :w], t[:, w:]
    return jnp.concatenate([ar * tr - ai * ti, ar * ti + ai * tr], axis=1)


def _s5_kernel(tm, x_ref, mod_ref, g_ref, bm_ref, cm_ref, e_ref, f_ref, a_ref, tri_ref, dskip_ref,
               wglu_ref, o_ref, u_sc, y_sc, s_sc):
    t = pl.program_id(1)
    x = x_ref[0]
    shift, scale, gate = _mod_rows(mod_ref, 1)
    u_sc[...] = _adaln(x, g_ref[...], shift, scale)
    L = S5_CHUNK
    n_blocks = bm_ref.shape[0]

    @pl.when(t == 0)
    def _():
        s_sc[...] = jnp.zeros_like(s_sc)

    tri = tri_ref[...]
    chunks = [slice(c * L, (c + 1) * L) for c in range(tm // L)]
    for kb in range(n_blocks):
        cols = slice(kb * LANES, (kb + 1) * LANES)
        bu = _dot(u_sc[:, cols].astype(BF16), bm_ref[kb]).astype(BF16)
        sums = [_dot(tri, _cmul(bu[rows], e_ref[kb])) for rows in chunks]
        carried = [s_sc[kb:kb + 1, :]]
        for p in sums:
            carried.append(_cmul(p[L - 1:L, :] + carried[-1], a_ref[kb]))
        s_sc[kb:kb + 1, :] = carried[-1]
        xs = jnp.concatenate([_cmul((p + s).astype(BF16), f_ref[kb]) for p, s in zip(sums, carried)],
                             axis=0)
        y_sc[:, cols] = _dot(xs, cm_ref[kb])
    y = y_sc[...] + dskip_ref[...] * u_sc[...]
    g = jax.nn.gelu(y, approximate=True)
    out = g * jax.nn.sigmoid(_dot(g.astype(BF16), wglu_ref[...]))
    o_ref[0] = x + (1.0 + gate) * out


def _s5_tables(lam_re, lam_im, log_dt, b_re, b_im, c_re, c_im):
    g_, n_ = lam_re.shape
    i_ = b_re.shape[2]
    gl = LANES // i_
    nb = g_ // gl
    dt = jnp.exp(log_dt)[:, None]
    ar, ai = lam_re, lam_im
    mag = jnp.exp(ar * dt)
    lb_re, lb_im = mag * jnp.cos(ai * dt), mag * jnp.sin(ai * dt)
    den = ar * ar + ai * ai
    nr, ni = lb_re - 1.0, lb_im
    k_re = (nr * ar + ni * ai) / den
    k_im = (ni * ar - nr * ai) / den
    bb_re = k_re[..., None] * b_re - k_im[..., None] * b_im
    bb_im = k_re[..., None] * b_im + k_im[..., None] * b_re
    eye = jnp.eye(gl, dtype=F32)
    bb = jnp.stack([bb_re, bb_im]).reshape(2, nb, gl, n_, i_)
    bm = jnp.einsum('gh,pkgni->kgiphn', eye, bb).reshape(nb, gl * i_, 2 * gl * n_)
    cc = jnp.stack([c_re, -c_im]).reshape(2, nb, gl, i_, n_)
    cm = jnp.einsum('gh,pkgin->kpgnhi', eye, cc).reshape(nb, 2 * gl * n_, gl * i_)

    def lbar_pow(p):
        e = jnp.exp(p[:, None, None] * (ar * dt)[None])
        th = p[:, None, None] * (ai * dt)[None]
        t = jnp.stack([e * jnp.cos(th), e * jnp.sin(th)], axis=1)
        return t.reshape(-1, 2, nb, gl, n_).transpose(2, 0, 1, 3, 4).reshape(nb, -1, 2 * gl * n_)

    mid = S5_CHUNK // 2
    j = jnp.arange(S5_CHUNK, dtype=F32) - mid
    return (bm.astype(BF16), cm.astype(BF16), lbar_pow(-j).astype(BF16), lbar_pow(j).astype(BF16),
            lbar_pow(jnp.full((1,), float(S5_CHUNK), F32)))


def _s5_layer(x, mod, gain, lam_re, lam_im, log_dt, b_re, b_im, c_re, c_im, d_skip, w_glu, tm=512):
    b, s, d = x.shape
    bm, cm, e_tab, f_tab, a_tab = _s5_tables(lam_re, lam_im, log_dt, b_re, b_im, c_re, c_im)
    nb, _, sw = bm.shape
    L = S5_CHUNK
    tri = (jnp.arange(L)[:, None] >= jnp.arange(L)[None, :]).astype(BF16)
    x_spec, mod_spec = _tile_specs(tm, d)
    return pl.pallas_call(
        functools.partial(_s5_kernel, tm),
        out_shape=jax.ShapeDtypeStruct(x.shape, F32),
        grid=(b, s // tm),
        in_specs=[x_spec, mod_spec, _const_spec((1, d)),
                  _const_spec((nb, LANES, sw)), _const_spec((nb, sw, LANES)),
                  _const_spec((nb, L, sw)), _const_spec((nb, L, sw)), _const_spec((nb, 1, sw)),
                  _const_spec((L, L)), _const_spec((1, d)), _const_spec((d, d))],
        out_specs=x_spec,
        scratch_shapes=[pltpu.VMEM((tm, d), F32), pltpu.VMEM((tm, d), F32), pltpu.VMEM((nb, sw), F32)],
        compiler_params=_params(2),
    )(x, mod, gain.reshape(1, d), bm, cm, e_tab, f_tab, a_tab, tri, d_skip.reshape(1, d),
      w_glu.astype(BF16))


def _conv_kernel(tm, x_ref, mod_ref, g_ref, win_ref, cw_ref, wout_ref, o_ref, cz_sc):
    t = pl.program_id(1)
    x = x_ref[0]
    d = x.shape[1]
    shift, scale, gate = _mod_rows(mod_ref, 1)
    h = _adaln(x, g_ref[...], shift, scale).astype(BF16)

    @pl.when(t == 0)
    def _():
        cz_sc[0:CONV_HALO, :] = jnp.zeros((CONV_HALO, d), F32)

    cz_sc[CONV_HALO:, :] = _dot(h, win_ref[:, d:2 * d]) * _dot(h, win_ref[:, 2 * d:3 * d])
    cw = cw_ref[...]
    conv = (cw[0:1] * cz_sc[CONV_HALO - 2:CONV_HALO - 2 + tm, :]
            + cw[1:2] * cz_sc[CONV_HALO - 1:CONV_HALO - 1 + tm, :]
            + cw[2:3] * cz_sc[CONV_HALO:, :])
    cz_sc[0:CONV_HALO, :] = cz_sc[tm:tm + CONV_HALO, :]
    gated = (_dot(h, win_ref[:, 0:d]) * conv).astype(BF16)
    o_ref[0] = x + (1.0 + gate) * _dot(gated, wout_ref[...])


def _conv_layer(x, mod, gain, w_in, conv_w, w_out, tm=512):
    b, s, d = x.shape
    kw = conv_w.shape[0]
    x_spec, mod_spec = _tile_specs(tm, d)
    return pl.pallas_call(
        functools.partial(_conv_kernel, tm),
        out_shape=jax.ShapeDtypeStruct(x.shape, F32),
        grid=(b, s // tm),
        in_specs=[x_spec, mod_spec, _const_spec((1, d)),
                  _const_spec((d, 3 * d)), _const_spec((kw, d)), _const_spec((d, d))],
        out_specs=x_spec,
        scratch_shapes=[pltpu.VMEM((tm + CONV_HALO, d), F32)],
        compiler_params=_params(2),
    )(x, mod, gain.reshape(1, d), w_in.astype(BF16), conv_w.reshape(kw, d), w_out.astype(BF16))


def kernel(x, c, ada_w, ada_b, norm_g, ffn_w_in, ffn_w_out, pool_w, pool_scale, fox_w_in, fox_b_f, fox_q_gain, fox_k_gain, fox_w_o, s5_lam_re, s5_lam_im, s5_log_dt, s5_b_re, s5_b_im, s5_c_re, s5_c_im, s5_d, s5_w_glu, conv_w_in, conv_w, conv_w_out):
    b, s, d = x.shape
    depth = ada_w.shape[0]
    mod_all = _modulation(c, ada_w, ada_b).reshape(depth, b, N_ADA, d)
    n_mixers = 4
    w_in_all, w_out_all = ffn_w_in, ffn_w_out
    for i in range(depth):
        mod = mod_all[i]
        x = _ffn(x, mod, norm_g[i, 0], w_in_all, w_out_all, i, 0, sub=0)
        m, r = i % n_mixers, i // n_mixers
        if m == 0:
            x = _pool_layer(x, mod, norm_g[i, 1], pool_w[r], pool_scale[r])
        elif m == 1:
            qt, k, vt, qat, ka = _fox_project(x, mod, norm_g[i, 1], fox_w_in[r], fox_b_f[r],
                                              fox_q_gain[r], fox_k_gain[r])
            o = _fox_attend(qt, qat, k, ka, vt)
            x = _proj_out(o, x, mod, fox_w_o[r])
        elif m == 2:
            x = _s5_layer(x, mod, norm_g[i, 1], s5_lam_re[r], s5_lam_im[r], s5_log_dt[r],
                          s5_b_re[r], s5_b_im[r], s5_c_re[r], s5_c_im[r], s5_d[r], s5_w_glu[r])
        else:
            x = _conv_layer(x, mod, norm_g[i, 1], conv_w_in[r], conv_w[r], conv_w_out[r])
        x = _ffn(x, mod, norm_g[i, 2], w_in_all, w_out_all, i, 1, sub=2)
    return x
```

```python
import functools
import math

import jax
import jax.numpy as jnp
from jax import lax
from jax.experimental import pallas as pl
from jax.experimental.pallas import tpu as pltpu

F32 = jnp.float32
BF16 = jnp.bfloat16

NORM_EPS = 1e-6
N_SUBLAYERS = 3
N_ADA = 3 * N_SUBLAYERS
POOL_WINDOWS = (2, 4, 8, 16)
POOL_HALO = 32
FOX_HEADS = 16
FOX_HEAD_DIM = 64
S5_GROUP = 16
S5_STATE = 64
S5_CHUNK = 32
CONV_HALO = 8
LANES = 128
NEG_BIG = -1e30
LOG2E = math.log2(math.e)
AUG_LANES = 8
V_ROWS = FOX_HEAD_DIM + 16
FOX_TILE = 512

VMEM_LIMIT = 56 * 1024 * 1024


def _params(n_grid, vmem=VMEM_LIMIT):
    return pltpu.CompilerParams(dimension_semantics=("arbitrary",) * n_grid,
                                vmem_limit_bytes=vmem)


def _dot(a, b):
    return jnp.dot(a, b, preferred_element_type=F32)


def _adaln(x, gain, shift, scale):
    ms = jnp.mean(x * x, axis=-1, keepdims=True)
    y = x * lax.rsqrt(ms + NORM_EPS) * gain
    return y * (1.0 + scale) + shift


def _mod_rows(mod_ref, sub):
    m = mod_ref[0]
    return m[3 * sub:3 * sub + 1], m[3 * sub + 1:3 * sub + 2], m[3 * sub + 2:3 * sub + 3]


def _const_spec(shape):
    nd = len(shape)
    return pl.BlockSpec(shape, lambda *_: (0,) * nd, pipeline_mode=pl.Buffered(1))


def _tile_specs(tm, d):
    x_spec = pl.BlockSpec((1, tm, d), lambda b, t: (b, t, 0))
    mod_spec = pl.BlockSpec((1, N_ADA, d), lambda b, t: (b, 0, 0))
    return x_spec, mod_spec


def _mod_kernel(c_ref, w_ref, b_ref, o_ref):
    c = c_ref[...]
    cond = (c * jax.nn.sigmoid(c)).astype(BF16)
    o_ref[0] = _dot(cond, w_ref[0].astype(BF16)) + b_ref[0]


def _modulation(c, ada_w, ada_b):
    depth, d, n = ada_w.shape
    b = c.shape[0]
    tn = 1024
    return pl.pallas_call(
        _mod_kernel,
        out_shape=jax.ShapeDtypeStruct((depth, b, n), F32),
        grid=(depth, n // tn),
        in_specs=[pl.BlockSpec((b, d), lambda l, j: (0, 0)),
                  pl.BlockSpec((1, d, tn), lambda l, j: (l, 0, j)),
                  pl.BlockSpec((1, 1, tn), lambda l, j: (l, 0, j))],
        out_specs=pl.BlockSpec((1, b, tn), lambda l, j: (l, 0, j)),
        compiler_params=_params(2),
    )(c, ada_w, ada_b.reshape(depth, 1, n))


def _ffn_kernel(sub, tf, x_ref, mod_ref, g_ref, win_ref, wout_ref, o_ref, h_sc, a_sc):
    x = x_ref[0]
    shift, scale, gate = _mod_rows(mod_ref, sub)
    h_sc[...] = _adaln(x, g_ref[...], shift, scale).astype(BF16)
    dff = a_sc.shape[1]
    for j in range(dff // tf):
        h = h_sc[...]
        g = _dot(h, win_ref[:, j * tf:(j + 1) * tf].astype(BF16))
        u = _dot(h, win_ref[:, dff + j * tf:dff + (j + 1) * tf].astype(BF16))
        a_sc[:, j * tf:(j + 1) * tf] = (g * jax.nn.sigmoid(g) * u).astype(BF16)
    y = _dot(a_sc[...], wout_ref[...].astype(BF16))
    o_ref[0] = x + (0.5 * (1.0 + gate)) * y


def _ffn(x, mod, gain, w_in_all, w_out_all, layer, half, sub, tm=512, tf=256):
    b, s, d = x.shape
    dff = w_out_all.shape[2]
    x_spec, mod_spec = _tile_specs(tm, d)

    def pick(rows, cols):
        return pl.BlockSpec((None, None, rows, cols), lambda b_, t: (layer, half, 0, 0),
                            pipeline_mode=pl.Buffered(1))

    return pl.pallas_call(
        functools.partial(_ffn_kernel, sub, tf),
        out_shape=jax.ShapeDtypeStruct(x.shape, F32),
        grid=(b, s // tm),
        in_specs=[x_spec, mod_spec, _const_spec((1, d)), pick(d, 2 * dff), pick(dff, d)],
        out_specs=x_spec,
        scratch_shapes=[pltpu.VMEM((tm, d), BF16), pltpu.VMEM((tm, dff), BF16)],
        compiler_params=_params(2),
    )(x, mod, gain.reshape(1, d), w_in_all, w_out_all)


def _pool_kernel(tm, x_ref, mod_ref, g_ref, w_ref, sc_ref, o_ref, h_sc, p_sc):
    t = pl.program_id(1)
    x = x_ref[0]
    d = x.shape[1]
    cg = d // len(POOL_WINDOWS)
    n = tm + POOL_HALO
    shift, scale, gate = _mod_rows(mod_ref, 1)

    @pl.when(t == 0)
    def _():
        h_sc[0:POOL_HALO, :] = jnp.zeros((POOL_HALO, d), F32)

    h_sc[POOL_HALO:, :] = _adaln(x, g_ref[...], shift, scale)
    pos = t * tm + lax.broadcasted_iota(jnp.int32, (tm, 1), 0)
    ys = []
    for gi, w in enumerate(POOL_WINDOWS):
        sl = slice(gi * cg, (gi + 1) * cg)
        cur = h_sc[POOL_HALO:, sl]

        def rows_of(buf):
            if buf is None:
                return lambda lo, hi: h_sc[lo:hi, sl]
            return lambda lo, hi: p_sc[buf, lo:hi, :]

        src, span, level = rows_of(None), 1, 0
        while 2 * span < w:
            level += 1
            lo = 8 * level
            p_sc[level % 2, lo:n, :] = src(lo, n) + src(lo - span, n - span)
            src, span = rows_of(level % 2), 2 * span
        acc = src(POOL_HALO, n) + src(POOL_HALO - span, n - span)
        cnt = jnp.minimum(pos + 1, w).astype(F32)
        pooled = acc / cnt - cur
        ys.append(_dot(pooled.astype(BF16), w_ref[gi]))
    y = jnp.concatenate(ys, axis=1) * sc_ref[...]
    h_sc[0:POOL_HALO, :] = h_sc[tm:tm + POOL_HALO, :]
    o_ref[0] = x + (1.0 + gate) * y


def _pool_layer(x, mod, gain, w_grp, scale, tm=512):
    b, s, d = x.shape
    ng, cg, _ = w_grp.shape
    x_spec, mod_spec = _tile_specs(tm, d)
    return pl.pallas_call(
        functools.partial(_pool_kernel, tm),
        out_shape=jax.ShapeDtypeStruct(x.shape, F32),
        grid=(b, s // tm),
        in_specs=[x_spec, mod_spec, _const_spec((1, d)),
                  _const_spec((ng, cg, cg)), _const_spec((1, d))],
        out_specs=x_spec,
        scratch_shapes=[pltpu.VMEM((tm + POOL_HALO, d), F32), pltpu.VMEM((2, tm + POOL_HALO, cg), F32)],
        compiler_params=_params(2),
    )(x, mod, gain.reshape(1, d), w_grp.astype(BF16), scale.reshape(1, d))


def _split3(v):
    p1 = v.astype(BF16)
    r1 = v - p1.astype(F32)
    p2 = r1.astype(BF16)
    p3 = (r1 - p2.astype(F32)).astype(BF16)
    return p1, p2, p3


def _foxproj_kernel(x_ref, mod_ref, g_ref, wq_ref, wk_ref, wv_ref, wf_ref, bf_ref, qg_ref, kg_ref,
                    tri_ref, place_ref, ones_ref,
                    q_ref, k_ref, v_ref, qa_ref, ka_ref, carry_sc):
    t = pl.program_id(1)
    x = x_ref[0]
    shift, scale, _ = _mod_rows(mod_ref, 1)
    h = _adaln(x, g_ref[...], shift, scale).astype(BF16)
    nt_dims = (((1,), (1,)), ((), ()))

    def head_normed(xt, gain_ref, hd):
        rows = slice(hd * FOX_HEAD_DIM, (hd + 1) * FOX_HEAD_DIM)
        xh = xt[rows, :]
        r = lax.rsqrt(jnp.mean(xh * xh, axis=0, keepdims=True) + NORM_EPS)
        return (xh * r * gain_ref[rows, :]).astype(BF16)

    qt = lax.dot_general(wq_ref[...], h, nt_dims, preferred_element_type=F32)
    kt = lax.dot_general(wk_ref[...], h, nt_dims, preferred_element_type=F32)
    for pair in range(FOX_HEADS // 2):
        lanes = slice(pair * LANES, (pair + 1) * LANES)
        q_ref[0, 0, lanes, :] = jnp.concatenate(
            [head_normed(qt, qg_ref, 2 * pair), head_normed(qt, qg_ref, 2 * pair + 1)], axis=0)
        k_ref[0, :, lanes] = jnp.concatenate(
            [head_normed(kt, kg_ref, 2 * pair), head_normed(kt, kg_ref, 2 * pair + 1)], axis=0).T
    v_ref[0, 0] = lax.dot_general(wv_ref[...], h, nt_dims, preferred_element_type=F32).astype(BF16)
    fl = _dot(h, wf_ref[...]) + bf_ref[...]
    logf = jnp.minimum(fl, 0.0) - jnp.log(1.0 + jnp.exp(-jnp.abs(fl)))

    @pl.when(t == 0)
    def _():
        carry_sc[...] = jnp.zeros_like(carry_sc)

    tri = tri_ref[...]
    cf = sum(_dot(tri, p) for p in _split3(logf)) + carry_sc[0:1, :]
    tm = cf.shape[0]
    carry_sc[0:1, :] = cf[tm - 1:tm, :]
    pieces = jnp.concatenate(_split3(cf * LOG2E), axis=1)
    aug = _dot(pieces, place_ref[...]) + ones_ref[...]
    qa_ref[0, 0] = aug[:, :LANES].T.astype(BF16)
    ka_ref[0] = aug[:, LANES:].astype(BF16)


def _fox_project(x, mod, gain, w_in, b_f, q_gain, k_gain, tm=FOX_TILE):
    b, s, d = x.shape
    x_spec, mod_spec = _tile_specs(tm, d)
    w = w_in.astype(BF16)
    wf = jnp.zeros((d, LANES), BF16).at[:, :FOX_HEADS].set(w[:, 3 * d:])
    bf = jnp.zeros((1, LANES), F32).at[0, :FOX_HEADS].set(b_f)
    qg = jnp.tile(q_gain, FOX_HEADS).reshape(d, 1) * (FOX_HEAD_DIM ** -0.5 * LOG2E)
    kg = jnp.tile(k_gain, FOX_HEADS).reshape(d, 1)
    tri = (jnp.arange(tm)[:, None] >= jnp.arange(tm)[None, :]).astype(BF16)
    hh, ii = jnp.meshgrid(jnp.arange(FOX_HEADS), jnp.arange(3), indexing="ij")
    place = jnp.zeros((3 * LANES, 2 * LANES), F32)
    place = place.at[ii * LANES + hh, AUG_LANES * hh + 3 + ii].set(1.0)
    place = place.at[ii * LANES + hh, LANES + AUG_LANES * hh + ii].set(-1.0)
    ones = jnp.zeros((1, 2 * LANES), F32)
    ones = ones.at[0, AUG_LANES * hh + ii].set(1.0).at[0, LANES + AUG_LANES * hh + 3 + ii].set(1.0)
    act = jax.ShapeDtypeStruct((b, s, d), BF16)
    aug = jax.ShapeDtypeStruct((b, s, LANES), BF16)
    act_t = jax.ShapeDtypeStruct((b, s // tm, d, tm), BF16)
    aug_t = jax.ShapeDtypeStruct((b, s // tm, LANES, tm), BF16)
    token_major = lambda width: pl.BlockSpec((1, tm, width), lambda b_, t: (b_, t, 0))
    channel_major = lambda rows: pl.BlockSpec((1, 1, rows, tm), lambda b_, t: (b_, t, 0, 0))
    return pl.pallas_call(
        _foxproj_kernel,
        out_shape=(act_t, act, act_t, aug_t, aug),
        grid=(b, s // tm),
        in_specs=[x_spec, mod_spec, _const_spec((1, d)),
                  _const_spec((d, d)), _const_spec((d, d)), _const_spec((d, d)),
                  _const_spec((d, LANES)), _const_spec((1, LANES)),
                  _const_spec((d, 1)), _const_spec((d, 1)), _const_spec((tm, tm)),
                  _const_spec((3 * LANES, 2 * LANES)), _const_spec((1, 2 * LANES))],
        out_specs=(channel_major(d), token_major(d), channel_major(d),
                   channel_major(LANES), token_major(LANES)),
        scratch_shapes=[pltpu.VMEM((8, LANES), F32)],
        compiler_params=_params(2),
    )(x, mod, gain.reshape(1, d), w[:, :d].T, w[:, d:2 * d].T, w[:, 2 * d:3 * d].T, wf, bf, qg, kg,
      tri, place.astype(BF16), ones)


def _foxattn_kernel(TK, TQ, q_ref, qa_ref, qn_ref, qan_ref, k_ref, ka_ref, vt_ref, o_ref,
                    sa_sc, sb_sc, qt_sc, qtn_sc, m_sc, acc_sc, diff_sc):
    first_step = (pl.program_id(0) == 0) & (pl.program_id(1) == 0) & (pl.program_id(2) == 0)
    pair = pl.program_id(1)
    qi = pl.program_id(2)
    chan = lax.broadcasted_iota(jnp.int32, (LANES, 1), 0)
    zero = jnp.zeros((LANES, TK), BF16)
    for hd in range(2):
        own = (chan < FOX_HEAD_DIM) == (hd == 0)
        bias_lo = AUG_LANES * (2 * pair + hd)
        own_bias = (chan >= bias_lo) & (chan < bias_lo + AUG_LANES)
        for src, src_a, dst in ((q_ref, qa_ref, qt_sc), (qn_ref, qan_ref, qtn_sc)):
            for i in range(TQ // TK):
                cols = slice(i * TK, (i + 1) * TK)
                dst[hd, 0:LANES, cols] = jnp.where(own, src[0, i], zero)
                dst[hd, LANES:2 * LANES, cols] = jnp.where(own_bias, src_a[0, i], zero)
        m_sc[hd] = jnp.full((8, TQ), NEG_BIG, F32)
        acc_sc[hd] = jnp.zeros((V_ROWS, TQ), F32)

    @pl.when(first_step)
    def _():
        diff_sc[...] = (lax.broadcasted_iota(jnp.int32, (TK, TQ), 0)
                        - lax.broadcasted_iota(jnp.int32, (TK, TQ), 1))

    every = slice(0, TQ)
    early, late = slice(0, TK), slice(TK, TQ)

    def scores(j, s_sc, qs=every, queries=qt_sc):
        rows = pl.ds(pl.multiple_of(j * TK, TK), TK)
        kj = jnp.concatenate([k_ref[0, rows, :], ka_ref[0, rows, :]], axis=1)
        for hd in range(2):
            s_sc[hd, :, qs] = _dot(kj, queries[hd, :, qs])

    def absorb(j, s_sc, masked, qs=every):
        vt = vt_ref[0, j]
        ones = jnp.ones((V_ROWS - FOX_HEAD_DIM, TK), BF16)
        if masked:
            visible = diff_sc[:, qs] <= qi * TQ - j * TK
        for hd in range(2):
            s = s_sc[hd, :, qs]
            if masked:
                s = jnp.where(visible, s, NEG_BIG)
            m_prev = m_sc[hd, :, qs]
            m_new = jnp.maximum(m_prev, jnp.max(s, axis=0, keepdims=True))
            p = jnp.exp2(s - m_new[0:1, :]).astype(BF16)
            v_aug = jnp.concatenate([vt[hd * FOX_HEAD_DIM:(hd + 1) * FOX_HEAD_DIM, :], ones], axis=0)
            alpha = jnp.exp2(m_prev - m_new)[0:1, :]
            acc_sc[hd, :, qs] = alpha * acc_sc[hd, :, qs] + _dot(v_aug, p)
            m_sc[hd, :, qs] = m_new

    @pl.when(qi == 0)
    def _():
        scores(0, sa_sc)

    def two_tiles(jj):
        j = 2 * jj
        scores(j + 1, sb_sc)
        absorb(j, sa_sc, False)
        scores(j + 2, sa_sc)
        absorb(j + 1, sb_sc, False)

    def four_tiles(jjjj, carry):
        two_tiles(2 * jjjj)
        two_tiles(2 * jjjj + 1)
        return carry

    lax.fori_loop(0, qi // 2, four_tiles, 0)

    @pl.when(qi % 2 == 1)
    def _():
        two_tiles(qi - 1)
    scores(2 * qi + 1, sb_sc, late)
    absorb(2 * qi, sa_sc, True, early)
    scores(0, sa_sc, early, qtn_sc)
    absorb(2 * qi, sa_sc, False, late)
    scores(0, sa_sc, late, qtn_sc)
    absorb(2 * qi + 1, sb_sc, True, late)

    out_t = jnp.concatenate(
        [acc_sc[hd, 0:FOX_HEAD_DIM, :] / acc_sc[hd, FOX_HEAD_DIM:FOX_HEAD_DIM + 1, :] for hd in range(2)],
        axis=0)
    o_ref[0] = out_t.T.astype(o_ref.dtype)


def _fox_attend(qt, qat, k, ka, vt, tk=FOX_TILE):
    b, s, d = k.shape
    pairs = d // LANES
    tq = 2 * tk
    nq = s // tq
    assert qt.shape == vt.shape == (b, s // tk, d, tk)
    return pl.pallas_call(
        functools.partial(_foxattn_kernel, tk, tq),
        out_shape=jax.ShapeDtypeStruct((b, s, d), BF16),
        grid=(b, pairs, s // tq),
        in_specs=[pl.BlockSpec((1, tq // tk, LANES, tk), lambda b_, p, i: (b_, i, p, 0)),
                  pl.BlockSpec((1, tq // tk, LANES, tk), lambda b_, p, i: (b_, i, 0, 0)),
                  pl.BlockSpec((1, tq // tk, LANES, tk), lambda b_, p, i: (b_, jnp.minimum(i + 1, nq - 1), p, 0)),
                  pl.BlockSpec((1, tq // tk, LANES, tk), lambda b_, p, i: (b_, jnp.minimum(i + 1, nq - 1), 0, 0)),
                  pl.BlockSpec((1, s, LANES), lambda b_, p, i: (b_, 0, p)),
                  pl.BlockSpec((1, s, LANES), lambda b_, p, i: (b_, 0, 0)),
                  pl.BlockSpec((1, s // tk, LANES, tk), lambda b_, p, i: (b_, 0, p, 0))],
        out_specs=pl.BlockSpec((1, tq, LANES), lambda b_, p, i: (b_, i, p)),
        scratch_shapes=[pltpu.VMEM((2, tk, tq), F32), pltpu.VMEM((2, tk, tq), F32),
                        pltpu.VMEM((2, 2 * LANES, tq), BF16), pltpu.VMEM((2, 2 * LANES, tq), BF16),
                        pltpu.VMEM((2, 8, tq), F32), pltpu.VMEM((2, V_ROWS, tq), F32),
                        pltpu.VMEM((tk, tq), jnp.int32)],
        compiler_params=_params(3),
    )(qt, qat, qt, qat, k, ka, vt)


def _proj_out_kernel(a_ref, x_ref, mod_ref, w_ref, o_ref):
    _, _, gate = _mod_rows(mod_ref, 1)
    o_ref[0] = x_ref[0] + (1.0 + gate) * _dot(a_ref[0], w_ref[...])


def _proj_out(a, x, mod, w, tm=512):
    b, s, d = x.shape
    x_spec, mod_spec = _tile_specs(tm, d)
    return pl.pallas_call(
        _proj_out_kernel,
        out_shape=jax.ShapeDtypeStruct(x.shape, F32),
        grid=(b, s // tm),
        in_specs=[x_spec, x_spec, mod_spec, _const_spec((d, d))],
        out_specs=x_spec,
        compiler_params=_params(2),
    )(a, x, mod, w.astype(BF16))


def _cmul(a, t):
    w = a.shape[1] // 2
    ar, ai, tr, ti = a[:, :w], a[:, w:], t[:, :w], t[:, w:]
    return jnp.concatenate([ar * tr - ai * ti, ar * ti + ai * tr], axis=1)


def _s5_kernel(tm, x_ref, mod_ref, g_ref, bm_ref, cm_ref, e_ref, f_ref, a_ref, tri_ref, dskip_ref,
               wglu_ref, o_ref, u_sc, y_sc, s_sc):
    t = pl.program_id(1)
    x = x_ref[0]
    shift, scale, gate = _mod_rows(mod_ref, 1)
    u_sc[...] = _adaln(x, g_ref[...], shift, scale)
    L = S5_CHUNK
    n_blocks = bm_ref.shape[0]

    @pl.when(t == 0)
    def _():
        s_sc[...] = jnp.zeros_like(s_sc)

    tri = tri_ref[...]
    chunks = [slice(c * L, (c + 1) * L) for c in range(tm // L)]
    for kb in range(n_blocks):
        cols = slice(kb * LANES, (kb + 1) * LANES)
        bu = _dot(u_sc[:, cols].astype(BF16), bm_ref[kb]).astype(BF16)
        sums = [_dot(tri, _cmul(bu[rows], e_ref[kb])) for rows in chunks]
        carried = [s_sc[kb:kb + 1, :]]
        for p in sums:
            carried.append(_cmul(p[L - 1:L, :] + carried[-1], a_ref[kb]))
        s_sc[kb:kb + 1, :] = carried[-1]
        xs = jnp.concatenate([_cmul((p + s).astype(BF16), f_ref[kb]) for p, s in zip(sums, carried)],
                             axis=0)
        y_sc[:, cols] = _dot(xs, cm_ref[kb])
    y = y_sc[...] + dskip_ref[...] * u_sc[...]
    g = jax.nn.gelu(y, approximate=True)
    out = g * jax.nn.sigmoid(_dot(g.astype(BF16), wglu_ref[...]))
    o_ref[0] = x + (1.0 + gate) * out


def _s5_tables(lam_re, lam_im, log_dt, b_re, b_im, c_re, c_im):
    g_, n_ = lam_re.shape
    i_ = b_re.shape[2]
    gl = LANES // i_
    nb = g_ // gl
    dt = jnp.exp(log_dt)[:, None]
    ar, ai = lam_re, lam_im
    mag = jnp.exp(ar * dt)
    lb_re, lb_im = mag * jnp.cos(ai * dt), mag * jnp.sin(ai * dt)
    den = ar * ar + ai * ai
    nr, ni = lb_re - 1.0, lb_im
    k_re = (nr * ar + ni * ai) / den
    k_im = (ni * ar - nr * ai) / den
    bb_re = k_re[..., None] * b_re - k_im[..., None] * b_im
    bb_im = k_re[..., None] * b_im + k_im[..., None] * b_re
    eye = jnp.eye(gl, dtype=F32)
    bb = jnp.stack([bb_re, bb_im]).reshape(2, nb, gl, n_, i_)
    bm = jnp.einsum('gh,pkgni->kgiphn', eye, bb).reshape(nb, gl * i_, 2 * gl * n_)
    cc = jnp.stack([c_re, -c_im]).reshape(2, nb, gl, i_, n_)
    cm = jnp.einsum('gh,pkgin->kpgnhi', eye, cc).reshape(nb, 2 * gl * n_, gl * i_)

    def lbar_pow(p):
        e = jnp.exp(p[:, None, None] * (ar * dt)[None])
        th = p[:, None, None] * (ai * dt)[None]
        t = jnp.stack([e * jnp.cos(th), e * jnp.sin(th)], axis=1)
        return t.reshape(-1, 2, nb, gl, n_).transpose(2, 0, 1, 3, 4).reshape(nb, -1, 2 * gl * n_)

    mid = S5_CHUNK // 2
    j = jnp.arange(S5_CHUNK, dtype=F32) - mid
    return (bm.astype(BF16), cm.astype(BF16), lbar_pow(-j).astype(BF16), lbar_pow(j).astype(BF16),
            lbar_pow(jnp.full((1,), float(S5_CHUNK), F32)))


def _s5_layer(x, mod, gain, lam_re, lam_im, log_dt, b_re, b_im, c_re, c_im, d_skip, w_glu, tm=512):
    b, s, d = x.shape
    bm, cm, e_tab, f_tab, a_tab = _s5_tables(lam_re, lam_im, log_dt, b_re, b_im, c_re, c_im)
    nb, _, sw = bm.shape
    L = S5_CHUNK
    tri = (jnp.arange(L)[:, None] >= jnp.arange(L)[None, :]).astype(BF16)
    x_spec, mod_spec = _tile_specs(tm, d)
    return pl.pallas_call(
        functools.partial(_s5_kernel, tm),
        out_shape=jax.ShapeDtypeStruct(x.shape, F32),
        grid=(b, s // tm),
        in_specs=[x_spec, mod_spec, _const_spec((1, d)),
                  _const_spec((nb, LANES, sw)), _const_spec((nb, sw, LANES)),
                  _const_spec((nb, L, sw)), _const_spec((nb, L, sw)), _const_spec((nb, 1, sw)),
                  _const_spec((L, L)), _const_spec((1, d)), _const_spec((d, d))],
        out_specs=x_spec,
        scratch_shapes=[pltpu.VMEM((tm, d), F32), pltpu.VMEM((tm, d), F32), pltpu.VMEM((nb, sw), F32)],
        compiler_params=_params(2),
    )(x, mod, gain.reshape(1, d), bm, cm, e_tab, f_tab, a_tab, tri, d_skip.reshape(1, d),
      w_glu.astype(BF16))


def _conv_kernel(tm, x_ref, mod_ref, g_ref, win_ref, cw_ref, wout_ref, o_ref, cz_sc):
    t = pl.program_id(1)
    x = x_ref[0]
    d = x.shape[1]
    shift, scale, gate = _mod_rows(mod_ref, 1)
    h = _adaln(x, g_ref[...], shift, scale).astype(BF16)

    @pl.when(t == 0)
    def _():
        cz_sc[0:CONV_HALO, :] = jnp.zeros((CONV_HALO, d), F32)

    cz_sc[CONV_HALO:, :] = _dot(h, win_ref[:, d:2 * d]) * _dot(h, win_ref[:, 2 * d:3 * d])
    cw = cw_ref[...]
    conv = (cw[0:1] * cz_sc[CONV_HALO - 2:CONV_HALO - 2 + tm, :]
            + cw[1:2] * cz_sc[CONV_HALO - 1:CONV_HALO - 1 + tm, :]
            + cw[2:3] * cz_sc[CONV_HALO:, :])
    cz_sc[0:CONV_HALO, :] = cz_sc[tm:tm + CONV_HALO, :]
    gated = (_dot(h, win_ref[:, 0:d]) * conv).astype(BF16)
    o_ref[0] = x + (1.0 + gate) * _dot(gated, wout_ref[...])


def _conv_layer(x, mod, gain, w_in, conv_w, w_out, tm=512):
    b, s, d = x.shape
    kw = conv_w.shape[0]
    x_spec, mod_spec = _tile_specs(tm, d)
    return pl.pallas_call(
        functools.partial(_conv_kernel, tm),
        out_shape=jax.ShapeDtypeStruct(x.shape, F32),
        grid=(b, s // tm),
        in_specs=[x_spec, mod_spec, _const_spec((1, d)),
                  _const_spec((d, 3 * d)), _const_spec((kw, d)), _const_spec((d, d))],
        out_specs=x_spec,
        scratch_shapes=[pltpu.VMEM((tm + CONV_HALO, d), F32)],
        compiler_params=_params(2),
    )(x, mod, gain.reshape(1, d), w_in.astype(BF16), conv_w.reshape(kw, d), w_out.astype(BF16))


def kernel(x, c, ada_w, ada_b, norm_g, ffn_w_in, ffn_w_out, pool_w, pool_scale, fox_w_in, fox_b_f, fox_q_gain, fox_k_gain, fox_w_o, s5_lam_re, s5_lam_im, s5_log_dt, s5_b_re, s5_b_im, s5_c_re, s5_c_im, s5_d, s5_w_glu, conv_w_in, conv_w, conv_w_out):
    b, s, d = x.shape
    depth = ada_w.shape[0]
    mod_all = _modulation(c, ada_w, ada_b).reshape(depth, b, N_ADA, d)
    n_mixers = 4
    w_in_all, w_out_all = ffn_w_in, ffn_w_out
    for i in range(depth):
        mod = mod_all[i]
        x = _ffn(x, mod, norm_g[i, 0], w_in_all, w_out_all, i, 0, sub=0)
        m, r = i % n_mixers, i // n_mixers
        if m == 0:
            x = _pool_layer(x, mod, norm_g[i, 1], pool_w[r], pool_scale[r])
        elif m == 1:
            qt, k, vt, qat, ka = _fox_project(x, mod, norm_g[i, 1], fox_w_in[r], fox_b_f[r],
                                              fox_q_gain[r], fox_k_gain[r])
            o = _fox_attend(qt, qat, k, ka, vt)
            x = _proj_out(o, x, mod, fox_w_o[r])
        elif m == 2:
            x = _s5_layer(x, mod, norm_g[i, 1], s5_lam_re[r], s5_lam_im[r], s5_log_dt[r],
                          s5_b_re[r], s5_b_im[r], s5_c_re[r], s5_c_im[r], s5_d[r], s5_w_glu[r])
        else:
            x = _conv_layer(x, mod, norm_g[i, 1], conv_w_in[r], conv_w[r], conv_w_out[r])
        x = _ffn(x, mod, norm_g[i, 2], w_in_all, w_out_all, i, 1, sub=2)
    return x
```

```python
import functools
import math

import jax
import jax.numpy as jnp
from jax import lax
from jax.experimental import pallas as pl
from jax.experimental.pallas import tpu as pltpu

F32 = jnp.float32
BF16 = jnp.bfloat16

NORM_EPS = 1e-6
N_SUBLAYERS = 3
N_ADA = 3 * N_SUBLAYERS
POOL_WINDOWS = (2, 4, 8, 16)
POOL_HALO = 32
FOX_HEADS = 16
FOX_HEAD_DIM = 64
S5_GROUP = 16
S5_STATE = 64
S5_CHUNK = 32
CONV_HALO = 8
LANES = 128
NEG_BIG = -1e30
LOG2E = math.log2(math.e)
AUG_LANES = 8
V_ROWS = FOX_HEAD_DIM + 16
FOX_TILE = 512

VMEM_LIMIT = 56 * 1024 * 1024


def _params(n_grid, vmem=VMEM_LIMIT):
    return pltpu.CompilerParams(dimension_semantics=("arbitrary",) * n_grid,
                                vmem_limit_bytes=vmem)


def _dot(a, b):
    return jnp.dot(a, b, preferred_element_type=F32)


def _adaln(x, gain, shift, scale):
    ms = jnp.mean(x * x, axis=-1, keepdims=True)
    y = x * lax.rsqrt(ms + NORM_EPS) * gain
    return y * (1.0 + scale) + shift


def _mod_rows(mod_ref, sub):
    m = mod_ref[0]
    return m[3 * sub:3 * sub + 1], m[3 * sub + 1:3 * sub + 2], m[3 * sub + 2:3 * sub + 3]


def _const_spec(shape):
    nd = len(shape)
    return pl.BlockSpec(shape, lambda *_: (0,) * nd, pipeline_mode=pl.Buffered(1))


def _tile_specs(tm, d):
    x_spec = pl.BlockSpec((1, tm, d), lambda b, t: (b, t, 0))
    mod_spec = pl.BlockSpec((1, N_ADA, d), lambda b, t: (b, 0, 0))
    return x_spec, mod_spec


def _mod_kernel(c_ref, w_ref, b_ref, o_ref):
    c = c_ref[...]
    cond = (c * jax.nn.sigmoid(c)).astype(BF16)
    o_ref[0] = _dot(cond, w_ref[0].astype(BF16)) + b_ref[0]


def _modulation(c, ada_w, ada_b):
    depth, d, n = ada_w.shape
    b = c.shape[0]
    tn = 1024
    return pl.pallas_call(
        _mod_kernel,
        out_shape=jax.ShapeDtypeStruct((depth, b, n), F32),
        grid=(depth, n // tn),
        in_specs=[pl.BlockSpec((b, d), lambda l, j: (0, 0)),
                  pl.BlockSpec((1, d, tn), lambda l, j: (l, 0, j)),
                  pl.BlockSpec((1, 1, tn), lambda l, j: (l, 0, j))],
        out_specs=pl.BlockSpec((1, b, tn), lambda l, j: (l, 0, j)),
        compiler_params=_params(2),
    )(c, ada_w, ada_b.reshape(depth, 1, n))


def _ffn_kernel(sub, tf, mixer_out, *refs):
    if mixer_out:
        x_ref, mod_ref, g_ref, win_ref, wout_ref, att_ref, wo_ref, o_ref, h_sc, a_sc = refs
        x = x_ref[0] + (1.0 + _mod_rows(mod_ref, 1)[2]) * _dot(att_ref[0], wo_ref[...])
    else:
        x_ref, mod_ref, g_ref, win_ref, wout_ref, o_ref, h_sc, a_sc = refs
        x = x_ref[0]
    shift, scale, gate = _mod_rows(mod_ref, sub)
    h_sc[...] = _adaln(x, g_ref[...], shift, scale).astype(BF16)
    dff = a_sc.shape[1]
    for j in range(dff // tf):
        h = h_sc[...]
        g = _dot(h, win_ref[:, j * tf:(j + 1) * tf].astype(BF16))
        u = _dot(h, win_ref[:, dff + j * tf:dff + (j + 1) * tf].astype(BF16))
        a_sc[:, j * tf:(j + 1) * tf] = (g * jax.nn.sigmoid(g) * u).astype(BF16)
    y = _dot(a_sc[...], wout_ref[...].astype(BF16))
    o_ref[0] = x + (0.5 * (1.0 + gate)) * y


def _ffn(x, mod, gain, w_in_all, w_out_all, layer, half, sub, mixer_out=None, tm=512, tf=256):
    b, s, d = x.shape
    dff = w_out_all.shape[2]
    x_spec, mod_spec = _tile_specs(tm, d)

    def pick(rows, cols):
        return pl.BlockSpec((None, None, rows, cols), lambda b_, t: (layer, half, 0, 0),
                            pipeline_mode=pl.Buffered(1))

    operands = [x, mod, gain.reshape(1, d), w_in_all, w_out_all]
    in_specs = [x_spec, mod_spec, _const_spec((1, d)), pick(d, 2 * dff), pick(dff, d)]
    if mixer_out is not None:
        operands += list(mixer_out)
        in_specs += [x_spec, _const_spec((d, d))]
    return pl.pallas_call(
        functools.partial(_ffn_kernel, sub, tf, mixer_out is not None),
        out_shape=jax.ShapeDtypeStruct(x.shape, F32),
        grid=(b, s // tm),
        in_specs=in_specs,
        out_specs=x_spec,
        scratch_shapes=[pltpu.VMEM((tm, d), BF16), pltpu.VMEM((tm, dff), BF16)],
        compiler_params=_params(2),
    )(*operands)


def _pool_kernel(tm, x_ref, mod_ref, g_ref, w_ref, sc_ref, o_ref, h_sc, p_sc):
    t = pl.program_id(1)
    x = x_ref[0]
    d = x.shape[1]
    cg = d // len(POOL_WINDOWS)
    n = tm + POOL_HALO
    shift, scale, gate = _mod_rows(mod_ref, 1)

    @pl.when(t == 0)
    def _():
        h_sc[0:POOL_HALO, :] = jnp.zeros((POOL_HALO, d), F32)

    h_sc[POOL_HALO:, :] = _adaln(x, g_ref[...], shift, scale)
    pos = t * tm + lax.broadcasted_iota(jnp.int32, (tm, 1), 0)
    ys = []
    for gi, w in enumerate(POOL_WINDOWS):
        sl = slice(gi * cg, (gi + 1) * cg)
        cur = h_sc[POOL_HALO:, sl]

        def rows_of(buf):
            if buf is None:
                return lambda lo, hi: h_sc[lo:hi, sl]
            return lambda lo, hi: p_sc[buf, lo:hi, :]

        src, span, level = rows_of(None), 1, 0
        while 2 * span < w:
            level += 1
            lo = 8 * level
            p_sc[level % 2, lo:n, :] = src(lo, n) + src(lo - span, n - span)
            src, span = rows_of(level % 2), 2 * span
        acc = src(POOL_HALO, n) + src(POOL_HALO - span, n - span)
        cnt = jnp.minimum(pos + 1, w).astype(F32)
        pooled = acc / cnt - cur
        ys.append(_dot(pooled.astype(BF16), w_ref[gi]))
    y = jnp.concatenate(ys, axis=1) * sc_ref[...]
    h_sc[0:POOL_HALO, :] = h_sc[tm:tm + POOL_HALO, :]
    o_ref[0] = x + (1.0 + gate) * y


def _pool_layer(x, mod, gain, w_grp, scale, tm=512):
    b, s, d = x.shape
    ng, cg, _ = w_grp.shape
    x_spec, mod_spec = _tile_specs(tm, d)
    return pl.pallas_call(
        functools.partial(_pool_kernel, tm),
        out_shape=jax.ShapeDtypeStruct(x.shape, F32),
        grid=(b, s // tm),
        in_specs=[x_spec, mod_spec, _const_spec((1, d)),
                  _const_spec((ng, cg, cg)), _const_spec((1, d))],
        out_specs=x_spec,
        scratch_shapes=[pltpu.VMEM((tm + POOL_HALO, d), F32), pltpu.VMEM((2, tm + POOL_HALO, cg), F32)],
        compiler_params=_params(2),
    )(x, mod, gain.reshape(1, d), w_grp.astype(BF16), scale.reshape(1, d))


def _split3(v):
    p1 = v.astype(BF16)
    r1 = v - p1.astype(F32)
    p2 = r1.astype(BF16)
    p3 = (r1 - p2.astype(F32)).astype(BF16)
    return p1, p2, p3


def _foxproj_kernel(x_ref, mod_ref, g_ref, wq_ref, wk_ref, wv_ref, wf_ref, bf_ref, qg_ref, kg_ref,
                    tri_ref, place_ref, ones_ref,
                    q_ref, k_ref, v_ref, qa_ref, ka_ref, carry_sc):
    t = pl.program_id(1)
    x = x_ref[0]
    shift, scale, _ = _mod_rows(mod_ref, 1)
    h = _adaln(x, g_ref[...], shift, scale).astype(BF16)
    nt_dims = (((1,), (1,)), ((), ()))

    def head_normed(xt, gain_ref, hd):
        rows = slice(hd * FOX_HEAD_DIM, (hd + 1) * FOX_HEAD_DIM)
        xh = xt[rows, :]
        r = lax.rsqrt(jnp.mean(xh * xh, axis=0, keepdims=True) + NORM_EPS)
        return (xh * r * gain_ref[rows, :]).astype(BF16)

    qt = lax.dot_general(wq_ref[...], h, nt_dims, preferred_element_type=F32)
    kt = lax.dot_general(wk_ref[...], h, nt_dims, preferred_element_type=F32)
    for pair in range(FOX_HEADS // 2):
        lanes = slice(pair * LANES, (pair + 1) * LANES)
        q_ref[0, 0, lanes, :] = jnp.concatenate(
            [head_normed(qt, qg_ref, 2 * pair), head_normed(qt, qg_ref, 2 * pair + 1)], axis=0)
        k_ref[0, :, lanes] = jnp.concatenate(
            [head_normed(kt, kg_ref, 2 * pair), head_normed(kt, kg_ref, 2 * pair + 1)], axis=0).T
    v_ref[0, 0] = lax.dot_general(wv_ref[...], h, nt_dims, preferred_element_type=F32).astype(BF16)
    fl = _dot(h, wf_ref[...]) + bf_ref[...]
    logf = jnp.minimum(fl, 0.0) - jnp.log(1.0 + jnp.exp(-jnp.abs(fl)))

    @pl.when(t == 0)
    def _():
        carry_sc[...] = jnp.zeros_like(carry_sc)

    tri = tri_ref[...]
    cf = sum(_dot(tri, p) for p in _split3(logf)) + carry_sc[0:1, :]
    tm = cf.shape[0]
    carry_sc[0:1, :] = cf[tm - 1:tm, :]
    pieces = jnp.concatenate(_split3(cf * LOG2E), axis=1)
    aug = _dot(pieces, place_ref[...]) + ones_ref[...]
    qa_ref[0, 0] = aug[:, :LANES].T.astype(BF16)
    ka_ref[0] = aug[:, LANES:].astype(BF16)


def _fox_project(x, mod, gain, w_in, b_f, q_gain, k_gain, tm=FOX_TILE):
    b, s, d = x.shape
    x_spec, mod_spec = _tile_specs(tm, d)
    w = w_in.astype(BF16)
    wf = jnp.zeros((d, LANES), BF16).at[:, :FOX_HEADS].set(w[:, 3 * d:])
    bf = jnp.zeros((1, LANES), F32).at[0, :FOX_HEADS].set(b_f)
    qg = jnp.tile(q_gain, FOX_HEADS).reshape(d, 1) * (FOX_HEAD_DIM ** -0.5 * LOG2E)
    kg = jnp.tile(k_gain, FOX_HEADS).reshape(d, 1)
    tri = (jnp.arange(tm)[:, None] >= jnp.arange(tm)[None, :]).astype(BF16)
    hh, ii = jnp.meshgrid(jnp.arange(FOX_HEADS), jnp.arange(3), indexing="ij")
    place = jnp.zeros((3 * LANES, 2 * LANES), F32)
    place = place.at[ii * LANES + hh, AUG_LANES * hh + 3 + ii].set(1.0)
    place = place.at[ii * LANES + hh, LANES + AUG_LANES * hh + ii].set(-1.0)
    ones = jnp.zeros((1, 2 * LANES), F32)
    ones = ones.at[0, AUG_LANES * hh + ii].set(1.0).at[0, LANES + AUG_LANES * hh + 3 + ii].set(1.0)
    act = jax.ShapeDtypeStruct((b, s, d), BF16)
    aug = jax.ShapeDtypeStruct((b, s, LANES), BF16)
    act_t = jax.ShapeDtypeStruct((b, s // tm, d, tm), BF16)
    aug_t = jax.ShapeDtypeStruct((b, s // tm, LANES, tm), BF16)
    token_major = lambda width: pl.BlockSpec((1, tm, width), lambda b_, t: (b_, t, 0))
    channel_major = lambda rows: pl.BlockSpec((1, 1, rows, tm), lambda b_, t: (b_, t, 0, 0))
    return pl.pallas_call(
        _foxproj_kernel,
        out_shape=(act_t, act, act_t, aug_t, aug),
        grid=(b, s // tm),
        in_specs=[x_spec, mod_spec, _const_spec((1, d)),
                  _const_spec((d, d)), _const_spec((d, d)), _const_spec((d, d)),
                  _const_spec((d, LANES)), _const_spec((1, LANES)),
                  _const_spec((d, 1)), _const_spec((d, 1)), _const_spec((tm, tm)),
                  _const_spec((3 * LANES, 2 * LANES)), _const_spec((1, 2 * LANES))],
        out_specs=(channel_major(d), token_major(d), channel_major(d),
                   channel_major(LANES), token_major(LANES)),
        scratch_shapes=[pltpu.VMEM((8, LANES), F32)],
        compiler_params=_params(2),
    )(x, mod, gain.reshape(1, d), w[:, :d].T, w[:, d:2 * d].T, w[:, 2 * d:3 * d].T, wf, bf, qg, kg,
      tri, place.astype(BF16), ones)


def _foxattn_kernel(TK, TQ, q_ref, qa_ref, qn_ref, qan_ref, k_ref, ka_ref, vt_ref, o_ref,
                    sa_sc, sb_sc, qt_sc, qtn_sc, m_sc, acc_sc, diff_sc):
    first_step = (pl.program_id(0) == 0) & (pl.program_id(1) == 0) & (pl.program_id(2) == 0)
    pair = pl.program_id(1)
    qi = pl.program_id(2)
    chan = lax.broadcasted_iota(jnp.int32, (LANES, 1), 0)
    zero = jnp.zeros((LANES, TK), BF16)
    for hd in range(2):
        own = (chan < FOX_HEAD_DIM) == (hd == 0)
        bias_lo = AUG_LANES * (2 * pair + hd)
        own_bias = (chan >= bias_lo) & (chan < bias_lo + AUG_LANES)
        for src, src_a, dst in ((q_ref, qa_ref, qt_sc), (qn_ref, qan_ref, qtn_sc)):
            for i in range(TQ // TK):
                cols = slice(i * TK, (i + 1) * TK)
                dst[hd, 0:LANES, cols] = jnp.where(own, src[0, i], zero)
                dst[hd, LANES:2 * LANES, cols] = jnp.where(own_bias, src_a[0, i], zero)
        m_sc[hd] = jnp.full((8, TQ), NEG_BIG, F32)
        acc_sc[hd] = jnp.zeros((V_ROWS, TQ), F32)

    @pl.when(first_step)
    def _():
        diff_sc[...] = (lax.broadcasted_iota(jnp.int32, (TK, TQ), 0)
                        - lax.broadcasted_iota(jnp.int32, (TK, TQ), 1))

    every = slice(0, TQ)
    early, late = slice(0, TK), slice(TK, TQ)

    def scores(j, s_sc, qs=every, queries=qt_sc):
        rows = pl.ds(pl.multiple_of(j * TK, TK), TK)
        kj = jnp.concatenate([k_ref[0, rows, :], ka_ref[0, rows, :]], axis=1)
        for hd in range(2):
            s_sc[hd, :, qs] = _dot(kj, queries[hd, :, qs])

    def absorb(j, s_sc, masked, qs=every):
        vt = vt_ref[0, j]
        ones = jnp.ones((V_ROWS - FOX_HEAD_DIM, TK), BF16)
        if masked:
            visible = diff_sc[:, qs] <= qi * TQ - j * TK
        for hd in range(2):
            s = s_sc[hd, :, qs]
            if masked:
                s = jnp.where(visible, s, NEG_BIG)
            m_prev = m_sc[hd, :, qs]
            m_new = jnp.maximum(m_prev, jnp.max(s, axis=0, keepdims=True))
            p = jnp.exp2(s - m_new[0:1, :]).astype(BF16)
            v_aug = jnp.concatenate([vt[hd * FOX_HEAD_DIM:(hd + 1) * FOX_HEAD_DIM, :], ones], axis=0)
            alpha = jnp.exp2(m_prev - m_new)[0:1, :]
            acc_sc[hd, :, qs] = alpha * acc_sc[hd, :, qs] + _dot(v_aug, p)
            m_sc[hd, :, qs] = m_new

    @pl.when(qi == 0)
    def _():
        scores(0, sa_sc)

    def two_tiles(jj):
        j = 2 * jj
        scores(j + 1, sb_sc)
        absorb(j, sa_sc, False)
        scores(j + 2, sa_sc)
        absorb(j + 1, sb_sc, False)

    def four_tiles(jjjj, carry):
        two_tiles(2 * jjjj)
        two_tiles(2 * jjjj + 1)
        return carry

    lax.fori_loop(0, qi // 2, four_tiles, 0)

    @pl.when(qi % 2 == 1)
    def _():
        two_tiles(qi - 1)
    scores(2 * qi + 1, sb_sc, late)
    absorb(2 * qi, sa_sc, True, early)
    scores(0, sa_sc, early, qtn_sc)
    absorb(2 * qi, sa_sc, False, late)
    scores(0, sa_sc, late, qtn_sc)
    absorb(2 * qi + 1, sb_sc, True, late)

    out_t = jnp.concatenate(
        [acc_sc[hd, 0:FOX_HEAD_DIM, :] / acc_sc[hd, FOX_HEAD_DIM:FOX_HEAD_DIM + 1, :] for hd in range(2)],
        axis=0)
    o_ref[0] = out_t.T.astype(o_ref.dtype)


def _fox_attend(qt, qat, k, ka, vt, tk=FOX_TILE):
    b, s, d = k.shape
    pairs = d // LANES
    tq = 2 * tk
    nq = s // tq
    assert qt.shape == vt.shape == (b, s // tk, d, tk)
    return pl.pallas_call(
        functools.partial(_foxattn_kernel, tk, tq),
        out_shape=jax.ShapeDtypeStruct((b, s, d), BF16),
        grid=(b, pairs, s // tq),
        in_specs=[pl.BlockSpec((1, tq // tk, LANES, tk), lambda b_, p, i: (b_, i, p, 0)),
                  pl.BlockSpec((1, tq // tk, LANES, tk), lambda b_, p, i: (b_, i, 0, 0)),
                  pl.BlockSpec((1, tq // tk, LANES, tk), lambda b_, p, i: (b_, jnp.minimum(i + 1, nq - 1), p, 0)),
                  pl.BlockSpec((1, tq // tk, LANES, tk), lambda b_, p, i: (b_, jnp.minimum(i + 1, nq - 1), 0, 0)),
                  pl.BlockSpec((1, s, LANES), lambda b_, p, i: (b_, 0, p)),
                  pl.BlockSpec((1, s, LANES), lambda b_, p, i: (b_, 0, 0)),
                  pl.BlockSpec((1, s // tk, LANES, tk), lambda b_, p, i: (b_, 0, p, 0))],
        out_specs=pl.BlockSpec((1, tq, LANES), lambda b_, p, i: (b_, i, p)),
        scratch_shapes=[pltpu.VMEM((2, tk, tq), F32), pltpu.VMEM((2, tk, tq), F32),
                        pltpu.VMEM((2, 2 * LANES, tq), BF16), pltpu.VMEM((2, 2 * LANES, tq), BF16),
                        pltpu.VMEM((2, 8, tq), F32), pltpu.VMEM((2, V_ROWS, tq), F32),
                        pltpu.VMEM((tk, tq), jnp.int32)],
        compiler_params=_params(3),
    )(qt, qat, qt, qat, k, ka, vt)


def _cmul(a, t):
    w = a.shape[1] // 2
    ar, ai, tr, ti = a[:, :w], a[:, w:], t[:, :w], t[:, w:]
    return jnp.concatenate([ar * tr - ai * ti, ar * ti + ai * tr], axis=1)


def _s5_kernel(tm, x_ref, mod_ref, g_ref, bm_ref, cm_ref, e_ref, f_ref, a_ref, tri_ref, dskip_ref,
               wglu_ref, o_ref, u_sc, y_sc, s_sc):
    t = pl.program_id(1)
    x = x_ref[0]
    shift, scale, gate = _mod_rows(mod_ref, 1)
    u_sc[...] = _adaln(x, g_ref[...], shift, scale)
    L = S5_CHUNK
    n_blocks = bm_ref.shape[0]

    @pl.when(t == 0)
    def _():
        s_sc[...] = jnp.zeros_like(s_sc)

    tri = tri_ref[...]
    chunks = [slice(c * L, (c + 1) * L) for c in range(tm // L)]
    for kb in range(n_blocks):
        cols = slice(kb * LANES, (kb + 1) * LANES)
        bu = _dot(u_sc[:, cols].astype(BF16), bm_ref[kb]).astype(BF16)
        sums = [_dot(tri, _cmul(bu[rows], e_ref[kb])) for rows in chunks]
        carried = [s_sc[kb:kb + 1, :]]
        for p in sums:
            carried.append(_cmul(p[L - 1:L, :] + carried[-1], a_ref[kb]))
        s_sc[kb:kb + 1, :] = carried[-1]
        xs = jnp.concatenate([_cmul((p + s).astype(BF16), f_ref[kb]) for p, s in zip(sums, carried)],
                             axis=0)
        y_sc[:, cols] = _dot(xs, cm_ref[kb])
    y = y_sc[...] + dskip_ref[...] * u_sc[...]
    g = jax.nn.gelu(y, approximate=True)
    out = g * jax.nn.sigmoid(_dot(g.astype(BF16), wglu_ref[...]))
    o_ref[0] = x + (1.0 + gate) * out


def _s5_tables(lam_re, lam_im, log_dt, b_re, b_im, c_re, c_im):
    g_, n_ = lam_re.shape
    i_ = b_re.shape[2]
    gl = LANES // i_
    nb = g_ // gl
    dt = jnp.exp(log_dt)[:, None]
    ar, ai = lam_re, lam_im
    mag = jnp.exp(ar * dt)
    lb_re, lb_im = mag * jnp.cos(ai * dt), mag * jnp.sin(ai * dt)
    den = ar * ar + ai * ai
    nr, ni = lb_re - 1.0, lb_im
    k_re = (nr * ar + ni * ai) / den
    k_im = (ni * ar - nr * ai) / den
    bb_re = k_re[..., None] * b_re - k_im[..., None] * b_im
    bb_im = k_re[..., None] * b_im + k_im[..., None] * b_re
    eye = jnp.eye(gl, dtype=F32)
    bb = jnp.stack([bb_re, bb_im]).reshape(2, nb, gl, n_, i_)
    bm = jnp.einsum('gh,pkgni->kgiphn', eye, bb).reshape(nb, gl * i_, 2 * gl * n_)
    cc = jnp.stack([c_re, -c_im]).reshape(2, nb, gl, i_, n_)
    cm = jnp.einsum('gh,pkgin->kpgnhi', eye, cc).reshape(nb, 2 * gl * n_, gl * i_)

    def lbar_pow(p):
        e = jnp.exp(p[:, None, None] * (ar * dt)[None])
        th = p[:, None, None] * (ai * dt)[None]
        t = jnp.stack([e * jnp.cos(th), e * jnp.sin(th)], axis=1)
        return t.reshape(-1, 2, nb, gl, n_).transpose(2, 0, 1, 3, 4).reshape(nb, -1, 2 * gl * n_)

    mid = S5_CHUNK // 2
    j = jnp.arange(S5_CHUNK, dtype=F32) - mid
    return (bm.astype(BF16), cm.astype(BF16), lbar_pow(-j).astype(BF16), lbar_pow(j).astype(BF16),
            lbar_pow(jnp.full((1,), float(S5_CHUNK), F32)))


def _s5_layer(x, mod, gain, lam_re, lam_im, log_dt, b_re, b_im, c_re, c_im, d_skip, w_glu, tm=512):
    b, s, d = x.shape
    bm, cm, e_tab, f_tab, a_tab = _s5_tables(lam_re, lam_im, log_dt, b_re, b_im, c_re, c_im)
    nb, _, sw = bm.shape
    L = S5_CHUNK
    tri = (jnp.arange(L)[:, None] >= jnp.arange(L)[None, :]).astype(BF16)
    x_spec, mod_spec = _tile_specs(tm, d)
    return pl.pallas_call(
        functools.partial(_s5_kernel, tm),
        out_shape=jax.ShapeDtypeStruct(x.shape, F32),
        grid=(b, s // tm),
        in_specs=[x_spec, mod_spec, _const_spec((1, d)),
                  _const_spec((nb, LANES, sw)), _const_spec((nb, sw, LANES)),
                  _const_spec((nb, L, sw)), _const_spec((nb, L, sw)), _const_spec((nb, 1, sw)),
                  _const_spec((L, L)), _const_spec((1, d)), _const_spec((d, d))],
        out_specs=x_spec,
        scratch_shapes=[pltpu.VMEM((tm, d), F32), pltpu.VMEM((tm, d), F32), pltpu.VMEM((nb, sw), F32)],
        compiler_params=_params(2),
    )(x, mod, gain.reshape(1, d), bm, cm, e_tab, f_tab, a_tab, tri, d_skip.reshape(1, d),
      w_glu.astype(BF16))


def _conv_kernel(tm, x_ref, mod_ref, g_ref, win_ref, cw_ref, wout_ref, o_ref, cz_sc):
    t = pl.program_id(1)
    x = x_ref[0]
    d = x.shape[1]
    shift, scale, gate = _mod_rows(mod_ref, 1)
    h = _adaln(x, g_ref[...], shift, scale).astype(BF16)

    @pl.when(t == 0)
    def _():
        cz_sc[0:CONV_HALO, :] = jnp.zeros((CONV_HALO, d), F32)

    cz_sc[CONV_HALO:, :] = _dot(h, win_ref[:, d:2 * d]) * _dot(h, win_ref[:, 2 * d:3 * d])
    cw = cw_ref[...]
    conv = (cw[0:1] * cz_sc[CONV_HALO - 2:CONV_HALO - 2 + tm, :]
            + cw[1:2] * cz_sc[CONV_HALO - 1:CONV_HALO - 1 + tm, :]
            + cw[2:3] * cz_sc[CONV_HALO:, :])
    cz_sc[0:CONV_HALO, :] = cz_sc[tm:tm + CONV_HALO, :]
    gated = (_dot(h, win_ref[:, 0:d]) * conv).astype(BF16)
    o_ref[0] = x + (1.0 + gate) * _dot(gated, wout_ref[...])


def _conv_layer(x, mod, gain, w_in, conv_w, w_out, tm=512):
    b, s, d = x.shape
    kw = conv_w.shape[0]
    x_spec, mod_spec = _tile_specs(tm, d)
    return pl.pallas_call(
        functools.partial(_conv_kernel, tm),
        out_shape=jax.ShapeDtypeStruct(x.shape, F32),
        grid=(b, s // tm),
        in_specs=[x_spec, mod_spec, _const_spec((1, d)),
                  _const_spec((d, 3 * d)), _const_spec((kw, d)), _const_spec((d, d))],
        out_specs=x_spec,
        scratch_shapes=[pltpu.VMEM((tm + CONV_HALO, d), F32)],
        compiler_params=_params(2),
    )(x, mod, gain.reshape(1, d), w_in.astype(BF16), conv_w.reshape(kw, d), w_out.astype(BF16))


def kernel(x, c, ada_w, ada_b, norm_g, ffn_w_in, ffn_w_out, pool_w, pool_scale, fox_w_in, fox_b_f, fox_q_gain, fox_k_gain, fox_w_o, s5_lam_re, s5_lam_im, s5_log_dt, s5_b_re, s5_b_im, s5_c_re, s5_c_im, s5_d, s5_w_glu, conv_w_in, conv_w, conv_w_out):
    b, s, d = x.shape
    depth = ada_w.shape[0]
    mod_all = _modulation(c, ada_w, ada_b).reshape(depth, b, N_ADA, d)
    n_mixers = 4
    for i in range(depth):
        mod = mod_all[i]
        x = _ffn(x, mod, norm_g[i, 0], ffn_w_in, ffn_w_out, i, 0, sub=0)
        m, r = i % n_mixers, i // n_mixers
        pending = None
        if m == 0:
            x = _pool_layer(x, mod, norm_g[i, 1], pool_w[r], pool_scale[r])
        elif m == 1:
            qt, k, vt, qat, ka = _fox_project(x, mod, norm_g[i, 1], fox_w_in[r], fox_b_f[r],
                                              fox_q_gain[r], fox_k_gain[r])
            pending = (_fox_attend(qt, qat, k, ka, vt), fox_w_o[r].astype(BF16))
        elif m == 2:
            x = _s5_layer(x, mod, norm_g[i, 1], s5_lam_re[r], s5_lam_im[r], s5_log_dt[r],
                          s5_b_re[r], s5_b_im[r], s5_c_re[r], s5_c_im[r], s5_d[r], s5_w_glu[r])
        else:
            x = _conv_layer(x, mod, norm_g[i, 1], conv_w_in[r], conv_w[r], conv_w_out[r])
        x = _ffn(x, mod, norm_g[i, 2], ffn_w_in, ffn_w_out, i, 1, sub=2, mixer_out=pending)
    return x
```

```python
import functools
import math

import jax
import jax.numpy as jnp
from jax import lax
from jax.experimental import pallas as pl
from jax.experimental.pallas import tpu as pltpu

F32 = jnp.float32
BF16 = jnp.bfloat16

NORM_EPS = 1e-6
N_SUBLAYERS = 3
N_ADA = 3 * N_SUBLAYERS
POOL_WINDOWS = (2, 4, 8, 16)
POOL_HALO = 32
FOX_HEADS = 16
FOX_HEAD_DIM = 64
S5_GROUP = 16
S5_STATE = 64
S5_CHUNK = 32
CONV_HALO = 8
LANES = 128
NEG_BIG = -1e30
LOG2E = math.log2(math.e)
AUG_LANES = 8
V_ROWS = FOX_HEAD_DIM + 16
FOX_TILE = 512

VMEM_LIMIT = 56 * 1024 * 1024


def _params(n_grid, vmem=VMEM_LIMIT):
    return pltpu.CompilerParams(dimension_semantics=("arbitrary",) * n_grid,
                                vmem_limit_bytes=vmem)


def _dot(a, b):
    return jnp.dot(a, b, preferred_element_type=F32)


def _adaln(x, gain, shift, scale):
    ms = jnp.mean(x * x, axis=-1, keepdims=True)
    return x * lax.rsqrt(ms + NORM_EPS) * (gain * (1.0 + scale)) + shift


def _mod_rows(mod_ref, sub):
    m = mod_ref[0]
    return m[3 * sub:3 * sub + 1], m[3 * sub + 1:3 * sub + 2], m[3 * sub + 2:3 * sub + 3]


def _const_spec(shape):
    nd = len(shape)
    return pl.BlockSpec(shape, lambda *_: (0,) * nd, pipeline_mode=pl.Buffered(1))


def _tile_specs(tm, d):
    x_spec = pl.BlockSpec((1, tm, d), lambda b, t: (b, t, 0))
    mod_spec = pl.BlockSpec((1, N_ADA, d), lambda b, t: (b, 0, 0))
    return x_spec, mod_spec


def _mod_kernel(c_ref, w_ref, b_ref, o_ref):
    c = c_ref[...]
    cond = (c * jax.nn.sigmoid(c)).astype(BF16)
    o_ref[0] = _dot(cond, w_ref[0].astype(BF16)) + b_ref[0]


def _modulation(c, ada_w, ada_b):
    depth, d, n = ada_w.shape
    b = c.shape[0]
    tn = 1024
    return pl.pallas_call(
        _mod_kernel,
        out_shape=jax.ShapeDtypeStruct((depth, b, n), F32),
        grid=(depth, n // tn),
        in_specs=[pl.BlockSpec((b, d), lambda l, j: (0, 0)),
                  pl.BlockSpec((1, d, tn), lambda l, j: (l, 0, j)),
                  pl.BlockSpec((1, 1, tn), lambda l, j: (l, 0, j))],
        out_specs=pl.BlockSpec((1, b, tn), lambda l, j: (l, 0, j)),
        compiler_params=_params(2),
    )(c, ada_w, ada_b.reshape(depth, 1, n))


def _ffn_kernel(sub, tf, mixer_out, *refs):
    if mixer_out:
        x_ref, mod_ref, g_ref, win_ref, wout_ref, att_ref, wo_ref, o_ref, h_sc, a_sc = refs
        x = x_ref[0] + (1.0 + _mod_rows(mod_ref, 1)[2]) * _dot(att_ref[0], wo_ref[...])
    else:
        x_ref, mod_ref, g_ref, win_ref, wout_ref, o_ref, h_sc, a_sc = refs
        x = x_ref[0]
    shift, scale, gate = _mod_rows(mod_ref, sub)
    h_sc[...] = _adaln(x, g_ref[...], shift, scale).astype(BF16)
    dff = a_sc.shape[1]
    for j in range(dff // tf):
        h = h_sc[...]
        g = _dot(h, win_ref[:, j * tf:(j + 1) * tf].astype(BF16))
        u = _dot(h, win_ref[:, dff + j * tf:dff + (j + 1) * tf].astype(BF16))
        a_sc[:, j * tf:(j + 1) * tf] = (g * jax.nn.sigmoid(g) * u).astype(BF16)
    y = _dot(a_sc[...], wout_ref[...].astype(BF16))
    o_ref[0] = x + (0.5 * (1.0 + gate)) * y


def _ffn(x, mod, gain, w_in_all, w_out_all, layer, half, sub, mixer_out=None, tm=512, tf=256):
    b, s, d = x.shape
    dff = w_out_all.shape[2]
    x_spec, mod_spec = _tile_specs(tm, d)

    def pick(rows, cols):
        return pl.BlockSpec((None, None, rows, cols), lambda b_, t: (layer, half, 0, 0),
                            pipeline_mode=pl.Buffered(1))

    operands = [x, mod, gain.reshape(1, d), w_in_all, w_out_all]
    in_specs = [x_spec, mod_spec, _const_spec((1, d)), pick(d, 2 * dff), pick(dff, d)]
    if mixer_out is not None:
        operands += list(mixer_out)
        in_specs += [x_spec, _const_spec((d, d))]
    return pl.pallas_call(
        functools.partial(_ffn_kernel, sub, tf, mixer_out is not None),
        out_shape=jax.ShapeDtypeStruct(x.shape, F32),
        grid=(b, s // tm),
        in_specs=in_specs,
        out_specs=x_spec,
        scratch_shapes=[pltpu.VMEM((tm, d), BF16), pltpu.VMEM((tm, dff), BF16)],
        compiler_params=_params(2),
    )(*operands)


def _pool_kernel(tm, x_ref, mod_ref, g_ref, w_ref, sc_ref, o_ref, h_sc, p_sc):
    t = pl.program_id(1)
    x = x_ref[0]
    d = x.shape[1]
    cg = d // len(POOL_WINDOWS)
    n = tm + POOL_HALO
    shift, scale, gate = _mod_rows(mod_ref, 1)

    @pl.when(t == 0)
    def _():
        h_sc[0:POOL_HALO, :] = jnp.zeros((POOL_HALO, d), F32)

    h_sc[POOL_HALO:, :] = _adaln(x, g_ref[...], shift, scale)
    pos = t * tm + lax.broadcasted_iota(jnp.int32, (tm, 1), 0)
    ys = []
    for gi, w in enumerate(POOL_WINDOWS):
        sl = slice(gi * cg, (gi + 1) * cg)
        cur = h_sc[POOL_HALO:, sl]

        def rows_of(buf):
            if buf is None:
                return lambda lo, hi: h_sc[lo:hi, sl]
            return lambda lo, hi: p_sc[buf, lo:hi, :]

        src, span, level = rows_of(None), 1, 0
        while 2 * span < w:
            level += 1
            lo = 8 * level
            p_sc[level % 2, lo:n, :] = src(lo, n) + src(lo - span, n - span)
            src, span = rows_of(level % 2), 2 * span
        acc = src(POOL_HALO, n) + src(POOL_HALO - span, n - span)
        cnt = jnp.minimum(pos + 1, w).astype(F32)
        pooled = acc / cnt - cur
        ys.append(_dot(pooled.astype(BF16), w_ref[gi]))
    y = jnp.concatenate(ys, axis=1) * sc_ref[...]
    h_sc[0:POOL_HALO, :] = h_sc[tm:tm + POOL_HALO, :]
    o_ref[0] = x + (1.0 + gate) * y


def _pool_layer(x, mod, gain, w_grp, scale, tm=512):
    b, s, d = x.shape
    ng, cg, _ = w_grp.shape
    x_spec, mod_spec = _tile_specs(tm, d)
    return pl.pallas_call(
        functools.partial(_pool_kernel, tm),
        out_shape=jax.ShapeDtypeStruct(x.shape, F32),
        grid=(b, s // tm),
        in_specs=[x_spec, mod_spec, _const_spec((1, d)),
                  _const_spec((ng, cg, cg)), _const_spec((1, d))],
        out_specs=x_spec,
        scratch_shapes=[pltpu.VMEM((tm + POOL_HALO, d), F32), pltpu.VMEM((2, tm + POOL_HALO, cg), F32)],
        compiler_params=_params(2),
    )(x, mod, gain.reshape(1, d), w_grp.astype(BF16), scale.reshape(1, d))


def _split3(v):
    p1 = v.astype(BF16)
    r1 = v - p1.astype(F32)
    p2 = r1.astype(BF16)
    p3 = (r1 - p2.astype(F32)).astype(BF16)
    return p1, p2, p3


def _foxproj_kernel(x_ref, mod_ref, g_ref, wq_ref, wk_ref, wv_ref, wf_ref, bf_ref, qg_ref, kg_ref,
                    tri_ref, place_ref, ones_ref,
                    q_ref, k_ref, v_ref, qa_ref, ka_ref, carry_sc):
    t = pl.program_id(1)
    x = x_ref[0]
    shift, scale, _ = _mod_rows(mod_ref, 1)
    h = _adaln(x, g_ref[...], shift, scale).astype(BF16)
    nt_dims = (((1,), (1,)), ((), ()))

    def head_normed(xt, gain_ref, hd):
        rows = slice(hd * FOX_HEAD_DIM, (hd + 1) * FOX_HEAD_DIM)
        xh = xt[rows, :]
        r = lax.rsqrt(jnp.mean(xh * xh, axis=0, keepdims=True) + NORM_EPS)
        return (xh * r * gain_ref[rows, :]).astype(BF16)

    qt = lax.dot_general(wq_ref[...], h, nt_dims, preferred_element_type=F32)
    kt = lax.dot_general(wk_ref[...], h, nt_dims, preferred_element_type=F32)
    for pair in range(FOX_HEADS // 2):
        lanes = slice(pair * LANES, (pair + 1) * LANES)
        q_ref[0, 0, lanes, :] = jnp.concatenate(
            [head_normed(qt, qg_ref, 2 * pair), head_normed(qt, qg_ref, 2 * pair + 1)], axis=0)
        k_ref[0, :, lanes] = jnp.concatenate(
            [head_normed(kt, kg_ref, 2 * pair), head_normed(kt, kg_ref, 2 * pair + 1)], axis=0).T
    v_ref[0, 0] = lax.dot_general(wv_ref[...], h, nt_dims, preferred_element_type=F32).astype(BF16)
    fl = _dot(h, wf_ref[...]) + bf_ref[...]
    logf = jnp.minimum(fl, 0.0) - jnp.log(1.0 + jnp.exp(-jnp.abs(fl)))

    @pl.when(t == 0)
    def _():
        carry_sc[...] = jnp.zeros_like(carry_sc)

    tri = tri_ref[...]
    cf = sum(_dot(tri, p) for p in _split3(logf)) + carry_sc[0:1, :]
    tm = cf.shape[0]
    carry_sc[0:1, :] = cf[tm - 1:tm, :]
    pieces = jnp.concatenate(_split3(cf * LOG2E), axis=1)
    aug = _dot(pieces, place_ref[...]) + ones_ref[...]
    qa_ref[0, 0] = aug[:, :LANES].T.astype(BF16)
    ka_ref[0] = aug[:, LANES:].astype(BF16)


def _fox_project(x, mod, gain, w_in, b_f, q_gain, k_gain, tm=FOX_TILE):
    b, s, d = x.shape
    x_spec, mod_spec = _tile_specs(tm, d)
    w = w_in.astype(BF16)
    wf = jnp.zeros((d, LANES), BF16).at[:, :FOX_HEADS].set(w[:, 3 * d:])
    bf = jnp.zeros((1, LANES), F32).at[0, :FOX_HEADS].set(b_f)
    qg = jnp.tile(q_gain, FOX_HEADS).reshape(d, 1) * (FOX_HEAD_DIM ** -0.5 * LOG2E)
    kg = jnp.tile(k_gain, FOX_HEADS).reshape(d, 1)
    tri = (jnp.arange(tm)[:, None] >= jnp.arange(tm)[None, :]).astype(BF16)
    hh, ii = jnp.meshgrid(jnp.arange(FOX_HEADS), jnp.arange(3), indexing="ij")
    place = jnp.zeros((3 * LANES, 2 * LANES), F32)
    place = place.at[ii * LANES + hh, AUG_LANES * hh + 3 + ii].set(1.0)
    place = place.at[ii * LANES + hh, LANES + AUG_LANES * hh + ii].set(-1.0)
    ones = jnp.zeros((1, 2 * LANES), F32)
    ones = ones.at[0, AUG_LANES * hh + ii].set(1.0).at[0, LANES + AUG_LANES * hh + 3 + ii].set(1.0)
    act = jax.ShapeDtypeStruct((b, s, d), BF16)
    aug = jax.ShapeDtypeStruct((b, s, LANES), BF16)
    act_t = jax.ShapeDtypeStruct((b, s // tm, d, tm), BF16)
    aug_t = jax.ShapeDtypeStruct((b, s // tm, LANES, tm), BF16)
    token_major = lambda width: pl.BlockSpec((1, tm, width), lambda b_, t: (b_, t, 0))
    channel_major = lambda rows: pl.BlockSpec((1, 1, rows, tm), lambda b_, t: (b_, t, 0, 0))
    return pl.pallas_call(
        _foxproj_kernel,
        out_shape=(act_t, act, act_t, aug_t, aug),
        grid=(b, s // tm),
        in_specs=[x_spec, mod_spec, _const_spec((1, d)),
                  _const_spec((d, d)), _const_spec((d, d)), _const_spec((d, d)),
                  _const_spec((d, LANES)), _const_spec((1, LANES)),
                  _const_spec((d, 1)), _const_spec((d, 1)), _const_spec((tm, tm)),
                  _const_spec((3 * LANES, 2 * LANES)), _const_spec((1, 2 * LANES))],
        out_specs=(channel_major(d), token_major(d), channel_major(d),
                   channel_major(LANES), token_major(LANES)),
        scratch_shapes=[pltpu.VMEM((8, LANES), F32)],
        compiler_params=_params(2),
    )(x, mod, gain.reshape(1, d), w[:, :d].T, w[:, d:2 * d].T, w[:, 2 * d:3 * d].T, wf, bf, qg, kg,
      tri, place.astype(BF16), ones)


def _foxattn_kernel(TK, TQ, q_ref, qa_ref, qn_ref, qan_ref, k_ref, ka_ref, vt_ref, o_ref,
                    sa_sc, sb_sc, qt_sc, qtn_sc, m_sc, acc_sc, diff_sc):
    first_step = (pl.program_id(0) == 0) & (pl.program_id(1) == 0) & (pl.program_id(2) == 0)
    pair = pl.program_id(1)
    qi = pl.program_id(2)
    chan = lax.broadcasted_iota(jnp.int32, (LANES, 1), 0)
    zero = jnp.zeros((LANES, TK), BF16)
    for hd in range(2):
        own = (chan < FOX_HEAD_DIM) == (hd == 0)
        bias_lo = AUG_LANES * (2 * pair + hd)
        own_bias = (chan >= bias_lo) & (chan < bias_lo + AUG_LANES)
        for src, src_a, dst in ((q_ref, qa_ref, qt_sc), (qn_ref, qan_ref, qtn_sc)):
            for i in range(TQ // TK):
                cols = slice(i * TK, (i + 1) * TK)
                dst[hd, 0:LANES, cols] = jnp.where(own, src[0, i], zero)
                dst[hd, LANES:2 * LANES, cols] = jnp.where(own_bias, src_a[0, i], zero)
        m_sc[hd] = jnp.full((8, TQ), NEG_BIG, F32)
        acc_sc[hd] = jnp.zeros((V_ROWS, TQ), F32)

    @pl.when(first_step)
    def _():
        diff_sc[...] = (lax.broadcasted_iota(jnp.int32, (TK, TQ), 0)
                        - lax.broadcasted_iota(jnp.int32, (TK, TQ), 1))

    every = slice(0, TQ)
    early, late = slice(0, TK), slice(TK, TQ)

    def scores(j, s_sc, qs=every, queries=qt_sc):
        rows = pl.ds(pl.multiple_of(j * TK, TK), TK)
        kj = jnp.concatenate([k_ref[0, rows, :], ka_ref[0, rows, :]], axis=1)
        for hd in range(2):
            s_sc[hd, :, qs] = _dot(kj, queries[hd, :, qs])

    def absorb(j, s_sc, masked, qs=every):
        vt = vt_ref[0, j]
        ones = jnp.ones((V_ROWS - FOX_HEAD_DIM, TK), BF16)
        if masked:
            visible = diff_sc[:, qs] <= qi * TQ - j * TK
        for hd in range(2):
            s = s_sc[hd, :, qs]
            if masked:
                s = jnp.where(visible, s, NEG_BIG)
            m_prev = m_sc[hd, :, qs]
            m_new = jnp.maximum(m_prev, jnp.max(s, axis=0, keepdims=True))
            p = jnp.exp2(s - m_new[0:1, :]).astype(BF16)
            v_aug = jnp.concatenate([vt[hd * FOX_HEAD_DIM:(hd + 1) * FOX_HEAD_DIM, :], ones], axis=0)
            alpha = jnp.exp2(m_prev - m_new)[0:1, :]
            acc_sc[hd, :, qs] = alpha * acc_sc[hd, :, qs] + _dot(v_aug, p)
            m_sc[hd, :, qs] = m_new

    @pl.when(qi == 0)
    def _():
        scores(0, sa_sc)

    def two_tiles(jj):
        j = 2 * jj
        scores(j + 1, sb_sc)
        absorb(j, sa_sc, False)
        scores(j + 2, sa_sc)
        absorb(j + 1, sb_sc, False)

    def four_tiles(jjjj, carry):
        two_tiles(2 * jjjj)
        two_tiles(2 * jjjj + 1)
        return carry

    lax.fori_loop(0, qi // 2, four_tiles, 0)

    @pl.when(qi % 2 == 1)
    def _():
        two_tiles(qi - 1)
    scores(2 * qi + 1, sb_sc, late)
    absorb(2 * qi, sa_sc, True, early)
    scores(0, sa_sc, early, qtn_sc)
    absorb(2 * qi, sa_sc, False, late)
    scores(0, sa_sc, late, qtn_sc)
    absorb(2 * qi + 1, sb_sc, True, late)

    out_t = jnp.concatenate(
        [acc_sc[hd, 0:FOX_HEAD_DIM, :] / acc_sc[hd, FOX_HEAD_DIM:FOX_HEAD_DIM + 1, :] for hd in range(2)],
        axis=0)
    o_ref[0] = out_t.T.astype(o_ref.dtype)


def _fox_attend(qt, qat, k, ka, vt, tk=FOX_TILE):
    b, s, d = k.shape
    pairs = d // LANES
    tq = 2 * tk
    nq = s // tq
    assert qt.shape == vt.shape == (b, s // tk, d, tk)
    return pl.pallas_call(
        functools.partial(_foxattn_kernel, tk, tq),
        out_shape=jax.ShapeDtypeStruct((b, s, d), BF16),
        grid=(b, pairs, s // tq),
        in_specs=[pl.BlockSpec((1, tq // tk, LANES, tk), lambda b_, p, i: (b_, i, p, 0)),
                  pl.BlockSpec((1, tq // tk, LANES, tk), lambda b_, p, i: (b_, i, 0, 0)),
                  pl.BlockSpec((1, tq // tk, LANES, tk), lambda b_, p, i: (b_, jnp.minimum(i + 1, nq - 1), p, 0)),
                  pl.BlockSpec((1, tq // tk, LANES, tk), lambda b_, p, i: (b_, jnp.minimum(i + 1, nq - 1), 0, 0)),
                  pl.BlockSpec((1, s, LANES), lambda b_, p, i: (b_, 0, p)),
                  pl.BlockSpec((1, s, LANES), lambda b_, p, i: (b_, 0, 0)),
                  pl.BlockSpec((1, s // tk, LANES, tk), lambda b_, p, i: (b_, 0, p, 0))],
        out_specs=pl.BlockSpec((1, tq, LANES), lambda b_, p, i: (b_, i, p)),
        scratch_shapes=[pltpu.VMEM((2, tk, tq), F32), pltpu.VMEM((2, tk, tq), F32),
                        pltpu.VMEM((2, 2 * LANES, tq), BF16), pltpu.VMEM((2, 2 * LANES, tq), BF16),
                        pltpu.VMEM((2, 8, tq), F32), pltpu.VMEM((2, V_ROWS, tq), F32),
                        pltpu.VMEM((tk, tq), jnp.int32)],
        compiler_params=_params(3),
    )(qt, qat, qt, qat, k, ka, vt)


def _cmul(a, t):
    w = a.shape[1] // 2
    ar, ai, tr, ti = a[:, :w], a[:, w:], t[:, :w], t[:, w:]
    return jnp.concatenate([ar * tr - ai * ti, ar * ti + ai * tr], axis=1)


def _s5_kernel(tm, x_ref, mod_ref, g_ref, bm_ref, cm_ref, e_ref, f_ref, a_ref, tri_ref, dskip_ref,
               wglu_ref, o_ref, u_sc, y_sc, s_sc):
    t = pl.program_id(1)
    x = x_ref[0]
    shift, scale, gate = _mod_rows(mod_ref, 1)
    u_sc[...] = _adaln(x, g_ref[...], shift, scale)
    L = S5_CHUNK
    n_blocks = bm_ref.shape[0]

    @pl.when(t == 0)
    def _():
        s_sc[...] = jnp.zeros_like(s_sc)

    tri = tri_ref[...]
    chunks = [slice(c * L, (c + 1) * L) for c in range(tm // L)]
    for kb in range(n_blocks):
        cols = slice(kb * LANES, (kb + 1) * LANES)
        bu = _dot(u_sc[:, cols].astype(BF16), bm_ref[kb]).astype(BF16)
        sums = [_dot(tri, _cmul(bu[rows], e_ref[kb])) for rows in chunks]
        carried = [s_sc[kb:kb + 1, :]]
        for p in sums:
            carried.append(_cmul(p[L - 1:L, :] + carried[-1], a_ref[kb]))
        s_sc[kb:kb + 1, :] = carried[-1]
        xs = jnp.concatenate([_cmul((p + s).astype(BF16), f_ref[kb]) for p, s in zip(sums, carried)],
                             axis=0)
        y_sc[:, cols] = _dot(xs, cm_ref[kb])
    y = y_sc[...] + dskip_ref[...] * u_sc[...]
    g = jax.nn.gelu(y, approximate=True)
    out = g * jax.nn.sigmoid(_dot(g.astype(BF16), wglu_ref[...]))
    o_ref[0] = x + (1.0 + gate) * out


def _s5_tables(lam_re, lam_im, log_dt, b_re, b_im, c_re, c_im):
    g_, n_ = lam_re.shape
    i_ = b_re.shape[2]
    gl = LANES // i_
    nb = g_ // gl
    dt = jnp.exp(log_dt)[:, None]
    ar, ai = lam_re, lam_im
    mag = jnp.exp(ar * dt)
    lb_re, lb_im = mag * jnp.cos(ai * dt), mag * jnp.sin(ai * dt)
    den = ar * ar + ai * ai
    nr, ni = lb_re - 1.0, lb_im
    k_re = (nr * ar + ni * ai) / den
    k_im = (ni * ar - nr * ai) / den
    bb_re = k_re[..., None] * b_re - k_im[..., None] * b_im
    bb_im = k_re[..., None] * b_im + k_im[..., None] * b_re
    eye = jnp.eye(gl, dtype=F32)
    bb = jnp.stack([bb_re, bb_im]).reshape(2, nb, gl, n_, i_)
    bm = jnp.einsum('gh,pkgni->kgiphn', eye, bb).reshape(nb, gl * i_, 2 * gl * n_)
    cc = jnp.stack([c_re, -c_im]).reshape(2, nb, gl, i_, n_)
    cm = jnp.einsum('gh,pkgin->kpgnhi', eye, cc).reshape(nb, 2 * gl * n_, gl * i_)

    def lbar_pow(p):
        e = jnp.exp(p[:, None, None] * (ar * dt)[None])
        th = p[:, None, None] * (ai * dt)[None]
        t = jnp.stack([e * jnp.cos(th), e * jnp.sin(th)], axis=1)
        return t.reshape(-1, 2, nb, gl, n_).transpose(2, 0, 1, 3, 4).reshape(nb, -1, 2 * gl * n_)

    mid = S5_CHUNK // 2
    j = jnp.arange(S5_CHUNK, dtype=F32) - mid
    return (bm.astype(BF16), cm.astype(BF16), lbar_pow(-j).astype(BF16), lbar_pow(j).astype(BF16),
            lbar_pow(jnp.full((1,), float(S5_CHUNK), F32)))


def _s5_layer(x, mod, gain, lam_re, lam_im, log_dt, b_re, b_im, c_re, c_im, d_skip, w_glu, tm=512):
    b, s, d = x.shape
    bm, cm, e_tab, f_tab, a_tab = _s5_tables(lam_re, lam_im, log_dt, b_re, b_im, c_re, c_im)
    nb, _, sw = bm.shape
    L = S5_CHUNK
    tri = (jnp.arange(L)[:, None] >= jnp.arange(L)[None, :]).astype(BF16)
    x_spec, mod_spec = _tile_specs(tm, d)
    return pl.pallas_call(
        functools.partial(_s5_kernel, tm),
        out_shape=jax.ShapeDtypeStruct(x.shape, F32),
        grid=(b, s // tm),
        in_specs=[x_spec, mod_spec, _const_spec((1, d)),
                  _const_spec((nb, LANES, sw)), _const_spec((nb, sw, LANES)),
                  _const_spec((nb, L, sw)), _const_spec((nb, L, sw)), _const_spec((nb, 1, sw)),
                  _const_spec((L, L)), _const_spec((1, d)), _const_spec((d, d))],
        out_specs=x_spec,
        scratch_shapes=[pltpu.VMEM((tm, d), F32), pltpu.VMEM((tm, d), F32), pltpu.VMEM((nb, sw), F32)],
        compiler_params=_params(2),
    )(x, mod, gain.reshape(1, d), bm, cm, e_tab, f_tab, a_tab, tri, d_skip.reshape(1, d),
      w_glu.astype(BF16))


def _conv_kernel(tm, x_ref, mod_ref, g_ref, win_ref, cw_ref, wout_ref, o_ref, cz_sc):
    t = pl.program_id(1)
    x = x_ref[0]
    d = x.shape[1]
    shift, scale, gate = _mod_rows(mod_ref, 1)
    h = _adaln(x, g_ref[...], shift, scale).astype(BF16)

    @pl.when(t == 0)
    def _():
        cz_sc[0:CONV_HALO, :] = jnp.zeros((CONV_HALO, d), F32)

    cz_sc[CONV_HALO:, :] = _dot(h, win_ref[:, d:2 * d]) * _dot(h, win_ref[:, 2 * d:3 * d])
    cw = cw_ref[...]
    conv = (cw[0:1] * cz_sc[CONV_HALO - 2:CONV_HALO - 2 + tm, :]
            + cw[1:2] * cz_sc[CONV_HALO - 1:CONV_HALO - 1 + tm, :]
            + cw[2:3] * cz_sc[CONV_HALO:, :])
    cz_sc[0:CONV_HALO, :] = cz_sc[tm:tm + CONV_HALO, :]
    gated = (_dot(h, win_ref[:, 0:d]) * conv).astype(BF16)
    o_ref[0] = x + (1.0 + gate) * _dot(gated, wout_ref[...])


def _conv_layer(x, mod, gain, w_in, conv_w, w_out, tm=512):
    b, s, d = x.shape
    kw = conv_w.shape[0]
    x_spec, mod_spec = _tile_specs(tm, d)
    return pl.pallas_call(
        functools.partial(_conv_kernel, tm),
        out_shape=jax.ShapeDtypeStruct(x.shape, F32),
        grid=(b, s // tm),
        in_specs=[x_spec, mod_spec, _const_spec((1, d)),
                  _const_spec((d, 3 * d)), _const_spec((kw, d)), _const_spec((d, d))],
        out_specs=x_spec,
        scratch_shapes=[pltpu.VMEM((tm + CONV_HALO, d), F32)],
        compiler_params=_params(2),
    )(x, mod, gain.reshape(1, d), w_in.astype(BF16), conv_w.reshape(kw, d), w_out.astype(BF16))


def kernel(x, c, ada_w, ada_b, norm_g, ffn_w_in, ffn_w_out, pool_w, pool_scale, fox_w_in, fox_b_f, fox_q_gain, fox_k_gain, fox_w_o, s5_lam_re, s5_lam_im, s5_log_dt, s5_b_re, s5_b_im, s5_c_re, s5_c_im, s5_d, s5_w_glu, conv_w_in, conv_w, conv_w_out):
    b, s, d = x.shape
    depth = ada_w.shape[0]
    mod_all = _modulation(c, ada_w, ada_b).reshape(depth, b, N_ADA, d)
    n_mixers = 4
    for i in range(depth):
        mod = mod_all[i]
        x = _ffn(x, mod, norm_g[i, 0], ffn_w_in, ffn_w_out, i, 0, sub=0)
        m, r = i % n_mixers, i // n_mixers
        pending = None
        if m == 0:
            x = _pool_layer(x, mod, norm_g[i, 1], pool_w[r], pool_scale[r])
        elif m == 1:
            qt, k, vt, qat, ka = _fox_project(x, mod, norm_g[i, 1], fox_w_in[r], fox_b_f[r],
                                              fox_q_gain[r], fox_k_gain[r])
            pending = (_fox_attend(qt, qat, k, ka, vt), fox_w_o[r].astype(BF16))
        elif m == 2:
            x = _s5_layer(x, mod, norm_g[i, 1], s5_lam_re[r], s5_lam_im[r], s5_log_dt[r],
                          s5_b_re[r], s5_b_im[r], s5_c_re[r], s5_c_im[r], s5_d[r], s5_w_glu[r])
        else:
            x = _conv_layer(x, mod, norm_g[i, 1], conv_w_in[r], conv_w[r], conv_w_out[r])
        x = _ffn(x, mod, norm_g[i, 2], ffn_w_in, ffn_w_out, i, 1, sub=2, mixer_out=pending)
    return x
```

```python
import functools
import math

import jax
import jax.numpy as jnp
from jax import lax
from jax.experimental import pallas as pl
from jax.experimental.pallas import tpu as pltpu

F32 = jnp.float32
BF16 = jnp.bfloat16

NORM_EPS = 1e-6
N_SUBLAYERS = 3
N_ADA = 3 * N_SUBLAYERS
POOL_WINDOWS = (2, 4, 8, 16)
POOL_HALO = 32
FOX_HEADS = 16
FOX_HEAD_DIM = 64
S5_GROUP = 16
S5_STATE = 64
S5_CHUNK = 32
S5_MAX_LOG_RATE = 5.0
CONV_HALO = 8
LANES = 128
NEG_BIG = -1e30
LOG2E = math.log2(math.e)
AUG_LANES = 8
V_ROWS = FOX_HEAD_DIM + 16
FOX_TILE = 512

VMEM_LIMIT = 56 * 1024 * 1024


def _params(n_grid, vmem=VMEM_LIMIT):
    return pltpu.CompilerParams(dimension_semantics=("arbitrary",) * n_grid,
                                vmem_limit_bytes=vmem)


def _dot(a, b):
    return jnp.dot(a, b, preferred_element_type=F32)


def _adaln(x, gain, shift, scale):
    ms = jnp.mean(x * x, axis=-1, keepdims=True)
    return x * lax.rsqrt(ms + NORM_EPS) * (gain * (1.0 + scale)) + shift


def _mod_rows(mod_ref, sub):
    m = mod_ref[0]
    return m[3 * sub:3 * sub + 1], m[3 * sub + 1:3 * sub + 2], m[3 * sub + 2:3 * sub + 3]


def _const_spec(shape):
    nd = len(shape)
    return pl.BlockSpec(shape, lambda *_: (0,) * nd, pipeline_mode=pl.Buffered(1))


def _tile_specs(tm, d):
    x_spec = pl.BlockSpec((1, tm, d), lambda b, t: (b, t, 0))
    mod_spec = pl.BlockSpec((1, N_ADA, d), lambda b, t: (b, 0, 0))
    return x_spec, mod_spec


def _mod_kernel(c_ref, w_ref, b_ref, o_ref):
    c = c_ref[...]
    cond = (c * jax.nn.sigmoid(c)).astype(BF16)
    o_ref[0] = _dot(cond, w_ref[0].astype(BF16)) + b_ref[0]


def _modulation(c, ada_w, ada_b):
    depth, d, n = ada_w.shape
    b = c.shape[0]
    tn = 1024
    return pl.pallas_call(
        _mod_kernel,
        out_shape=jax.ShapeDtypeStruct((depth, b, n), F32),
        grid=(depth, n // tn),
        in_specs=[pl.BlockSpec((b, d), lambda l, j: (0, 0)),
                  pl.BlockSpec((1, d, tn), lambda l, j: (l, 0, j)),
                  pl.BlockSpec((1, 1, tn), lambda l, j: (l, 0, j))],
        out_specs=pl.BlockSpec((1, b, tn), lambda l, j: (l, 0, j)),
        compiler_params=_params(2),
    )(c, ada_w, ada_b.reshape(depth, 1, n))


def _ffn_kernel(sub, tf, mixer_out, *refs):
    if mixer_out:
        x_ref, mod_ref, g_ref, win_ref, wout_ref, att_ref, wo_ref, o_ref, h_sc, a_sc = refs
        x = x_ref[0] + (1.0 + _mod_rows(mod_ref, 1)[2]) * _dot(att_ref[0], wo_ref[...])
    else:
        x_ref, mod_ref, g_ref, win_ref, wout_ref, o_ref, h_sc, a_sc = refs
        x = x_ref[0]
    shift, scale, gate = _mod_rows(mod_ref, sub)
    h_sc[...] = _adaln(x, g_ref[...], shift, scale).astype(BF16)
    dff = a_sc.shape[1]
    for j in range(dff // tf):
        h = h_sc[...]
        g = _dot(h, win_ref[:, j * tf:(j + 1) * tf].astype(BF16))
        u = _dot(h, win_ref[:, dff + j * tf:dff + (j + 1) * tf].astype(BF16))
        a_sc[:, j * tf:(j + 1) * tf] = (g * jax.nn.sigmoid(g) * u).astype(BF16)
    y = _dot(a_sc[...], wout_ref[...].astype(BF16))
    o_ref[0] = x + (0.5 * (1.0 + gate)) * y


def _ffn(x, mod, gain, w_in_all, w_out_all, layer, half, sub, mixer_out=None, tm=512, tf=256):
    b, s, d = x.shape
    dff = w_out_all.shape[2]
    x_spec, mod_spec = _tile_specs(tm, d)

    def pick(rows, cols):
        return pl.BlockSpec((None, None, rows, cols), lambda b_, t: (layer, half, 0, 0),
                            pipeline_mode=pl.Buffered(1))

    operands = [x, mod, gain.reshape(1, d), w_in_all, w_out_all]
    in_specs = [x_spec, mod_spec, _const_spec((1, d)), pick(d, 2 * dff), pick(dff, d)]
    if mixer_out is not None:
        operands += list(mixer_out)
        in_specs += [x_spec, _const_spec((d, d))]
    return pl.pallas_call(
        functools.partial(_ffn_kernel, sub, tf, mixer_out is not None),
        out_shape=jax.ShapeDtypeStruct(x.shape, F32),
        grid=(b, s // tm),
        in_specs=in_specs,
        out_specs=x_spec,
        scratch_shapes=[pltpu.VMEM((tm, d), BF16), pltpu.VMEM((tm, dff), BF16)],
        compiler_params=_params(2),
    )(*operands)


def _pool_kernel(tm, x_ref, mod_ref, g_ref, w_ref, sc_ref, o_ref, h_sc, p_sc):
    t = pl.program_id(1)
    x = x_ref[0]
    d = x.shape[1]
    cg = d // len(POOL_WINDOWS)
    n = tm + POOL_HALO
    shift, scale, gate = _mod_rows(mod_ref, 1)

    @pl.when(t == 0)
    def _():
        h_sc[0:POOL_HALO, :] = jnp.zeros((POOL_HALO, d), F32)

    h_sc[POOL_HALO:, :] = _adaln(x, g_ref[...], shift, scale)
    pos = t * tm + lax.broadcasted_iota(jnp.int32, (tm, 1), 0)
    ys = []
    for gi, w in enumerate(POOL_WINDOWS):
        sl = slice(gi * cg, (gi + 1) * cg)
        cur = h_sc[POOL_HALO:, sl]

        def rows_of(buf):
            if buf is None:
                return lambda lo, hi: h_sc[lo:hi, sl]
            return lambda lo, hi: p_sc[buf, lo:hi, :]

        src, span, level = rows_of(None), 1, 0
        while 2 * span < w:
            level += 1
            lo = 8 * level
            p_sc[level % 2, lo:n, :] = src(lo, n) + src(lo - span, n - span)
            src, span = rows_of(level % 2), 2 * span
        acc = src(POOL_HALO, n) + src(POOL_HALO - span, n - span)
        cnt = jnp.minimum(pos + 1, w).astype(F32)
        pooled = acc / cnt - cur
        ys.append(_dot(pooled.astype(BF16), w_ref[gi]))
    y = jnp.concatenate(ys, axis=1) * sc_ref[...]
    h_sc[0:POOL_HALO, :] = h_sc[tm:tm + POOL_HALO, :]
    o_ref[0] = x + (1.0 + gate) * y


def _pool_layer(x, mod, gain, w_grp, scale, tm=512):
    b, s, d = x.shape
    ng, cg, _ = w_grp.shape
    x_spec, mod_spec = _tile_specs(tm, d)
    return pl.pallas_call(
        functools.partial(_pool_kernel, tm),
        out_shape=jax.ShapeDtypeStruct(x.shape, F32),
        grid=(b, s // tm),
        in_specs=[x_spec, mod_spec, _const_spec((1, d)),
                  _const_spec((ng, cg, cg)), _const_spec((1, d))],
        out_specs=x_spec,
        scratch_shapes=[pltpu.VMEM((tm + POOL_HALO, d), F32), pltpu.VMEM((2, tm + POOL_HALO, cg), F32)],
        compiler_params=_params(2),
    )(x, mod, gain.reshape(1, d), w_grp.astype(BF16), scale.reshape(1, d))


def _split3(v):
    p1 = v.astype(BF16)
    r1 = v - p1.astype(F32)
    p2 = r1.astype(BF16)
    p3 = (r1 - p2.astype(F32)).astype(BF16)
    return p1, p2, p3


def _foxproj_kernel(x_ref, mod_ref, g_ref, wq_ref, wk_ref, wv_ref, wf_ref, bf_ref, qg_ref, kg_ref,
                    tri_ref, place_ref, ones_ref,
                    q_ref, k_ref, v_ref, qa_ref, ka_ref, carry_sc):
    t = pl.program_id(1)
    x = x_ref[0]
    shift, scale, _ = _mod_rows(mod_ref, 1)
    h = _adaln(x, g_ref[...], shift, scale).astype(BF16)
    nt_dims = (((1,), (1,)), ((), ()))

    def head_normed(xt, gain_ref, hd):
        rows = slice(hd * FOX_HEAD_DIM, (hd + 1) * FOX_HEAD_DIM)
        xh = xt[rows, :]
        r = lax.rsqrt(jnp.mean(xh * xh, axis=0, keepdims=True) + NORM_EPS)
        return (xh * r * gain_ref[rows, :]).astype(BF16)

    qt = lax.dot_general(wq_ref[...], h, nt_dims, preferred_element_type=F32)
    kt = lax.dot_general(wk_ref[...], h, nt_dims, preferred_element_type=F32)
    for pair in range(FOX_HEADS // 2):
        lanes = slice(pair * LANES, (pair + 1) * LANES)
        q_ref[0, 0, lanes, :] = jnp.concatenate(
            [head_normed(qt, qg_ref, 2 * pair), head_normed(qt, qg_ref, 2 * pair + 1)], axis=0)
        k_ref[0, :, lanes] = jnp.concatenate(
            [head_normed(kt, kg_ref, 2 * pair), head_normed(kt, kg_ref, 2 * pair + 1)], axis=0).T
    v_ref[0, 0] = lax.dot_general(wv_ref[...], h, nt_dims, preferred_element_type=F32).astype(BF16)
    fl = _dot(h, wf_ref[...]) + bf_ref[...]
    logf = jnp.minimum(fl, 0.0) - jnp.log(1.0 + jnp.exp(-jnp.abs(fl)))

    @pl.when(t == 0)
    def _():
        carry_sc[...] = jnp.zeros_like(carry_sc)

    tri = tri_ref[...]
    cf = sum(_dot(tri, p) for p in _split3(logf)) + carry_sc[0:1, :]
    tm = cf.shape[0]
    carry_sc[0:1, :] = cf[tm - 1:tm, :]
    pieces = jnp.concatenate(_split3(cf * LOG2E), axis=1)
    aug = _dot(pieces, place_ref[...]) + ones_ref[...]
    qa_ref[0, 0] = aug[:, :LANES].T.astype(BF16)
    ka_ref[0] = aug[:, LANES:].astype(BF16)


def _fox_project(x, mod, gain, w_in, b_f, q_gain, k_gain, tm=FOX_TILE):
    b, s, d = x.shape
    x_spec, mod_spec = _tile_specs(tm, d)
    w = w_in.astype(BF16)
    wf = jnp.zeros((d, LANES), BF16).at[:, :FOX_HEADS].set(w[:, 3 * d:])
    bf = jnp.zeros((1, LANES), F32).at[0, :FOX_HEADS].set(b_f)
    qg = jnp.tile(q_gain, FOX_HEADS).reshape(d, 1) * (FOX_HEAD_DIM ** -0.5 * LOG2E)
    kg = jnp.tile(k_gain, FOX_HEADS).reshape(d, 1)
    tri = (jnp.arange(tm)[:, None] >= jnp.arange(tm)[None, :]).astype(BF16)
    hh, ii = jnp.meshgrid(jnp.arange(FOX_HEADS), jnp.arange(3), indexing="ij")
    place = jnp.zeros((3 * LANES, 2 * LANES), F32)
    place = place.at[ii * LANES + hh, AUG_LANES * hh + 3 + ii].set(1.0)
    place = place.at[ii * LANES + hh, LANES + AUG_LANES * hh + ii].set(-1.0)
    ones = jnp.zeros((1, 2 * LANES), F32)
    ones = ones.at[0, AUG_LANES * hh + ii].set(1.0).at[0, LANES + AUG_LANES * hh + 3 + ii].set(1.0)
    act = jax.ShapeDtypeStruct((b, s, d), BF16)
    aug = jax.ShapeDtypeStruct((b, s, LANES), BF16)
    act_t = jax.ShapeDtypeStruct((b, s // tm, d, tm), BF16)
    aug_t = jax.ShapeDtypeStruct((b, s // tm, LANES, tm), BF16)
    token_major = lambda width: pl.BlockSpec((1, tm, width), lambda b_, t: (b_, t, 0))
    channel_major = lambda rows: pl.BlockSpec((1, 1, rows, tm), lambda b_, t: (b_, t, 0, 0))
    return pl.pallas_call(
        _foxproj_kernel,
        out_shape=(act_t, act, act_t, aug_t, aug),
        grid=(b, s // tm),
        in_specs=[x_spec, mod_spec, _const_spec((1, d)),
                  _const_spec((d, d)), _const_spec((d, d)), _const_spec((d, d)),
                  _const_spec((d, LANES)), _const_spec((1, LANES)),
                  _const_spec((d, 1)), _const_spec((d, 1)), _const_spec((tm, tm)),
                  _const_spec((3 * LANES, 2 * LANES)), _const_spec((1, 2 * LANES))],
        out_specs=(channel_major(d), token_major(d), channel_major(d),
                   channel_major(LANES), token_major(LANES)),
        scratch_shapes=[pltpu.VMEM((8, LANES), F32)],
        compiler_params=_params(2),
    )(x, mod, gain.reshape(1, d), w[:, :d].T, w[:, d:2 * d].T, w[:, 2 * d:3 * d].T, wf, bf, qg, kg,
      tri, place.astype(BF16), ones)


def _foxattn_kernel(TK, TQ, q_ref, qa_ref, qn_ref, qan_ref, k_ref, ka_ref, vt_ref, o_ref,
                    sa_sc, sb_sc, qt_sc, qtn_sc, m_sc, acc_sc, diff_sc):
    first_step = (pl.program_id(0) == 0) & (pl.program_id(1) == 0) & (pl.program_id(2) == 0)
    pair = pl.program_id(1)
    qi = pl.program_id(2)
    chan = lax.broadcasted_iota(jnp.int32, (LANES, 1), 0)
    zero = jnp.zeros((LANES, TK), BF16)
    for hd in range(2):
        own = (chan < FOX_HEAD_DIM) == (hd == 0)
        bias_lo = AUG_LANES * (2 * pair + hd)
        own_bias = (chan >= bias_lo) & (chan < bias_lo + AUG_LANES)
        for src, src_a, dst in ((q_ref, qa_ref, qt_sc), (qn_ref, qan_ref, qtn_sc)):
            for i in range(TQ // TK):
                cols = slice(i * TK, (i + 1) * TK)
                dst[hd, 0:LANES, cols] = jnp.where(own, src[0, i], zero)
                dst[hd, LANES:2 * LANES, cols] = jnp.where(own_bias, src_a[0, i], zero)
        m_sc[hd] = jnp.full((8, TQ), NEG_BIG, F32)
        acc_sc[hd] = jnp.zeros((V_ROWS, TQ), F32)

    @pl.when(first_step)
    def _():
        diff_sc[...] = (lax.broadcasted_iota(jnp.int32, (TK, TQ), 0)
                        - lax.broadcasted_iota(jnp.int32, (TK, TQ), 1))

    every = slice(0, TQ)
    early, late = slice(0, TK), slice(TK, TQ)

    def scores(j, s_sc, qs=every, queries=qt_sc):
        rows = pl.ds(pl.multiple_of(j * TK, TK), TK)
        kj = jnp.concatenate([k_ref[0, rows, :], ka_ref[0, rows, :]], axis=1)
        for hd in range(2):
            s_sc[hd, :, qs] = _dot(kj, queries[hd, :, qs])

    def absorb(j, s_sc, masked, qs=every):
        vt = vt_ref[0, j]
        ones = jnp.ones((V_ROWS - FOX_HEAD_DIM, TK), BF16)
        if masked:
            visible = diff_sc[:, qs] <= qi * TQ - j * TK
        for hd in range(2):
            s = s_sc[hd, :, qs]
            if masked:
                s = jnp.where(visible, s, NEG_BIG)
            m_prev = m_sc[hd, :, qs]
            m_new = jnp.maximum(m_prev, jnp.max(s, axis=0, keepdims=True))
            p = jnp.exp2(s - m_new[0:1, :]).astype(BF16)
            v_aug = jnp.concatenate([vt[hd * FOX_HEAD_DIM:(hd + 1) * FOX_HEAD_DIM, :], ones], axis=0)
            alpha = jnp.exp2(m_prev - m_new)[0:1, :]
            acc_sc[hd, :, qs] = alpha * acc_sc[hd, :, qs] + _dot(v_aug, p)
            m_sc[hd, :, qs] = m_new

    @pl.when(qi == 0)
    def _():
        scores(0, sa_sc)

    def two_tiles(jj):
        j = 2 * jj
        scores(j + 1, sb_sc)
        absorb(j, sa_sc, False)
        scores(j + 2, sa_sc)
        absorb(j + 1, sb_sc, False)

    def four_tiles(jjjj, carry):
        two_tiles(2 * jjjj)
        two_tiles(2 * jjjj + 1)
        return carry

    lax.fori_loop(0, qi // 2, four_tiles, 0)

    @pl.when(qi % 2 == 1)
    def _():
        two_tiles(qi - 1)
    scores(2 * qi + 1, sb_sc, late)
    absorb(2 * qi, sa_sc, True, early)
    scores(0, sa_sc, early, qtn_sc)
    absorb(2 * qi, sa_sc, False, late)
    scores(0, sa_sc, late, qtn_sc)
    absorb(2 * qi + 1, sb_sc, True, late)

    out_t = jnp.concatenate(
        [acc_sc[hd, 0:FOX_HEAD_DIM, :] / acc_sc[hd, FOX_HEAD_DIM:FOX_HEAD_DIM + 1, :] for hd in range(2)],
        axis=0)
    o_ref[0] = out_t.T.astype(o_ref.dtype)


def _fox_attend(qt, qat, k, ka, vt, tk=FOX_TILE):
    b, s, d = k.shape
    pairs = d // LANES
    tq = 2 * tk
    nq = s // tq
    assert qt.shape == vt.shape == (b, s // tk, d, tk)
    return pl.pallas_call(
        functools.partial(_foxattn_kernel, tk, tq),
        out_shape=jax.ShapeDtypeStruct((b, s, d), BF16),
        grid=(b, pairs, s // tq),
        in_specs=[pl.BlockSpec((1, tq // tk, LANES, tk), lambda b_, p, i: (b_, i, p, 0)),
                  pl.BlockSpec((1, tq // tk, LANES, tk), lambda b_, p, i: (b_, i, 0, 0)),
                  pl.BlockSpec((1, tq // tk, LANES, tk), lambda b_, p, i: (b_, jnp.minimum(i + 1, nq - 1), p, 0)),
                  pl.BlockSpec((1, tq // tk, LANES, tk), lambda b_, p, i: (b_, jnp.minimum(i + 1, nq - 1), 0, 0)),
                  pl.BlockSpec((1, s, LANES), lambda b_, p, i: (b_, 0, p)),
                  pl.BlockSpec((1, s, LANES), lambda b_, p, i: (b_, 0, 0)),
                  pl.BlockSpec((1, s // tk, LANES, tk), lambda b_, p, i: (b_, 0, p, 0))],
        out_specs=pl.BlockSpec((1, tq, LANES), lambda b_, p, i: (b_, i, p)),
        scratch_shapes=[pltpu.VMEM((2, tk, tq), F32), pltpu.VMEM((2, tk, tq), F32),
                        pltpu.VMEM((2, 2 * LANES, tq), BF16), pltpu.VMEM((2, 2 * LANES, tq), BF16),
                        pltpu.VMEM((2, 8, tq), F32), pltpu.VMEM((2, V_ROWS, tq), F32),
                        pltpu.VMEM((tk, tq), jnp.int32)],
        compiler_params=_params(3),
    )(qt, qat, qt, qat, k, ka, vt)


def _cmul(a, t):
    w = a.shape[1] // 2
    ar, ai, tr, ti = a[:, :w], a[:, w:], t[:, :w], t[:, w:]
    return jnp.concatenate([ar * tr - ai * ti, ar * ti + ai * tr], axis=1)


def _s5_kernel(tm, x_ref, mod_ref, g_ref, bm_ref, cm_ref, e_ref, f_ref, a_ref, tri_ref, dskip_ref,
               wglu_ref, o_ref, u_sc, y_sc, s_sc):
    t = pl.program_id(1)
    x = x_ref[0]
    shift, scale, gate = _mod_rows(mod_ref, 1)
    u_sc[...] = _adaln(x, g_ref[...], shift, scale)
    L = S5_CHUNK
    n_blocks = bm_ref.shape[0]

    @pl.when(t == 0)
    def _():
        s_sc[...] = jnp.zeros_like(s_sc)

    tri = tri_ref[...]
    chunks = [slice(c * L, (c + 1) * L) for c in range(tm // L)]
    for kb in range(n_blocks):
        cols = slice(kb * LANES, (kb + 1) * LANES)
        bu = _dot(u_sc[:, cols].astype(BF16), bm_ref[kb]).astype(BF16)
        sums = [_dot(tri, _cmul(bu[rows], e_ref[kb])) for rows in chunks]
        carried = [s_sc[kb:kb + 1, :]]
        for p in sums:
            carried.append(_cmul(p[L - 1:L, :] + carried[-1], a_ref[kb]))
        s_sc[kb:kb + 1, :] = carried[-1]
        xs = jnp.concatenate([_cmul((p + s).astype(BF16), f_ref[kb]) for p, s in zip(sums, carried)],
                             axis=0)
        y_sc[:, cols] = _dot(xs, cm_ref[kb])
    y = y_sc[...] + dskip_ref[...] * u_sc[...]
    g = jax.nn.gelu(y, approximate=True)
    out = g * jax.nn.sigmoid(_dot(g.astype(BF16), wglu_ref[...]))
    o_ref[0] = x + (1.0 + gate) * out


def _s5_tables(lam_re, lam_im, log_dt, b_re, b_im, c_re, c_im):
    g_, n_ = lam_re.shape
    i_ = b_re.shape[2]
    gl = LANES // i_
    nb = g_ // gl
    dt = jnp.exp(log_dt)[:, None]
    ar, ai = lam_re, lam_im
    mag = jnp.exp(ar * dt)
    lb_re, lb_im = mag * jnp.cos(ai * dt), mag * jnp.sin(ai * dt)
    den = ar * ar + ai * ai
    nr, ni = lb_re - 1.0, lb_im
    k_re = (nr * ar + ni * ai) / den
    k_im = (ni * ar - nr * ai) / den
    bb_re = k_re[..., None] * b_re - k_im[..., None] * b_im
    bb_im = k_re[..., None] * b_im + k_im[..., None] * b_re
    eye = jnp.eye(gl, dtype=F32)
    bb = jnp.stack([bb_re, bb_im]).reshape(2, nb, gl, n_, i_)
    bm = jnp.einsum('gh,pkgni->kgiphn', eye, bb).reshape(nb, gl * i_, 2 * gl * n_)
    cc = jnp.stack([c_re, -c_im]).reshape(2, nb, gl, i_, n_)
    cm = jnp.einsum('gh,pkgin->kpgnhi', eye, cc).reshape(nb, 2 * gl * n_, gl * i_)

    def lbar_pow(p, max_log_rate=None):
        log_rate = ar * dt
        if max_log_rate is not None:
            log_rate = jnp.clip(log_rate, -max_log_rate, max_log_rate)
        e = jnp.exp(p[:, None, None] * log_rate[None])
        th = p[:, None, None] * (ai * dt)[None]
        t = jnp.stack([e * jnp.cos(th), e * jnp.sin(th)], axis=1)
        return t.reshape(-1, 2, nb, gl, n_).transpose(2, 0, 1, 3, 4).reshape(nb, -1, 2 * gl * n_)

    mid = S5_CHUNK // 2
    j = jnp.arange(S5_CHUNK, dtype=F32) - mid
    return (bm.astype(BF16), cm.astype(BF16),
            lbar_pow(-j, S5_MAX_LOG_RATE).astype(BF16), lbar_pow(j, S5_MAX_LOG_RATE).astype(BF16),
            lbar_pow(jnp.full((1,), float(S5_CHUNK), F32)))


def _s5_layer(x, mod, gain, lam_re, lam_im, log_dt, b_re, b_im, c_re, c_im, d_skip, w_glu, tm=512):
    b, s, d = x.shape
    bm, cm, e_tab, f_tab, a_tab = _s5_tables(lam_re, lam_im, log_dt, b_re, b_im, c_re, c_im)
    nb, _, sw = bm.shape
    L = S5_CHUNK
    tri = (jnp.arange(L)[:, None] >= jnp.arange(L)[None, :]).astype(BF16)
    x_spec, mod_spec = _tile_specs(tm, d)
    return pl.pallas_call(
        functools.partial(_s5_kernel, tm),
        out_shape=jax.ShapeDtypeStruct(x.shape, F32),
        grid=(b, s // tm),
        in_specs=[x_spec, mod_spec, _const_spec((1, d)),
                  _const_spec((nb, LANES, sw)), _const_spec((nb, sw, LANES)),
                  _const_spec((nb, L, sw)), _const_spec((nb, L, sw)), _const_spec((nb, 1, sw)),
                  _const_spec((L, L)), _const_spec((1, d)), _const_spec((d, d))],
        out_specs=x_spec,
        scratch_shapes=[pltpu.VMEM((tm, d), F32), pltpu.VMEM((tm, d), F32), pltpu.VMEM((nb, sw), F32)],
        compiler_params=_params(2),
    )(x, mod, gain.reshape(1, d), bm, cm, e_tab, f_tab, a_tab, tri, d_skip.reshape(1, d),
      w_glu.astype(BF16))


def _conv_kernel(tm, x_ref, mod_ref, g_ref, win_ref, cw_ref, wout_ref, o_ref, cz_sc):
    t = pl.program_id(1)
    x = x_ref[0]
    d = x.shape[1]
    shift, scale, gate = _mod_rows(mod_ref, 1)
    h = _adaln(x, g_ref[...], shift, scale).astype(BF16)

    @pl.when(t == 0)
    def _():
        cz_sc[0:CONV_HALO, :] = jnp.zeros((CONV_HALO, d), F32)

    cz_sc[CONV_HALO:, :] = _dot(h, win_ref[:, d:2 * d]) * _dot(h, win_ref[:, 2 * d:3 * d])
    cw = cw_ref[...]
    conv = (cw[0:1] * cz_sc[CONV_HALO - 2:CONV_HALO - 2 + tm, :]
            + cw[1:2] * cz_sc[CONV_HALO - 1:CONV_HALO - 1 + tm, :]
            + cw[2:3] * cz_sc[CONV_HALO:, :])
    cz_sc[0:CONV_HALO, :] = cz_sc[tm:tm + CONV_HALO, :]
    gated = (_dot(h, win_ref[:, 0:d]) * conv).astype(BF16)
    o_ref[0] = x + (1.0 + gate) * _dot(gated, wout_ref[...])


def _conv_layer(x, mod, gain, w_in, conv_w, w_out, tm=512):
    b, s, d = x.shape
    kw = conv_w.shape[0]
    x_spec, mod_spec = _tile_specs(tm, d)
    return pl.pallas_call(
        functools.partial(_conv_kernel, tm),
        out_shape=jax.ShapeDtypeStruct(x.shape, F32),
        grid=(b, s // tm),
        in_specs=[x_spec, mod_spec, _const_spec((1, d)),
                  _const_spec((d, 3 * d)), _const_spec((kw, d)), _const_spec((d, d))],
        out_specs=x_spec,
        scratch_shapes=[pltpu.VMEM((tm + CONV_HALO, d), F32)],
        compiler_params=_params(2),
    )(x, mod, gain.reshape(1, d), w_in.astype(BF16), conv_w.reshape(kw, d), w_out.astype(BF16))


def kernel(x, c, ada_w, ada_b, norm_g, ffn_w_in, ffn_w_out, pool_w, pool_scale, fox_w_in, fox_b_f, fox_q_gain, fox_k_gain, fox_w_o, s5_lam_re, s5_lam_im, s5_log_dt, s5_b_re, s5_b_im, s5_c_re, s5_c_im, s5_d, s5_w_glu, conv_w_in, conv_w, conv_w_out):
    b, s, d = x.shape
    depth = ada_w.shape[0]
    mod_all = _modulation(c, ada_w, ada_b).reshape(depth, b, N_ADA, d)
    n_mixers = 4
    for i in range(depth):
        mod = mod_all[i]
        x = _ffn(x, mod, norm_g[i, 0], ffn_w_in, ffn_w_out, i, 0, sub=0)
        m, r = i % n_mixers, i // n_mixers
        pending = None
        if m == 0:
            x = _pool_layer(x, mod, norm_g[i, 1], pool_w[r], pool_scale[r])
        elif m == 1:
            qt, k, vt, qat, ka = _fox_project(x, mod, norm_g[i, 1], fox_w_in[r], fox_b_f[r],
                                              fox_q_gain[r], fox_k_gain[r])
            pending = (_fox_attend(qt, qat, k, ka, vt), fox_w_o[r].astype(BF16))
        elif m == 2:
            x = _s5_layer(x, mod, norm_g[i, 1], s5_lam_re[r], s5_lam_im[r], s5_log_dt[r],
                          s5_b_re[r], s5_b_im[r], s5_c_re[r], s5_c_im[r], s5_d[r], s5_w_glu[r])
        else:
            x = _conv_layer(x, mod, norm_g[i, 1], conv_w_in[r], conv_w[r], conv_w_out[r])
        x = _ffn(x, mod, norm_g[i, 2], ffn_w_in, ffn_w_out, i, 1, sub=2, mixer_out=pending)
    return x
```

```python
import functools
import math

import jax
import jax.numpy as jnp
from jax import lax
from jax.experimental import pallas as pl
from jax.experimental.pallas import tpu as pltpu

F32 = jnp.float32
BF16 = jnp.bfloat16

NORM_EPS = 1e-6
N_SUBLAYERS = 3
N_ADA = 3 * N_SUBLAYERS
POOL_WINDOWS = (2, 4, 8, 16)
POOL_HALO = 32
FOX_HEADS = 16
FOX_HEAD_DIM = 64
S5_GROUP = 16
S5_STATE = 64
S5_CHUNK = 32
S5_MAX_LOG_RATE = 5.0
CONV_HALO = 8
LANES = 128
NEG_BIG = -1e30
LOG2E = math.log2(math.e)
AUG_LANES = 8
V_ROWS = FOX_HEAD_DIM + 16
FOX_TILE = 512

VMEM_LIMIT = 56 * 1024 * 1024


def _params(n_grid, vmem=VMEM_LIMIT):
    return pltpu.CompilerParams(dimension_semantics=("arbitrary",) * n_grid,
                                vmem_limit_bytes=vmem)


def _dot(a, b):
    return jnp.dot(a, b, preferred_element_type=F32)


def _adaln(x, gain, shift, scale):
    ms = jnp.mean(x * x, axis=-1, keepdims=True)
    return x * lax.rsqrt(ms + NORM_EPS) * (gain * (1.0 + scale)) + shift


def _mod_rows(mod_ref, sub):
    m = mod_ref[0]
    return m[3 * sub:3 * sub + 1], m[3 * sub + 1:3 * sub + 2], m[3 * sub + 2:3 * sub + 3]


def _const_spec(shape):
    nd = len(shape)
    return pl.BlockSpec(shape, lambda *_: (0,) * nd, pipeline_mode=pl.Buffered(1))


def _tile_specs(tm, d):
    x_spec = pl.BlockSpec((1, tm, d), lambda b, t: (b, t, 0))
    mod_spec = pl.BlockSpec((1, N_ADA, d), lambda b, t: (b, 0, 0))
    return x_spec, mod_spec


def _mod_kernel(c_ref, w_ref, b_ref, o_ref):
    c = c_ref[...]
    cond = (c * jax.nn.sigmoid(c)).astype(BF16)
    o_ref[0] = _dot(cond, w_ref[0].astype(BF16)) + b_ref[0]


def _modulation(c, ada_w, ada_b):
    depth, d, n = ada_w.shape
    b = c.shape[0]
    tn = 1024
    return pl.pallas_call(
        _mod_kernel,
        out_shape=jax.ShapeDtypeStruct((depth, b, n), F32),
        grid=(depth, n // tn),
        in_specs=[pl.BlockSpec((b, d), lambda l, j: (0, 0)),
                  pl.BlockSpec((1, d, tn), lambda l, j: (l, 0, j)),
                  pl.BlockSpec((1, 1, tn), lambda l, j: (l, 0, j))],
        out_specs=pl.BlockSpec((1, b, tn), lambda l, j: (l, 0, j)),
        compiler_params=_params(2),
    )(c, ada_w, ada_b.reshape(depth, 1, n))


def _ffn_kernel(sub, tf, mixer_out, *refs):
    if mixer_out:
        x_ref, mod_ref, g_ref, win_ref, wout_ref, att_ref, wo_ref, o_ref, h_sc, a_sc = refs
        x = x_ref[0] + (1.0 + _mod_rows(mod_ref, 1)[2]) * _dot(att_ref[0], wo_ref[...])
    else:
        x_ref, mod_ref, g_ref, win_ref, wout_ref, o_ref, h_sc, a_sc = refs
        x = x_ref[0]
    shift, scale, gate = _mod_rows(mod_ref, sub)
    h_sc[...] = _adaln(x, g_ref[...], shift, scale).astype(BF16)
    dff = a_sc.shape[1]
    for j in range(dff // tf):
        h = h_sc[...]
        g = _dot(h, win_ref[:, j * tf:(j + 1) * tf].astype(BF16))
        u = _dot(h, win_ref[:, dff + j * tf:dff + (j + 1) * tf].astype(BF16))
        a_sc[:, j * tf:(j + 1) * tf] = (g * jax.nn.sigmoid(g) * u).astype(BF16)
    y = _dot(a_sc[...], wout_ref[...].astype(BF16))
    o_ref[0] = x + (0.5 * (1.0 + gate)) * y


def _ffn(x, mod, gain, w_in_all, w_out_all, layer, half, sub, mixer_out=None, tm=512, tf=256):
    b, s, d = x.shape
    dff = w_out_all.shape[2]
    x_spec, mod_spec = _tile_specs(tm, d)

    def pick(rows, cols):
        return pl.BlockSpec((None, None, rows, cols), lambda b_, t: (layer, half, 0, 0),
                            pipeline_mode=pl.Buffered(1))

    operands = [x, mod, gain.reshape(1, d), w_in_all, w_out_all]
    in_specs = [x_spec, mod_spec, _const_spec((1, d)), pick(d, 2 * dff), pick(dff, d)]
    if mixer_out is not None:
        operands += list(mixer_out)
        in_specs += [x_spec, _const_spec((d, d))]
    return pl.pallas_call(
        functools.partial(_ffn_kernel, sub, tf, mixer_out is not None),
        out_shape=jax.ShapeDtypeStruct(x.shape, F32),
        grid=(b, s // tm),
        in_specs=in_specs,
        out_specs=x_spec,
        scratch_shapes=[pltpu.VMEM((tm, d), BF16), pltpu.VMEM((tm, dff), BF16)],
        compiler_params=_params(2),
    )(*operands)


def _pool_kernel(tm, x_ref, mod_ref, g_ref, w_ref, sc_ref, o_ref, h_sc, p_sc):
    t = pl.program_id(1)
    x = x_ref[0]
    d = x.shape[1]
    cg = d // len(POOL_WINDOWS)
    n = tm + POOL_HALO
    shift, scale, gate = _mod_rows(mod_ref, 1)

    @pl.when(t == 0)
    def _():
        h_sc[0:POOL_HALO, :] = jnp.zeros((POOL_HALO, d), F32)

    h_sc[POOL_HALO:, :] = _adaln(x, g_ref[...], shift, scale)
    pos = t * tm + lax.broadcasted_iota(jnp.int32, (tm, 1), 0)
    ys = []
    for gi, w in enumerate(POOL_WINDOWS):
        sl = slice(gi * cg, (gi + 1) * cg)
        cur = h_sc[POOL_HALO:, sl]

        def rows_of(buf):
            if buf is None:
                return lambda lo, hi: h_sc[lo:hi, sl]
            return lambda lo, hi: p_sc[buf, lo:hi, :]

        src, span, level = rows_of(None), 1, 0
        while 2 * span < w:
            level += 1
            lo = 8 * level
            p_sc[level % 2, lo:n, :] = src(lo, n) + src(lo - span, n - span)
            src, span = rows_of(level % 2), 2 * span
        acc = src(POOL_HALO, n) + src(POOL_HALO - span, n - span)
        cnt = jnp.minimum(pos + 1, w).astype(F32)
        pooled = acc / cnt - cur
        ys.append(_dot(pooled.astype(BF16), w_ref[gi]))
    y = jnp.concatenate(ys, axis=1) * sc_ref[...]
    h_sc[0:POOL_HALO, :] = h_sc[tm:tm + POOL_HALO, :]
    o_ref[0] = x + (1.0 + gate) * y


def _pool_layer(x, mod, gain, w_grp, scale, tm=512):
    b, s, d = x.shape
    ng, cg, _ = w_grp.shape
    x_spec, mod_spec = _tile_specs(tm, d)
    return pl.pallas_call(
        functools.partial(_pool_kernel, tm),
        out_shape=jax.ShapeDtypeStruct(x.shape, F32),
        grid=(b, s // tm),
        in_specs=[x_spec, mod_spec, _const_spec((1, d)),
                  _const_spec((ng, cg, cg)), _const_spec((1, d))],
        out_specs=x_spec,
        scratch_shapes=[pltpu.VMEM((tm + POOL_HALO, d), F32), pltpu.VMEM((2, tm + POOL_HALO, cg), F32)],
        compiler_params=_params(2),
    )(x, mod, gain.reshape(1, d), w_grp.astype(BF16), scale.reshape(1, d))


def _split3(v):
    p1 = v.astype(BF16)
    r1 = v - p1.astype(F32)
    p2 = r1.astype(BF16)
    p3 = (r1 - p2.astype(F32)).astype(BF16)
    return p1, p2, p3


def _foxproj_kernel(x_ref, mod_ref, g_ref, wq_ref, wk_ref, wv_ref, wf_ref, bf_ref, qg_ref, kg_ref,
                    tri_ref, place_ref, ones_ref,
                    q_ref, k_ref, v_ref, qa_ref, ka_ref, carry_sc):
    t = pl.program_id(1)
    x = x_ref[0]
    shift, scale, _ = _mod_rows(mod_ref, 1)
    h = _adaln(x, g_ref[...], shift, scale).astype(BF16)
    nt_dims = (((1,), (1,)), ((), ()))

    def head_normed(xt, gain_ref, hd):
        rows = slice(hd * FOX_HEAD_DIM, (hd + 1) * FOX_HEAD_DIM)
        xh = xt[rows, :]
        r = lax.rsqrt(jnp.mean(xh * xh, axis=0, keepdims=True) + NORM_EPS)
        return (xh * r * gain_ref[rows, :]).astype(BF16)

    qt = lax.dot_general(wq_ref[...], h, nt_dims, preferred_element_type=F32)
    kt = lax.dot_general(wk_ref[...], h, nt_dims, preferred_element_type=F32)
    for pair in range(FOX_HEADS // 2):
        lanes = slice(pair * LANES, (pair + 1) * LANES)
        q_ref[0, 0, lanes, :] = jnp.concatenate(
            [head_normed(qt, qg_ref, 2 * pair), head_normed(qt, qg_ref, 2 * pair + 1)], axis=0)
        k_ref[0, :, lanes] = jnp.concatenate(
            [head_normed(kt, kg_ref, 2 * pair), head_normed(kt, kg_ref, 2 * pair + 1)], axis=0).T
    v_ref[0, 0] = lax.dot_general(wv_ref[...], h, nt_dims, preferred_element_type=F32).astype(BF16)
    fl = _dot(h, wf_ref[...]) + bf_ref[...]
    logf = jnp.minimum(fl, 0.0) - jnp.log(1.0 + jnp.exp(-jnp.abs(fl)))

    @pl.when(t == 0)
    def _():
        carry_sc[...] = jnp.zeros_like(carry_sc)

    tri = tri_ref[...]
    cf = sum(_dot(tri, p) for p in _split3(logf)) + carry_sc[0:1, :]
    tm = cf.shape[0]
    carry_sc[0:1, :] = cf[tm - 1:tm, :]
    pieces = jnp.concatenate(_split3(cf * LOG2E), axis=1)
    aug = _dot(pieces, place_ref[...]) + ones_ref[...]
    qa_ref[0, 0] = aug[:, :LANES].T.astype(BF16)
    ka_ref[0] = aug[:, LANES:].astype(BF16)


def _fox_project(x, mod, gain, w_in, b_f, q_gain, k_gain, tm=FOX_TILE):
    b, s, d = x.shape
    x_spec, mod_spec = _tile_specs(tm, d)
    w = w_in.astype(BF16)
    wf = jnp.zeros((d, LANES), BF16).at[:, :FOX_HEADS].set(w[:, 3 * d:])
    bf = jnp.zeros((1, LANES), F32).at[0, :FOX_HEADS].set(b_f)
    qg = jnp.tile(q_gain, FOX_HEADS).reshape(d, 1) * (FOX_HEAD_DIM ** -0.5 * LOG2E)
    kg = jnp.tile(k_gain, FOX_HEADS).reshape(d, 1)
    tri = (jnp.arange(tm)[:, None] >= jnp.arange(tm)[None, :]).astype(BF16)
    hh, ii = jnp.meshgrid(jnp.arange(FOX_HEADS), jnp.arange(3), indexing="ij")
    place = jnp.zeros((3 * LANES, 2 * LANES), F32)
    place = place.at[ii * LANES + hh, AUG_LANES * hh + 3 + ii].set(1.0)
    place = place.at[ii * LANES + hh, LANES + AUG_LANES * hh + ii].set(-1.0)
    ones = jnp.zeros((1, 2 * LANES), F32)
    ones = ones.at[0, AUG_LANES * hh + ii].set(1.0).at[0, LANES + AUG_LANES * hh + 3 + ii].set(1.0)
    act = jax.ShapeDtypeStruct((b, s, d), BF16)
    aug = jax.ShapeDtypeStruct((b, s, LANES), BF16)
    act_t = jax.ShapeDtypeStruct((b, s // tm, d, tm), BF16)
    aug_t = jax.ShapeDtypeStruct((b, s // tm, LANES, tm), BF16)
    token_major = lambda width: pl.BlockSpec((1, tm, width), lambda b_, t: (b_, t, 0))
    channel_major = lambda rows: pl.BlockSpec((1, 1, rows, tm), lambda b_, t: (b_, t, 0, 0))
    return pl.pallas_call(
        _foxproj_kernel,
        out_shape=(act_t, act, act_t, aug_t, aug),
        grid=(b, s // tm),
        in_specs=[x_spec, mod_spec, _const_spec((1, d)),
                  _const_spec((d, d)), _const_spec((d, d)), _const_spec((d, d)),
                  _const_spec((d, LANES)), _const_spec((1, LANES)),
                  _const_spec((d, 1)), _const_spec((d, 1)), _const_spec((tm, tm)),
                  _const_spec((3 * LANES, 2 * LANES)), _const_spec((1, 2 * LANES))],
        out_specs=(channel_major(d), token_major(d), channel_major(d),
                   channel_major(LANES), token_major(LANES)),
        scratch_shapes=[pltpu.VMEM((8, LANES), F32)],
        compiler_params=_params(2),
    )(x, mod, gain.reshape(1, d), w[:, :d].T, w[:, d:2 * d].T, w[:, 2 * d:3 * d].T, wf, bf, qg, kg,
      tri, place.astype(BF16), ones)


def _foxattn_kernel(TK, TQ, q_ref, qa_ref, qn_ref, qan_ref, k_ref, ka_ref, vt_ref, o_ref,
                    sa_sc, sb_sc, qt_sc, qtn_sc, m_sc, acc_sc, diff_sc):
    first_step = (pl.program_id(0) == 0) & (pl.program_id(1) == 0) & (pl.program_id(2) == 0)
    pair = pl.program_id(1)
    qi = pl.program_id(2)
    chan = lax.broadcasted_iota(jnp.int32, (LANES, 1), 0)
    zero = jnp.zeros((LANES, TK), BF16)
    for hd in range(2):
        own = (chan < FOX_HEAD_DIM) == (hd == 0)
        bias_lo = AUG_LANES * (2 * pair + hd)
        own_bias = (chan >= bias_lo) & (chan < bias_lo + AUG_LANES)
        for src, src_a, dst in ((q_ref, qa_ref, qt_sc), (qn_ref, qan_ref, qtn_sc)):
            for i in range(TQ // TK):
                cols = slice(i * TK, (i + 1) * TK)
                dst[hd, 0:LANES, cols] = jnp.where(own, src[0, i], zero)
                dst[hd, LANES:2 * LANES, cols] = jnp.where(own_bias, src_a[0, i], zero)
        m_sc[hd] = jnp.full((8, TQ), NEG_BIG, F32)
        acc_sc[hd] = jnp.zeros((V_ROWS, TQ), F32)

    @pl.when(first_step)
    def _():
        diff_sc[...] = (lax.broadcasted_iota(jnp.int32, (TK, TQ), 0)
                        - lax.broadcasted_iota(jnp.int32, (TK, TQ), 1))

    every = slice(0, TQ)
    early, late = slice(0, TK), slice(TK, TQ)

    def scores(j, s_sc, qs=every, queries=qt_sc):
        rows = pl.ds(pl.multiple_of(j * TK, TK), TK)
        kj = jnp.concatenate([k_ref[0, rows, :], ka_ref[0, rows, :]], axis=1)
        for hd in range(2):
            s_sc[hd, :, qs] = _dot(kj, queries[hd, :, qs])

    def absorb(j, s_sc, masked, qs=every, ks=slice(0, TK)):
        vt = vt_ref[0, j, :, ks]
        ones = jnp.ones((V_ROWS - FOX_HEAD_DIM, ks.stop - ks.start), BF16)
        if masked:
            visible = diff_sc[ks, qs] <= qi * TQ - j * TK
        for hd in range(2):
            s = s_sc[hd, ks, qs]
            if masked:
                s = jnp.where(visible, s, NEG_BIG)
            m_prev = m_sc[hd, :, qs]
            m_new = jnp.maximum(m_prev, jnp.max(s, axis=0, keepdims=True))
            p = jnp.exp2(s - m_new[0:1, :]).astype(BF16)
            v_aug = jnp.concatenate([vt[hd * FOX_HEAD_DIM:(hd + 1) * FOX_HEAD_DIM, :], ones], axis=0)
            alpha = jnp.exp2(m_prev - m_new)[0:1, :]
            acc_sc[hd, :, qs] = alpha * acc_sc[hd, :, qs] + _dot(v_aug, p)
            m_sc[hd, :, qs] = m_new

    @pl.when(qi == 0)
    def _():
        scores(0, sa_sc)

    def two_tiles(jj):
        j = 2 * jj
        scores(j + 1, sb_sc)
        absorb(j, sa_sc, False)
        scores(j + 2, sa_sc)
        absorb(j + 1, sb_sc, False)

    def four_tiles(jjjj, carry):
        two_tiles(2 * jjjj)
        two_tiles(2 * jjjj + 1)
        return carry

    lax.fori_loop(0, qi // 2, four_tiles, 0)

    @pl.when(qi % 2 == 1)
    def _():
        two_tiles(qi - 1)
    scores(2 * qi + 1, sb_sc, late)
    half = TK // 2
    absorb(2 * qi, sa_sc, True, slice(0, half), slice(0, half))
    absorb(2 * qi, sa_sc, True, slice(half, TK))
    scores(0, sa_sc, early, qtn_sc)
    absorb(2 * qi, sa_sc, False, late)
    scores(0, sa_sc, late, qtn_sc)
    absorb(2 * qi + 1, sb_sc, True, slice(TK, TK + half), slice(0, half))
    absorb(2 * qi + 1, sb_sc, True, slice(TK + half, TQ))

    out_t = jnp.concatenate(
        [acc_sc[hd, 0:FOX_HEAD_DIM, :] / acc_sc[hd, FOX_HEAD_DIM:FOX_HEAD_DIM + 1, :] for hd in range(2)],
        axis=0)
    o_ref[0] = out_t.T.astype(o_ref.dtype)


def _fox_attend(qt, qat, k, ka, vt, tk=FOX_TILE):
    b, s, d = k.shape
    pairs = d // LANES
    tq = 2 * tk
    nq = s // tq
    assert qt.shape == vt.shape == (b, s // tk, d, tk)
    return pl.pallas_call(
        functools.partial(_foxattn_kernel, tk, tq),
        out_shape=jax.ShapeDtypeStruct((b, s, d), BF16),
        grid=(b, pairs, s // tq),
        in_specs=[pl.BlockSpec((1, tq // tk, LANES, tk), lambda b_, p, i: (b_, i, p, 0)),
                  pl.BlockSpec((1, tq // tk, LANES, tk), lambda b_, p, i: (b_, i, 0, 0)),
                  pl.BlockSpec((1, tq // tk, LANES, tk), lambda b_, p, i: (b_, jnp.minimum(i + 1, nq - 1), p, 0)),
                  pl.BlockSpec((1, tq // tk, LANES, tk), lambda b_, p, i: (b_, jnp.minimum(i + 1, nq - 1), 0, 0)),
                  pl.BlockSpec((1, s, LANES), lambda b_, p, i: (b_, 0, p)),
                  pl.BlockSpec((1, s, LANES), lambda b_, p, i: (b_, 0, 0)),
                  pl.BlockSpec((1, s // tk, LANES, tk), lambda b_, p, i: (b_, 0, p, 0))],
        out_specs=pl.BlockSpec((1, tq, LANES), lambda b_, p, i: (b_, i, p)),
        scratch_shapes=[pltpu.VMEM((2, tk, tq), F32), pltpu.VMEM((2, tk, tq), F32),
                        pltpu.VMEM((2, 2 * LANES, tq), BF16), pltpu.VMEM((2, 2 * LANES, tq), BF16),
                        pltpu.VMEM((2, 8, tq), F32), pltpu.VMEM((2, V_ROWS, tq), F32),
                        pltpu.VMEM((tk, tq), jnp.int32)],
        compiler_params=_params(3),
    )(qt, qat, qt, qat, k, ka, vt)


def _cmul(a, t):
    w = a.shape[1] // 2
    ar, ai, tr, ti = a[:, :w], a[:, w:], t[:, :w], t[:, w:]
    return jnp.concatenate([ar * tr - ai * ti, ar * ti + ai * tr], axis=1)


def _s5_kernel(tm, x_ref, mod_ref, g_ref, bm_ref, cm_ref, e_ref, f_ref, a_ref, tri_ref, dskip_ref,
               wglu_ref, o_ref, u_sc, y_sc, s_sc):
    t = pl.program_id(1)
    x = x_ref[0]
    shift, scale, gate = _mod_rows(mod_ref, 1)
    u_sc[...] = _adaln(x, g_ref[...], shift, scale)
    L = S5_CHUNK
    n_blocks = bm_ref.shape[0]

    @pl.when(t == 0)
    def _():
        s_sc[...] = jnp.zeros_like(s_sc)

    tri = tri_ref[...]
    chunks = [slice(c * L, (c + 1) * L) for c in range(tm // L)]
    for kb in range(n_blocks):
        cols = slice(kb * LANES, (kb + 1) * LANES)
        bu = _dot(u_sc[:, cols].astype(BF16), bm_ref[kb]).astype(BF16)
        sums = [_dot(tri, _cmul(bu[rows], e_ref[kb])) for rows in chunks]
        carried = [s_sc[kb:kb + 1, :]]
        for p in sums:
            carried.append(_cmul(p[L - 1:L, :] + carried[-1], a_ref[kb]))
        s_sc[kb:kb + 1, :] = carried[-1]
        xs = jnp.concatenate([_cmul((p + s).astype(BF16), f_ref[kb]) for p, s in zip(sums, carried)],
                             axis=0)
        y_sc[:, cols] = _dot(xs, cm_ref[kb])
    y = y_sc[...] + dskip_ref[...] * u_sc[...]
    g = jax.nn.gelu(y, approximate=True)
    out = g * jax.nn.sigmoid(_dot(g.astype(BF16), wglu_ref[...]))
    o_ref[0] = x + (1.0 + gate) * out


def _s5_tables(lam_re, lam_im, log_dt, b_re, b_im, c_re, c_im):
    g_, n_ = lam_re.shape
    i_ = b_re.shape[2]
    gl = LANES // i_
    nb = g_ // gl
    dt = jnp.exp(log_dt)[:, None]
    ar, ai = lam_re, lam_im
    mag = jnp.exp(ar * dt)
    lb_re, lb_im = mag * jnp.cos(ai * dt), mag * jnp.sin(ai * dt)
    den = ar * ar + ai * ai
    nr, ni = lb_re - 1.0, lb_im
    k_re = (nr * ar + ni * ai) / den
    k_im = (ni * ar - nr * ai) / den
    bb_re = k_re[..., None] * b_re - k_im[..., None] * b_im
    bb_im = k_re[..., None] * b_im + k_im[..., None] * b_re
    eye = jnp.eye(gl, dtype=F32)
    bb = jnp.stack([bb_re, bb_im]).reshape(2, nb, gl, n_, i_)
    bm = jnp.einsum('gh,pkgni->kgiphn', eye, bb).reshape(nb, gl * i_, 2 * gl * n_)
    cc = jnp.stack([c_re, -c_im]).reshape(2, nb, gl, i_, n_)
    cm = jnp.einsum('gh,pkgin->kpgnhi', eye, cc).reshape(nb, 2 * gl * n_, gl * i_)

    def lbar_pow(p, max_log_rate=None):
        log_rate = ar * dt
        if max_log_rate is not None:
            log_rate = jnp.clip(log_rate, -max_log_rate, max_log_rate)
        e = jnp.exp(p[:, None, None] * log_rate[None])
        th = p[:, None, None] * (ai * dt)[None]
        t = jnp.stack([e * jnp.cos(th), e * jnp.sin(th)], axis=1)
        return t.reshape(-1, 2, nb, gl, n_).transpose(2, 0, 1, 3, 4).reshape(nb, -1, 2 * gl * n_)

    mid = S5_CHUNK // 2
    j = jnp.arange(S5_CHUNK, dtype=F32) - mid
    return (bm.astype(BF16), cm.astype(BF16),
            lbar_pow(-j, S5_MAX_LOG_RATE).astype(BF16), lbar_pow(j, S5_MAX_LOG_RATE).astype(BF16),
            lbar_pow(jnp.full((1,), float(S5_CHUNK), F32)))


def _s5_layer(x, mod, gain, lam_re, lam_im, log_dt, b_re, b_im, c_re, c_im, d_skip, w_glu, tm=512):
    b, s, d = x.shape
    bm, cm, e_tab, f_tab, a_tab = _s5_tables(lam_re, lam_im, log_dt, b_re, b_im, c_re, c_im)
    nb, _, sw = bm.shape
    L = S5_CHUNK
    tri = (jnp.arange(L)[:, None] >= jnp.arange(L)[None, :]).astype(BF16)
    x_spec, mod_spec = _tile_specs(tm, d)
    return pl.pallas_call(
        functools.partial(_s5_kernel, tm),
        out_shape=jax.ShapeDtypeStruct(x.shape, F32),
        grid=(b, s // tm),
        in_specs=[x_spec, mod_spec, _const_spec((1, d)),
                  _const_spec((nb, LANES, sw)), _const_spec((nb, sw, LANES)),
                  _const_spec((nb, L, sw)), _const_spec((nb, L, sw)), _const_spec((nb, 1, sw)),
                  _const_spec((L, L)), _const_spec((1, d)), _const_spec((d, d))],
        out_specs=x_spec,
        scratch_shapes=[pltpu.VMEM((tm, d), F32), pltpu.VMEM((tm, d), F32), pltpu.VMEM((nb, sw), F32)],
        compiler_params=_params(2),
    )(x, mod, gain.reshape(1, d), bm, cm, e_tab, f_tab, a_tab, tri, d_skip.reshape(1, d),
      w_glu.astype(BF16))


def _conv_kernel(tm, x_ref, mod_ref, g_ref, win_ref, cw_ref, wout_ref, o_ref, cz_sc):
    t = pl.program_id(1)
    x = x_ref[0]
    d = x.shape[1]
    shift, scale, gate = _mod_rows(mod_ref, 1)
    h = _adaln(x, g_ref[...], shift, scale).astype(BF16)

    @pl.when(t == 0)
    def _():
        cz_sc[0:CONV_HALO, :] = jnp.zeros((CONV_HALO, d), F32)

    cz_sc[CONV_HALO:, :] = _dot(h, win_ref[:, d:2 * d]) * _dot(h, win_ref[:, 2 * d:3 * d])
    cw = cw_ref[...]
    conv = (cw[0:1] * cz_sc[CONV_HALO - 2:CONV_HALO - 2 + tm, :]
            + cw[1:2] * cz_sc[CONV_HALO - 1:CONV_HALO - 1 + tm, :]
            + cw[2:3] * cz_sc[CONV_HALO:, :])
    cz_sc[0:CONV_HALO, :] = cz_sc[tm:tm + CONV_HALO, :]
    gated = (_dot(h, win_ref[:, 0:d]) * conv).astype(BF16)
    o_ref[0] = x + (1.0 + gate) * _dot(gated, wout_ref[...])


def _conv_layer(x, mod, gain, w_in, conv_w, w_out, tm=512):
    b, s, d = x.shape
    kw = conv_w.shape[0]
    x_spec, mod_spec = _tile_specs(tm, d)
    return pl.pallas_call(
        functools.partial(_conv_kernel, tm),
        out_shape=jax.ShapeDtypeStruct(x.shape, F32),
        grid=(b, s // tm),
        in_specs=[x_spec, mod_spec, _const_spec((1, d)),
                  _const_spec((d, 3 * d)), _const_spec((kw, d)), _const_spec((d, d))],
        out_specs=x_spec,
        scratch_shapes=[pltpu.VMEM((tm + CONV_HALO, d), F32)],
        compiler_params=_params(2),
    )(x, mod, gain.reshape(1, d), w_in.astype(BF16), conv_w.reshape(kw, d), w_out.astype(BF16))


def kernel(x, c, ada_w, ada_b, norm_g, ffn_w_in, ffn_w_out, pool_w, pool_scale, fox_w_in, fox_b_f, fox_q_gain, fox_k_gain, fox_w_o, s5_lam_re, s5_lam_im, s5_log_dt, s5_b_re, s5_b_im, s5_c_re, s5_c_im, s5_d, s5_w_glu, conv_w_in, conv_w, conv_w_out):
    b, s, d = x.shape
    depth = ada_w.shape[0]
    mod_all = _modulation(c, ada_w, ada_b).reshape(depth, b, N_ADA, d)
    n_mixers = 4
    for i in range(depth):
        mod = mod_all[i]
        x = _ffn(x, mod, norm_g[i, 0], ffn_w_in, ffn_w_out, i, 0, sub=0)
        m, r = i % n_mixers, i // n_mixers
        pending = None
        if m == 0:
            x = _pool_layer(x, mod, norm_g[i, 1], pool_w[r], pool_scale[r])
        elif m == 1:
            qt, k, vt, qat, ka = _fox_project(x, mod, norm_g[i, 1], fox_w_in[r], fox_b_f[r],
                                              fox_q_gain[r], fox_k_gain[r])
            pending = (_fox_attend(qt, qat, k, ka, vt), fox_w_o[r].astype(BF16))
        elif m == 2:
            x = _s5_layer(x, mod, norm_g[i, 1], s5_lam_re[r], s5_lam_im[r], s5_log_dt[r],
                          s5_b_re[r], s5_b_im[r], s5_c_re[r], s5_c_im[r], s5_d[r], s5_w_glu[r])
        else:
            x = _conv_layer(x, mod, norm_g[i, 1], conv_w_in[r], conv_w[r], conv_w_out[r])
        x = _ffn(x, mod, norm_g[i, 2], ffn_w_in, ffn_w_out, i, 1, sub=2, mixer_out=pending)
    return x
```

```python
import functools
import math

import jax
import jax.numpy as jnp
from jax import lax
from jax.experimental import pallas as pl
from jax.experimental.pallas import tpu as pltpu

F32 = jnp.float32
BF16 = jnp.bfloat16

NORM_EPS = 1e-6
N_SUBLAYERS = 3
N_ADA = 3 * N_SUBLAYERS
POOL_WINDOWS = (2, 4, 8, 16)
POOL_HALO = 32
FOX_HEADS = 16
FOX_HEAD_DIM = 64
S5_GROUP = 16
S5_STATE = 64
S5_CHUNK = 32
S5_MAX_LOG_RATE = 5.0
CONV_HALO = 8
ROW_BLOCK = 512
LANES = 128
NEG_BIG = -1e30
LOG2E = math.log2(math.e)
AUG_LANES = 8
V_ROWS = FOX_HEAD_DIM + 16
FOX_TILE = 512

VMEM_LIMIT = 56 * 1024 * 1024
FFN_VMEM_LIMIT = 62 * 1024 * 1024


def _params(n_grid, vmem=VMEM_LIMIT):
    return pltpu.CompilerParams(dimension_semantics=("arbitrary",) * n_grid,
                                vmem_limit_bytes=vmem)


def _dot(a, b):
    return jnp.dot(a, b, preferred_element_type=F32)


def _adaln(x, gain, shift, scale):
    ms = jnp.mean(x * x, axis=-1, keepdims=True)
    return x * lax.rsqrt(ms + NORM_EPS) * (gain * (1.0 + scale)) + shift


def _mod_rows(mod_ref, sub):
    m = mod_ref[0]
    return m[3 * sub:3 * sub + 1], m[3 * sub + 1:3 * sub + 2], m[3 * sub + 2:3 * sub + 3]


def _const_spec(shape):
    nd = len(shape)
    return pl.BlockSpec(shape, lambda *_: (0,) * nd, pipeline_mode=pl.Buffered(1))


def _tile_specs(tm, d):
    x_spec = pl.BlockSpec((1, tm, d), lambda b, t: (b, t, 0))
    mod_spec = pl.BlockSpec((1, N_ADA, d), lambda b, t: (b, 0, 0))
    return x_spec, mod_spec


def _mod_kernel(c_ref, w_ref, b_ref, o_ref):
    c = c_ref[...]
    cond = (c * jax.nn.sigmoid(c)).astype(BF16)
    o_ref[0] = _dot(cond, w_ref[0].astype(BF16)) + b_ref[0]


def _modulation(c, ada_w, ada_b):
    depth, d, n = ada_w.shape
    b = c.shape[0]
    tn = 1024
    return pl.pallas_call(
        _mod_kernel,
        out_shape=jax.ShapeDtypeStruct((depth, b, n), F32),
        grid=(depth, n // tn),
        in_specs=[pl.BlockSpec((b, d), lambda l, j: (0, 0)),
                  pl.BlockSpec((1, d, tn), lambda l, j: (l, 0, j)),
                  pl.BlockSpec((1, 1, tn), lambda l, j: (l, 0, j))],
        out_specs=pl.BlockSpec((1, b, tn), lambda l, j: (l, 0, j)),
        compiler_params=_params(2),
    )(c, ada_w, ada_b.reshape(depth, 1, n))


def _ffn_kernel(sub, tf, mixer_out, *refs):
    if mixer_out:
        x_ref, mod_ref, g_ref, win_ref, wout_ref, att_ref, wo_ref, o_ref, h_sc, a_sc = refs
    else:
        x_ref, mod_ref, g_ref, win_ref, wout_ref, o_ref, h_sc, a_sc = refs
    shift, scale, gate = _mod_rows(mod_ref, sub)
    dff = a_sc.shape[1]
    for r0 in range(0, x_ref.shape[1], ROW_BLOCK):
        rows = slice(r0, r0 + ROW_BLOCK)
        x = x_ref[0, rows, :]
        if mixer_out:
            x = x + (1.0 + _mod_rows(mod_ref, 1)[2]) * _dot(att_ref[0, rows, :], wo_ref[...])
        h_sc[rows, :] = _adaln(x, g_ref[...], shift, scale).astype(BF16)
        for j in range(dff // tf):
            h = h_sc[rows, :]
            g = _dot(h, win_ref[:, j * tf:(j + 1) * tf].astype(BF16))
            u = _dot(h, win_ref[:, dff + j * tf:dff + (j + 1) * tf].astype(BF16))
            a_sc[rows, j * tf:(j + 1) * tf] = (g * jax.nn.sigmoid(g) * u).astype(BF16)
        y = _dot(a_sc[rows, :], wout_ref[...].astype(BF16))
        o_ref[0, rows, :] = x + (0.5 * (1.0 + gate)) * y


def _ffn(x, mod, gain, w_in_all, w_out_all, layer, half, sub, mixer_out=None, tm=2 * ROW_BLOCK,
         tf=256):
    b, s, d = x.shape
    dff = w_out_all.shape[2]
    x_spec, mod_spec = _tile_specs(tm, d)

    def pick(rows, cols):
        return pl.BlockSpec((None, None, rows, cols), lambda b_, t: (layer, half, 0, 0),
                            pipeline_mode=pl.Buffered(1))

    operands = [x, mod, gain.reshape(1, d), w_in_all, w_out_all]
    in_specs = [x_spec, mod_spec, _const_spec((1, d)), pick(d, 2 * dff), pick(dff, d)]
    if mixer_out is not None:
        operands += list(mixer_out)
        in_specs += [x_spec, _const_spec((d, d))]
    return pl.pallas_call(
        functools.partial(_ffn_kernel, sub, tf, mixer_out is not None),
        out_shape=jax.ShapeDtypeStruct(x.shape, F32),
        grid=(b, s // tm),
        in_specs=in_specs,
        out_specs=x_spec,
        scratch_shapes=[pltpu.VMEM((tm, d), BF16), pltpu.VMEM((tm, dff), BF16)],
        compiler_params=_params(2, FFN_VMEM_LIMIT),
    )(*operands)


def _pool_kernel(tm, x_ref, mod_ref, g_ref, w_ref, sc_ref, o_ref, h_sc, p_sc):
    t = pl.program_id(1)
    x = x_ref[0]
    d = x.shape[1]
    cg = d // len(POOL_WINDOWS)
    n = tm + POOL_HALO
    shift, scale, gate = _mod_rows(mod_ref, 1)

    @pl.when(t == 0)
    def _():
        h_sc[0:POOL_HALO, :] = jnp.zeros((POOL_HALO, d), F32)

    h_sc[POOL_HALO:, :] = _adaln(x, g_ref[...], shift, scale)
    pos = t * tm + lax.broadcasted_iota(jnp.int32, (tm, 1), 0)
    ys = []
    for gi, w in enumerate(POOL_WINDOWS):
        sl = slice(gi * cg, (gi + 1) * cg)
        cur = h_sc[POOL_HALO:, sl]

        def rows_of(buf):
            if buf is None:
                return lambda lo, hi: h_sc[lo:hi, sl]
            return lambda lo, hi: p_sc[buf, lo:hi, :]

        src, span, level = rows_of(None), 1, 0
        while 2 * span < w:
            level += 1
            lo = 8 * level
            p_sc[level % 2, lo:n, :] = src(lo, n) + src(lo - span, n - span)
            src, span = rows_of(level % 2), 2 * span
        acc = src(POOL_HALO, n) + src(POOL_HALO - span, n - span)
        cnt = jnp.minimum(pos + 1, w).astype(F32)
        pooled = acc / cnt - cur
        ys.append(_dot(pooled.astype(BF16), w_ref[gi]))
    y = jnp.concatenate(ys, axis=1) * sc_ref[...]
    h_sc[0:POOL_HALO, :] = h_sc[tm:tm + POOL_HALO, :]
    o_ref[0] = x + (1.0 + gate) * y


def _pool_layer(x, mod, gain, w_grp, scale, tm=512):
    b, s, d = x.shape
    ng, cg, _ = w_grp.shape
    x_spec, mod_spec = _tile_specs(tm, d)
    return pl.pallas_call(
        functools.partial(_pool_kernel, tm),
        out_shape=jax.ShapeDtypeStruct(x.shape, F32),
        grid=(b, s // tm),
        in_specs=[x_spec, mod_spec, _const_spec((1, d)),
                  _const_spec((ng, cg, cg)), _const_spec((1, d))],
        out_specs=x_spec,
        scratch_shapes=[pltpu.VMEM((tm + POOL_HALO, d), F32), pltpu.VMEM((2, tm + POOL_HALO, cg), F32)],
        compiler_params=_params(2),
    )(x, mod, gain.reshape(1, d), w_grp.astype(BF16), scale.reshape(1, d))


def _split3(v):
    p1 = v.astype(BF16)
    r1 = v - p1.astype(F32)
    p2 = r1.astype(BF16)
    p3 = (r1 - p2.astype(F32)).astype(BF16)
    return p1, p2, p3


def _foxproj_kernel(x_ref, mod_ref, g_ref, wq_ref, wk_ref, wv_ref, wf_ref, bf_ref, qg_ref, kg_ref,
                    tri_ref, place_ref, ones_ref,
                    q_ref, k_ref, v_ref, qa_ref, ka_ref, carry_sc):
    t = pl.program_id(1)
    shift, scale, _ = _mod_rows(mod_ref, 1)
    tm = tri_ref.shape[0]
    nt_dims = (((1,), (1,)), ((), ()))

    def head_normed(xt, gain_ref, hd):
        rows = slice(hd * FOX_HEAD_DIM, (hd + 1) * FOX_HEAD_DIM)
        xh = xt[rows, :]
        r = lax.rsqrt(jnp.mean(xh * xh, axis=0, keepdims=True) + NORM_EPS)
        return (xh * r * gain_ref[rows, :]).astype(BF16)

    @pl.when(t == 0)
    def _():
        carry_sc[...] = jnp.zeros_like(carry_sc)

    tri = tri_ref[...]
    for i in range(x_ref.shape[1] // tm):
        rows = slice(i * tm, (i + 1) * tm)
        h = _adaln(x_ref[0, rows, :], g_ref[...], shift, scale).astype(BF16)
        qt = lax.dot_general(wq_ref[...], h, nt_dims, preferred_element_type=F32)
        kt = lax.dot_general(wk_ref[...], h, nt_dims, preferred_element_type=F32)
        for pair in range(FOX_HEADS // 2):
            lanes = slice(pair * LANES, (pair + 1) * LANES)
            q_ref[0, i, lanes, :] = jnp.concatenate(
                [head_normed(qt, qg_ref, 2 * pair), head_normed(qt, qg_ref, 2 * pair + 1)], axis=0)
            k_ref[0, rows, lanes] = jnp.concatenate(
                [head_normed(kt, kg_ref, 2 * pair), head_normed(kt, kg_ref, 2 * pair + 1)], axis=0).T
        v_ref[0, i] = lax.dot_general(wv_ref[...], h, nt_dims, preferred_element_type=F32).astype(BF16)
        fl = _dot(h, wf_ref[...]) + bf_ref[...]
        logf = jnp.minimum(fl, 0.0) - jnp.log(1.0 + jnp.exp(-jnp.abs(fl)))
        cf = sum(_dot(tri, p) for p in _split3(logf)) + carry_sc[0:1, :]
        carry_sc[0:1, :] = cf[tm - 1:tm, :]
        pieces = jnp.concatenate(_split3(cf * LOG2E), axis=1)
        aug = _dot(pieces, place_ref[...]) + ones_ref[...]
        qa_ref[0, i] = aug[:, :LANES].T.astype(BF16)
        ka_ref[0, rows, :] = aug[:, LANES:].astype(BF16)


def _fox_project(x, mod, gain, w_in, b_f, q_gain, k_gain, tm=FOX_TILE):
    b, s, d = x.shape
    x_spec, mod_spec = _tile_specs(tm, d)
    w = w_in.astype(BF16)
    wf = jnp.zeros((d, LANES), BF16).at[:, :FOX_HEADS].set(w[:, 3 * d:])
    bf = jnp.zeros((1, LANES), F32).at[0, :FOX_HEADS].set(b_f)
    qg = jnp.tile(q_gain, FOX_HEADS).reshape(d, 1) * (FOX_HEAD_DIM ** -0.5 * LOG2E)
    kg = jnp.tile(k_gain, FOX_HEADS).reshape(d, 1)
    tri = (jnp.arange(tm)[:, None] >= jnp.arange(tm)[None, :]).astype(BF16)
    hh, ii = jnp.meshgrid(jnp.arange(FOX_HEADS), jnp.arange(3), indexing="ij")
    place = jnp.zeros((3 * LANES, 2 * LANES), F32)
    place = place.at[ii * LANES + hh, AUG_LANES * hh + 3 + ii].set(1.0)
    place = place.at[ii * LANES + hh, LANES + AUG_LANES * hh + ii].set(-1.0)
    ones = jnp.zeros((1, 2 * LANES), F32)
    ones = ones.at[0, AUG_LANES * hh + ii].set(1.0).at[0, LANES + AUG_LANES * hh + 3 + ii].set(1.0)
    act = jax.ShapeDtypeStruct((b, s, d), BF16)
    aug = jax.ShapeDtypeStruct((b, s, LANES), BF16)
    act_t = jax.ShapeDtypeStruct((b, s // tm, d, tm), BF16)
    aug_t = jax.ShapeDtypeStruct((b, s // tm, LANES, tm), BF16)
    per_step = 2
    token_major = lambda width: pl.BlockSpec((1, per_step * tm, width), lambda b_, t: (b_, t, 0))
    channel_major = lambda rows: pl.BlockSpec((1, per_step, rows, tm), lambda b_, t: (b_, t, 0, 0))
    return pl.pallas_call(
        _foxproj_kernel,
        out_shape=(act_t, act, act_t, aug_t, aug),
        grid=(b, s // (per_step * tm)),
        in_specs=[token_major(d), mod_spec, _const_spec((1, d)),
                  _const_spec((d, d)), _const_spec((d, d)), _const_spec((d, d)),
                  _const_spec((d, LANES)), _const_spec((1, LANES)),
                  _const_spec((d, 1)), _const_spec((d, 1)), _const_spec((tm, tm)),
                  _const_spec((3 * LANES, 2 * LANES)), _const_spec((1, 2 * LANES))],
        out_specs=(channel_major(d), token_major(d), channel_major(d),
                   channel_major(LANES), token_major(LANES)),
        scratch_shapes=[pltpu.VMEM((8, LANES), F32)],
        compiler_params=_params(2),
    )(x, mod, gain.reshape(1, d), w[:, :d].T, w[:, d:2 * d].T, w[:, 2 * d:3 * d].T, wf, bf, qg, kg,
      tri, place.astype(BF16), ones)


def _foxattn_kernel(TK, TQ, q_ref, qa_ref, qn_ref, qan_ref, k_ref, ka_ref, vt_ref, o_ref,
                    sa_sc, sb_sc, qt_sc, qtn_sc, m_sc, acc_sc, diff_sc):
    first_step = (pl.program_id(0) == 0) & (pl.program_id(1) == 0) & (pl.program_id(2) == 0)
    pair = pl.program_id(1)
    qi = pl.program_id(2)
    chan = lax.broadcasted_iota(jnp.int32, (LANES, 1), 0)
    zero = jnp.zeros((LANES, TK), BF16)
    for hd in range(2):
        own = (chan < FOX_HEAD_DIM) == (hd == 0)
        bias_lo = AUG_LANES * (2 * pair + hd)
        own_bias = (chan >= bias_lo) & (chan < bias_lo + AUG_LANES)
        for src, src_a, dst in ((q_ref, qa_ref, qt_sc), (qn_ref, qan_ref, qtn_sc)):
            for i in range(TQ // TK):
                cols = slice(i * TK, (i + 1) * TK)
                dst[hd, 0:LANES, cols] = jnp.where(own, src[0, i], zero)
                dst[hd, LANES:2 * LANES, cols] = jnp.where(own_bias, src_a[0, i], zero)
        m_sc[hd] = jnp.full((8, TQ), NEG_BIG, F32)
        acc_sc[hd] = jnp.zeros((V_ROWS, TQ), F32)

    @pl.when(first_step)
    def _():
        diff_sc[...] = (lax.broadcasted_iota(jnp.int32, (TK, TQ), 0)
                        - lax.broadcasted_iota(jnp.int32, (TK, TQ), 1))

    every = slice(0, TQ)
    early, late = slice(0, TK), slice(TK, TQ)

    def scores(j, s_sc, qs=every, queries=qt_sc):
        rows = pl.ds(pl.multiple_of(j * TK, TK), TK)
        kj = jnp.concatenate([k_ref[0, rows, :], ka_ref[0, rows, :]], axis=1)
        for hd in range(2):
            s_sc[hd, :, qs] = _dot(kj, queries[hd, :, qs])

    def absorb(j, s_sc, masked, qs=every, ks=slice(0, TK)):
        vt = vt_ref[0, j, :, ks]
        ones = jnp.ones((V_ROWS - FOX_HEAD_DIM, ks.stop - ks.start), BF16)
        if masked:
            visible = diff_sc[ks, qs] <= qi * TQ - j * TK
        for hd in range(2):
            s = s_sc[hd, ks, qs]
            if masked:
                s = jnp.where(visible, s, NEG_BIG)
            m_prev = m_sc[hd, :, qs]
            m_new = jnp.maximum(m_prev, jnp.max(s, axis=0, keepdims=True))
            p = jnp.exp2(s - m_new[0:1, :]).astype(BF16)
            v_aug = jnp.concatenate([vt[hd * FOX_HEAD_DIM:(hd + 1) * FOX_HEAD_DIM, :], ones], axis=0)
            alpha = jnp.exp2(m_prev - m_new)[0:1, :]
            acc_sc[hd, :, qs] = alpha * acc_sc[hd, :, qs] + _dot(v_aug, p)
            m_sc[hd, :, qs] = m_new

    @pl.when(qi == 0)
    def _():
        scores(0, sa_sc)

    def two_tiles(jj):
        j = 2 * jj
        scores(j + 1, sb_sc)
        absorb(j, sa_sc, False)
        scores(j + 2, sa_sc)
        absorb(j + 1, sb_sc, False)

    def four_tiles(jjjj, carry):
        two_tiles(2 * jjjj)
        two_tiles(2 * jjjj + 1)
        return carry

    lax.fori_loop(0, qi // 2, four_tiles, 0)

    @pl.when(qi % 2 == 1)
    def _():
        two_tiles(qi - 1)
    scores(2 * qi + 1, sb_sc, late)
    half = TK // 2
    absorb(2 * qi, sa_sc, True, slice(0, half), slice(0, half))
    absorb(2 * qi, sa_sc, True, slice(half, TK))
    scores(0, sa_sc, early, qtn_sc)
    absorb(2 * qi, sa_sc, False, late)
    scores(0, sa_sc, late, qtn_sc)
    absorb(2 * qi + 1, sb_sc, True, slice(TK, TK + half), slice(0, half))
    absorb(2 * qi + 1, sb_sc, True, slice(TK + half, TQ))

    out_t = jnp.concatenate(
        [acc_sc[hd, 0:FOX_HEAD_DIM, :] / acc_sc[hd, FOX_HEAD_DIM:FOX_HEAD_DIM + 1, :] for hd in range(2)],
        axis=0)
    o_ref[0] = out_t.T.astype(o_ref.dtype)


def _fox_attend(qt, qat, k, ka, vt, tk=FOX_TILE):
    b, s, d = k.shape
    pairs = d // LANES
    tq = 2 * tk
    nq = s // tq
    assert qt.shape == vt.shape == (b, s // tk, d, tk)
    return pl.pallas_call(
        functools.partial(_foxattn_kernel, tk, tq),
        out_shape=jax.ShapeDtypeStruct((b, s, d), BF16),
        grid=(b, pairs, s // tq),
        in_specs=[pl.BlockSpec((1, tq // tk, LANES, tk), lambda b_, p, i: (b_, i, p, 0)),
                  pl.BlockSpec((1, tq // tk, LANES, tk), lambda b_, p, i: (b_, i, 0, 0)),
                  pl.BlockSpec((1, tq // tk, LANES, tk), lambda b_, p, i: (b_, jnp.minimum(i + 1, nq - 1), p, 0)),
                  pl.BlockSpec((1, tq // tk, LANES, tk), lambda b_, p, i: (b_, jnp.minimum(i + 1, nq - 1), 0, 0)),
                  pl.BlockSpec((1, s, LANES), lambda b_, p, i: (b_, 0, p)),
                  pl.BlockSpec((1, s, LANES), lambda b_, p, i: (b_, 0, 0)),
                  pl.BlockSpec((1, s // tk, LANES, tk), lambda b_, p, i: (b_, 0, p, 0))],
        out_specs=pl.BlockSpec((1, tq, LANES), lambda b_, p, i: (b_, i, p)),
        scratch_shapes=[pltpu.VMEM((2, tk, tq), F32), pltpu.VMEM((2, tk, tq), F32),
                        pltpu.VMEM((2, 2 * LANES, tq), BF16), pltpu.VMEM((2, 2 * LANES, tq), BF16),
                        pltpu.VMEM((2, 8, tq), F32), pltpu.VMEM((2, V_ROWS, tq), F32),
                        pltpu.VMEM((tk, tq), jnp.int32)],
        compiler_params=_params(3),
    )(qt, qat, qt, qat, k, ka, vt)


def _cmul(a, t):
    w = a.shape[1] // 2
    ar, ai, tr, ti = a[:, :w], a[:, w:], t[:, :w], t[:, w:]
    return jnp.concatenate([ar * tr - ai * ti, ar * ti + ai * tr], axis=1)


def _s5_kernel(tm, x_ref, mod_ref, g_ref, bm_ref, cm_ref, e_ref, f_ref, a_ref, tri_ref, dskip_ref,
               wglu_ref, o_ref, u_sc, y_sc, s_sc):
    t = pl.program_id(1)
    shift, scale, gate = _mod_rows(mod_ref, 1)
    L = S5_CHUNK
    n_blocks = bm_ref.shape[0]

    @pl.when(t == 0)
    def _():
        s_sc[...] = jnp.zeros_like(s_sc)

    tri = tri_ref[...]
    chunks = [slice(c * L, (c + 1) * L) for c in range(ROW_BLOCK // L)]
    for r0 in range(0, tm, ROW_BLOCK):
        rows = slice(r0, r0 + ROW_BLOCK)
        x = x_ref[0, rows, :]
        u_sc[rows, :] = _adaln(x, g_ref[...], shift, scale)
        for kb in range(n_blocks):
            cols = slice(kb * LANES, (kb + 1) * LANES)
            bu = _dot(u_sc[rows, cols].astype(BF16), bm_ref[kb]).astype(BF16)
            sums = [_dot(tri, _cmul(bu[c], e_ref[kb])) for c in chunks]
            carried = [s_sc[kb:kb + 1, :]]
            for p in sums:
                carried.append(_cmul(p[L - 1:L, :] + carried[-1], a_ref[kb]))
            s_sc[kb:kb + 1, :] = carried[-1]
            xs = jnp.concatenate([_cmul((p + s).astype(BF16), f_ref[kb])
                                  for p, s in zip(sums, carried)], axis=0)
            y_sc[rows, cols] = _dot(xs, cm_ref[kb])
        y = y_sc[rows, :] + dskip_ref[...] * u_sc[rows, :]
        g = jax.nn.gelu(y, approximate=True)
        out = g * jax.nn.sigmoid(_dot(g.astype(BF16), wglu_ref[...]))
        o_ref[0, rows, :] = x + (1.0 + gate) * out


def _s5_tables(lam_re, lam_im, log_dt, b_re, b_im, c_re, c_im):
    g_, n_ = lam_re.shape
    i_ = b_re.shape[2]
    gl = LANES // i_
    nb = g_ // gl
    dt = jnp.exp(log_dt)[:, None]
    ar, ai = lam_re, lam_im
    mag = jnp.exp(ar * dt)
    lb_re, lb_im = mag * jnp.cos(ai * dt), mag * jnp.sin(ai * dt)
    den = ar * ar + ai * ai
    nr, ni = lb_re - 1.0, lb_im
    k_re = (nr * ar + ni * ai) / den
    k_im = (ni * ar - nr * ai) / den
    bb_re = k_re[..., None] * b_re - k_im[..., None] * b_im
    bb_im = k_re[..., None] * b_im + k_im[..., None] * b_re
    eye = jnp.eye(gl, dtype=F32)
    bb = jnp.stack([bb_re, bb_im]).reshape(2, nb, gl, n_, i_)
    bm = jnp.einsum('gh,pkgni->kgiphn', eye, bb).reshape(nb, gl * i_, 2 * gl * n_)
    cc = jnp.stack([c_re, -c_im]).reshape(2, nb, gl, i_, n_)
    cm = jnp.einsum('gh,pkgin->kpgnhi', eye, cc).reshape(nb, 2 * gl * n_, gl * i_)

    def lbar_pow(p, max_log_rate=None):
        log_rate = ar * dt
        if max_log_rate is not None:
            log_rate = jnp.clip(log_rate, -max_log_rate, max_log_rate)
        e = jnp.exp(p[:, None, None] * log_rate[None])
        th = p[:, None, None] * (ai * dt)[None]
        t = jnp.stack([e * jnp.cos(th), e * jnp.sin(th)], axis=1)
        return t.reshape(-1, 2, nb, gl, n_).transpose(2, 0, 1, 3, 4).reshape(nb, -1, 2 * gl * n_)

    mid = S5_CHUNK // 2
    j = jnp.arange(S5_CHUNK, dtype=F32) - mid
    return (bm.astype(BF16), cm.astype(BF16),
            lbar_pow(-j, S5_MAX_LOG_RATE).astype(BF16), lbar_pow(j, S5_MAX_LOG_RATE).astype(BF16),
            lbar_pow(jnp.full((1,), float(S5_CHUNK), F32)))


def _s5_layer(x, mod, gain, lam_re, lam_im, log_dt, b_re, b_im, c_re, c_im, d_skip, w_glu,
              tm=2 * ROW_BLOCK):
    b, s, d = x.shape
    bm, cm, e_tab, f_tab, a_tab = _s5_tables(lam_re, lam_im, log_dt, b_re, b_im, c_re, c_im)
    nb, _, sw = bm.shape
    L = S5_CHUNK
    tri = (jnp.arange(L)[:, None] >= jnp.arange(L)[None, :]).astype(BF16)
    x_spec, mod_spec = _tile_specs(tm, d)
    return pl.pallas_call(
        functools.partial(_s5_kernel, tm),
        out_shape=jax.ShapeDtypeStruct(x.shape, F32),
        grid=(b, s // tm),
        in_specs=[x_spec, mod_spec, _const_spec((1, d)),
                  _const_spec((nb, LANES, sw)), _const_spec((nb, sw, LANES)),
                  _const_spec((nb, L, sw)), _const_spec((nb, L, sw)), _const_spec((nb, 1, sw)),
                  _const_spec((L, L)), _const_spec((1, d)), _const_spec((d, d))],
        out_specs=x_spec,
        scratch_shapes=[pltpu.VMEM((tm, d), F32), pltpu.VMEM((tm, d), F32), pltpu.VMEM((nb, sw), F32)],
        compiler_params=_params(2),
    )(x, mod, gain.reshape(1, d), bm, cm, e_tab, f_tab, a_tab, tri, d_skip.reshape(1, d),
      w_glu.astype(BF16))


def _conv_kernel(tm, x_ref, mod_ref, g_ref, win_ref, cw_ref, wout_ref, o_ref, cz_sc):
    t = pl.program_id(1)
    d = x_ref.shape[2]
    shift, scale, gate = _mod_rows(mod_ref, 1)

    @pl.when(t == 0)
    def _():
        cz_sc[0:CONV_HALO, :] = jnp.zeros((CONV_HALO, d), F32)

    cw = cw_ref[...]
    for r0 in range(0, tm, ROW_BLOCK):
        rows = slice(r0, r0 + ROW_BLOCK)
        x = x_ref[0, rows, :]
        h = _adaln(x, g_ref[...], shift, scale).astype(BF16)
        lo = CONV_HALO + r0
        cz_sc[lo:lo + ROW_BLOCK, :] = _dot(h, win_ref[:, d:2 * d]) * _dot(h, win_ref[:, 2 * d:3 * d])
        conv = (cw[0:1] * cz_sc[lo - 2:lo - 2 + ROW_BLOCK, :]
                + cw[1:2] * cz_sc[lo - 1:lo - 1 + ROW_BLOCK, :]
                + cw[2:3] * cz_sc[lo:lo + ROW_BLOCK, :])
        gated = (_dot(h, win_ref[:, 0:d]) * conv).astype(BF16)
        o_ref[0, rows, :] = x + (1.0 + gate) * _dot(gated, wout_ref[...])
    cz_sc[0:CONV_HALO, :] = cz_sc[tm:tm + CONV_HALO, :]


def _conv_layer(x, mod, gain, w_in, conv_w, w_out, tm=2 * ROW_BLOCK):
    b, s, d = x.shape
    kw = conv_w.shape[0]
    x_spec, mod_spec = _tile_specs(tm, d)
    return pl.pallas_call(
        functools.partial(_conv_kernel, tm),
        out_shape=jax.ShapeDtypeStruct(x.shape, F32),
        grid=(b, s // tm),
        in_specs=[x_spec, mod_spec, _const_spec((1, d)),
                  _const_spec((d, 3 * d)), _const_spec((kw, d)), _const_spec((d, d))],
        out_specs=x_spec,
        scratch_shapes=[pltpu.VMEM((tm + CONV_HALO, d), F32)],
        compiler_params=_params(2),
    )(x, mod, gain.reshape(1, d), w_in.astype(BF16), conv_w.reshape(kw, d), w_out.astype(BF16))


def kernel(x, c, ada_w, ada_b, norm_g, ffn_w_in, ffn_w_out, pool_w, pool_scale, fox_w_in, fox_b_f, fox_q_gain, fox_k_gain, fox_w_o, s5_lam_re, s5_lam_im, s5_log_dt, s5_b_re, s5_b_im, s5_c_re, s5_c_im, s5_d, s5_w_glu, conv_w_in, conv_w, conv_w_out):
    b, s, d = x.shape
    depth = ada_w.shape[0]
    mod_all = _modulation(c, ada_w, ada_b).reshape(depth, b, N_ADA, d)
    n_mixers = 4
    for i in range(depth):
        mod = mod_all[i]
        x = _ffn(x, mod, norm_g[i, 0], ffn_w_in, ffn_w_out, i, 0, sub=0)
        m, r = i % n_mixers, i // n_mixers
        pending = None
        if m == 0:
            x = _pool_layer(x, mod, norm_g[i, 1], pool_w[r], pool_scale[r])
        elif m == 1:
            qt, k, vt, qat, ka = _fox_project(x, mod, norm_g[i, 1], fox_w_in[r], fox_b_f[r],
                                              fox_q_gain[r], fox_k_gain[r])
            pending = (_fox_attend(qt, qat, k, ka, vt), fox_w_o[r].astype(BF16))
        elif m == 2:
            x = _s5_layer(x, mod, norm_g[i, 1], s5_lam_re[r], s5_lam_im[r], s5_log_dt[r],
                          s5_b_re[r], s5_b_im[r], s5_c_re[r], s5_c_im[r], s5_d[r], s5_w_glu[r])
        else:
            x = _conv_layer(x, mod, norm_g[i, 1], conv_w_in[r], conv_w[r], conv_w_out[r])
        x = _ffn(x, mod, norm_g[i, 2], ffn_w_in, ffn_w_out, i, 1, sub=2, mixer_out=pending)
    return x
```

```python
import functools
import math

import jax
import jax.numpy as jnp
from jax import lax
from jax.experimental import pallas as pl
from jax.experimental.pallas import tpu as pltpu

F32 = jnp.float32
BF16 = jnp.bfloat16

NORM_EPS = 1e-6
N_SUBLAYERS = 3
N_ADA = 3 * N_SUBLAYERS
POOL_WINDOWS = (2, 4, 8, 16)
POOL_HALO = 32
FOX_HEADS = 16
FOX_HEAD_DIM = 64
S5_GROUP = 16
S5_STATE = 64
S5_CHUNK = 32
S5_MAX_LOG_RATE = 5.0
CONV_HALO = 8
ROW_BLOCK = 512
LANES = 128
NEG_BIG = -1e30
LOG2E = math.log2(math.e)
AUG_LANES = 8
V_ROWS = FOX_HEAD_DIM + 16
FOX_TILE = 512

VMEM_LIMIT = 56 * 1024 * 1024
FFN_VMEM_LIMIT = 62 * 1024 * 1024


def _params(n_grid, vmem=VMEM_LIMIT):
    return pltpu.CompilerParams(dimension_semantics=("arbitrary",) * n_grid,
                                vmem_limit_bytes=vmem)


def _dot(a, b):
    return jnp.dot(a, b, preferred_element_type=F32)


def _adaln(x, gain, shift, scale):
    ms = jnp.mean(x * x, axis=-1, keepdims=True)
    return x * lax.rsqrt(ms + NORM_EPS) * (gain * (1.0 + scale)) + shift


def _mod_rows(mod_ref, sub):
    m = mod_ref[0]
    return m[3 * sub:3 * sub + 1], m[3 * sub + 1:3 * sub + 2], m[3 * sub + 2:3 * sub + 3]


def _const_spec(shape):
    nd = len(shape)
    return pl.BlockSpec(shape, lambda *_: (0,) * nd, pipeline_mode=pl.Buffered(1))


def _tile_specs(tm, d):
    x_spec = pl.BlockSpec((1, tm, d), lambda b, t: (b, t, 0))
    mod_spec = pl.BlockSpec((1, N_ADA, d), lambda b, t: (b, 0, 0))
    return x_spec, mod_spec


def _mod_kernel(c_ref, w_ref, b_ref, o_ref):
    c = c_ref[...]
    cond = (c * jax.nn.sigmoid(c)).astype(BF16)
    o_ref[0] = _dot(cond, w_ref[0].astype(BF16)) + b_ref[0]


def _modulation(c, ada_w, ada_b):
    depth, d, n = ada_w.shape
    b = c.shape[0]
    tn = 1024
    return pl.pallas_call(
        _mod_kernel,
        out_shape=jax.ShapeDtypeStruct((depth, b, n), F32),
        grid=(depth, n // tn),
        in_specs=[pl.BlockSpec((b, d), lambda l, j: (0, 0)),
                  pl.BlockSpec((1, d, tn), lambda l, j: (l, 0, j)),
                  pl.BlockSpec((1, 1, tn), lambda l, j: (l, 0, j))],
        out_specs=pl.BlockSpec((1, b, tn), lambda l, j: (l, 0, j)),
        compiler_params=_params(2),
    )(c, ada_w, ada_b.reshape(depth, 1, n))


def _ffn_kernel(sub, tf, mixer_out, *refs):
    if mixer_out:
        x_ref, mod_ref, g_ref, win_ref, wout_ref, att_ref, wo_ref, o_ref, h_sc, a_sc = refs
    else:
        x_ref, mod_ref, g_ref, win_ref, wout_ref, o_ref, h_sc, a_sc = refs
    shift, scale, gate = _mod_rows(mod_ref, sub)
    dff = a_sc.shape[1]
    for r0 in range(0, x_ref.shape[1], ROW_BLOCK):
        rows = slice(r0, r0 + ROW_BLOCK)
        x = x_ref[0, rows, :]
        if mixer_out:
            x = x + (1.0 + _mod_rows(mod_ref, 1)[2]) * _dot(att_ref[0, rows, :], wo_ref[...])
        h_sc[rows, :] = _adaln(x, g_ref[...], shift, scale).astype(BF16)
        for j in range(dff // tf):
            h = h_sc[rows, :]
            g = _dot(h, win_ref[:, j * tf:(j + 1) * tf].astype(BF16))
            u = _dot(h, win_ref[:, dff + j * tf:dff + (j + 1) * tf].astype(BF16))
            a_sc[rows, j * tf:(j + 1) * tf] = (g * jax.nn.sigmoid(g) * u).astype(BF16)
        y = _dot(a_sc[rows, :], wout_ref[...].astype(BF16))
        o_ref[0, rows, :] = x + (0.5 * (1.0 + gate)) * y


def _ffn(x, mod, gain, w_in_all, w_out_all, layer, half, sub, mixer_out=None, tm=2 * ROW_BLOCK,
         tf=256):
    b, s, d = x.shape
    dff = w_out_all.shape[2]
    x_spec, mod_spec = _tile_specs(tm, d)

    def pick(rows, cols):
        return pl.BlockSpec((None, None, rows, cols), lambda b_, t: (layer, half, 0, 0),
                            pipeline_mode=pl.Buffered(1))

    operands = [x, mod, gain.reshape(1, d), w_in_all, w_out_all]
    in_specs = [x_spec, mod_spec, _const_spec((1, d)), pick(d, 2 * dff), pick(dff, d)]
    if mixer_out is not None:
        operands += list(mixer_out)
        in_specs += [x_spec, _const_spec((d, d))]
    return pl.pallas_call(
        functools.partial(_ffn_kernel, sub, tf, mixer_out is not None),
        out_shape=jax.ShapeDtypeStruct(x.shape, F32),
        grid=(b, s // tm),
        in_specs=in_specs,
        out_specs=x_spec,
        scratch_shapes=[pltpu.VMEM((tm, d), BF16), pltpu.VMEM((tm, dff), BF16)],
        compiler_params=_params(2, FFN_VMEM_LIMIT),
    )(*operands)


def _pool_kernel(tm, x_ref, mod_ref, g_ref, w_ref, sc_ref, o_ref, h_sc, p_sc):
    t = pl.program_id(1)
    x = x_ref[0]
    d = x.shape[1]
    cg = d // len(POOL_WINDOWS)
    n = tm + POOL_HALO
    shift, scale, gate = _mod_rows(mod_ref, 1)

    @pl.when(t == 0)
    def _():
        h_sc[0:POOL_HALO, :] = jnp.zeros((POOL_HALO, d), F32)

    h_sc[POOL_HALO:, :] = _adaln(x, g_ref[...], shift, scale)
    pos = t * tm + lax.broadcasted_iota(jnp.int32, (tm, 1), 0)
    ys = []
    for gi, w in enumerate(POOL_WINDOWS):
        sl = slice(gi * cg, (gi + 1) * cg)
        cur = h_sc[POOL_HALO:, sl]

        def rows_of(buf):
            if buf is None:
                return lambda lo, hi: h_sc[lo:hi, sl]
            return lambda lo, hi: p_sc[buf, lo:hi, :]

        src, span, level = rows_of(None), 1, 0
        while 2 * span < w:
            level += 1
            lo = 8 * level
            p_sc[level % 2, lo:n, :] = src(lo, n) + src(lo - span, n - span)
            src, span = rows_of(level % 2), 2 * span
        acc = src(POOL_HALO, n) + src(POOL_HALO - span, n - span)
        cnt = jnp.minimum(pos + 1, w).astype(F32)
        pooled = acc / cnt - cur
        ys.append(_dot(pooled.astype(BF16), w_ref[gi]))
    y = jnp.concatenate(ys, axis=1) * sc_ref[...]
    h_sc[0:POOL_HALO, :] = h_sc[tm:tm + POOL_HALO, :]
    o_ref[0] = x + (1.0 + gate) * y


def _pool_layer(x, mod, gain, w_grp, scale, tm=512):
    b, s, d = x.shape
    ng, cg, _ = w_grp.shape
    x_spec, mod_spec = _tile_specs(tm, d)
    return pl.pallas_call(
        functools.partial(_pool_kernel, tm),
        out_shape=jax.ShapeDtypeStruct(x.shape, F32),
        grid=(b, s // tm),
        in_specs=[x_spec, mod_spec, _const_spec((1, d)),
                  _const_spec((ng, cg, cg)), _const_spec((1, d))],
        out_specs=x_spec,
        scratch_shapes=[pltpu.VMEM((tm + POOL_HALO, d), F32), pltpu.VMEM((2, tm + POOL_HALO, cg), F32)],
        compiler_params=_params(2),
    )(x, mod, gain.reshape(1, d), w_grp.astype(BF16), scale.reshape(1, d))


def _split3(v):
    p1 = v.astype(BF16)
    r1 = v - p1.astype(F32)
    p2 = r1.astype(BF16)
    p3 = (r1 - p2.astype(F32)).astype(BF16)
    return p1, p2, p3


def _foxproj_kernel(x_ref, mod_ref, g_ref, wq_ref, wk_ref, wv_ref, wf_ref, bf_ref, qg_ref, kg_ref,
                    tri_ref, place_ref, ones_ref,
                    q_ref, k_ref, v_ref, qa_ref, ka_ref, carry_sc):
    t = pl.program_id(1)
    shift, scale, _ = _mod_rows(mod_ref, 1)
    tm = tri_ref.shape[0]
    nt_dims = (((1,), (1,)), ((), ()))

    def head_normed(xt, gain_ref, hd):
        rows = slice(hd * FOX_HEAD_DIM, (hd + 1) * FOX_HEAD_DIM)
        xh = xt[rows, :]
        r = lax.rsqrt(jnp.mean(xh * xh, axis=0, keepdims=True) + NORM_EPS)
        return (xh * r * gain_ref[rows, :]).astype(BF16)

    @pl.when(t == 0)
    def _():
        carry_sc[...] = jnp.zeros_like(carry_sc)

    tri = tri_ref[...]
    for i in range(x_ref.shape[1] // tm):
        rows = slice(i * tm, (i + 1) * tm)
        h = _adaln(x_ref[0, rows, :], g_ref[...], shift, scale).astype(BF16)
        qt = lax.dot_general(wq_ref[...], h, nt_dims, preferred_element_type=F32)
        kt = lax.dot_general(wk_ref[...], h, nt_dims, preferred_element_type=F32)
        for pair in range(FOX_HEADS // 2):
            lanes = slice(pair * LANES, (pair + 1) * LANES)
            q_ref[0, i, lanes, :] = jnp.concatenate(
                [head_normed(qt, qg_ref, 2 * pair), head_normed(qt, qg_ref, 2 * pair + 1)], axis=0)
            k_ref[0, rows, lanes] = jnp.concatenate(
                [head_normed(kt, kg_ref, 2 * pair), head_normed(kt, kg_ref, 2 * pair + 1)], axis=0).T
        v_ref[0, i] = lax.dot_general(wv_ref[...], h, nt_dims, preferred_element_type=F32).astype(BF16)
        fl = _dot(h, wf_ref[...]) + bf_ref[...]
        logf = jnp.minimum(fl, 0.0) - jnp.log(1.0 + jnp.exp(-jnp.abs(fl)))
        cf = sum(_dot(tri, p) for p in _split3(logf)) + carry_sc[0:1, :]
        carry_sc[0:1, :] = cf[tm - 1:tm, :]
        pieces = jnp.concatenate(_split3(cf * LOG2E), axis=1)
        aug = _dot(pieces, place_ref[...]) + ones_ref[...]
        qa_ref[0, i] = aug[:, :LANES].T.astype(BF16)
        ka_ref[0, rows, :] = aug[:, LANES:].astype(BF16)


def _fox_project(x, mod, gain, w_in, b_f, q_gain, k_gain, tm=FOX_TILE):
    b, s, d = x.shape
    x_spec, mod_spec = _tile_specs(tm, d)
    w = w_in.astype(BF16)
    wf = jnp.zeros((d, LANES), BF16).at[:, :FOX_HEADS].set(w[:, 3 * d:])
    bf = jnp.zeros((1, LANES), F32).at[0, :FOX_HEADS].set(b_f)
    qg = jnp.tile(q_gain, FOX_HEADS).reshape(d, 1) * (FOX_HEAD_DIM ** -0.5 * LOG2E)
    kg = jnp.tile(k_gain, FOX_HEADS).reshape(d, 1)
    tri = (jnp.arange(tm)[:, None] >= jnp.arange(tm)[None, :]).astype(BF16)
    hh, ii = jnp.meshgrid(jnp.arange(FOX_HEADS), jnp.arange(3), indexing="ij")
    place = jnp.zeros((3 * LANES, 2 * LANES), F32)
    place = place.at[ii * LANES + hh, AUG_LANES * hh + 3 + ii].set(1.0)
    place = place.at[ii * LANES + hh, LANES + AUG_LANES * hh + ii].set(-1.0)
    ones = jnp.zeros((1, 2 * LANES), F32)
    ones = ones.at[0, AUG_LANES * hh + ii].set(1.0).at[0, LANES + AUG_LANES * hh + 3 + ii].set(1.0)
    act = jax.ShapeDtypeStruct((b, s, d), BF16)
    aug = jax.ShapeDtypeStruct((b, s, LANES), BF16)
    act_t = jax.ShapeDtypeStruct((b, s // tm, d, tm), BF16)
    aug_t = jax.ShapeDtypeStruct((b, s // tm, LANES, tm), BF16)
    per_step = 2
    token_major = lambda width: pl.BlockSpec((1, per_step * tm, width), lambda b_, t: (b_, t, 0))
    channel_major = lambda rows: pl.BlockSpec((1, per_step, rows, tm), lambda b_, t: (b_, t, 0, 0))
    return pl.pallas_call(
        _foxproj_kernel,
        out_shape=(act_t, act, act_t, aug_t, aug),
        grid=(b, s // (per_step * tm)),
        in_specs=[token_major(d), mod_spec, _const_spec((1, d)),
                  _const_spec((d, d)), _const_spec((d, d)), _const_spec((d, d)),
                  _const_spec((d, LANES)), _const_spec((1, LANES)),
                  _const_spec((d, 1)), _const_spec((d, 1)), _const_spec((tm, tm)),
                  _const_spec((3 * LANES, 2 * LANES)), _const_spec((1, 2 * LANES))],
        out_specs=(channel_major(d), token_major(d), channel_major(d),
                   channel_major(LANES), token_major(LANES)),
        scratch_shapes=[pltpu.VMEM((8, LANES), F32)],
        compiler_params=_params(2),
    )(x, mod, gain.reshape(1, d), w[:, :d].T, w[:, d:2 * d].T, w[:, 2 * d:3 * d].T, wf, bf, qg, kg,
      tri, place.astype(BF16), ones)


def _foxattn_kernel(TK, TQ, NP, q_ref, qa_ref, qn_ref, qan_ref, k_ref, ka_ref, vt_ref, o_ref,
                    sa_all, sb_all, qt_all, qtn_all, m_all, acc_all, diff_sc):
    first_step = (pl.program_id(0) == 0) & (pl.program_id(1) == 0) & (pl.program_id(2) == 0)

    @pl.when(first_step)
    def _():
        diff_sc[...] = (lax.broadcasted_iota(jnp.int32, (TK, TQ), 0)
                        - lax.broadcasted_iota(jnp.int32, (TK, TQ), 1))

    for pp in range(NP):
        _foxattn_pair(TK, TQ, NP * pl.program_id(1) + pp, slice(pp * LANES, (pp + 1) * LANES),
                      q_ref, qa_ref, qn_ref, qan_ref, k_ref, ka_ref, vt_ref, o_ref,
                      sa_all.at[pp], sb_all.at[pp], qt_all.at[pp], qtn_all.at[pp], m_all.at[pp],
                      acc_all.at[pp], diff_sc)


def _foxattn_pair(TK, TQ, pair, lanes, q_ref, qa_ref, qn_ref, qan_ref, k_ref, ka_ref, vt_ref, o_ref,
                  sa_sc, sb_sc, qt_sc, qtn_sc, m_sc, acc_sc, diff_sc):
    qi = pl.program_id(2)
    chan = lax.broadcasted_iota(jnp.int32, (LANES, 1), 0)
    zero = jnp.zeros((LANES, TK), BF16)
    for hd in range(2):
        own = (chan < FOX_HEAD_DIM) == (hd == 0)
        bias_lo = AUG_LANES * (2 * pair + hd)
        own_bias = (chan >= bias_lo) & (chan < bias_lo + AUG_LANES)
        for src, src_a, dst in ((q_ref, qa_ref, qt_sc), (qn_ref, qan_ref, qtn_sc)):
            for i in range(TQ // TK):
                cols = slice(i * TK, (i + 1) * TK)
                dst[hd, 0:LANES, cols] = jnp.where(own, src[0, i, lanes, :], zero)
                dst[hd, LANES:2 * LANES, cols] = jnp.where(own_bias, src_a[0, i], zero)
        m_sc[hd] = jnp.full((8, TQ), NEG_BIG, F32)
        acc_sc[hd] = jnp.zeros((V_ROWS, TQ), F32)

    every = slice(0, TQ)
    early, late = slice(0, TK), slice(TK, TQ)

    def scores(j, s_sc, qs=every, queries=qt_sc):
        rows = pl.ds(pl.multiple_of(j * TK, TK), TK)
        kj = jnp.concatenate([k_ref[0, rows, lanes], ka_ref[0, rows, :]], axis=1)
        for hd in range(2):
            s_sc[hd, :, qs] = _dot(kj, queries[hd, :, qs])

    def absorb(j, s_sc, masked, qs=every, ks=slice(0, TK)):
        vt = vt_ref[0, j, lanes, ks]
        ones = jnp.ones((V_ROWS - FOX_HEAD_DIM, ks.stop - ks.start), BF16)
        if masked:
            visible = diff_sc[ks, qs] <= qi * TQ - j * TK
        for hd in range(2):
            s = s_sc[hd, ks, qs]
            if masked:
                s = jnp.where(visible, s, NEG_BIG)
            m_prev = m_sc[hd, :, qs]
            m_new = jnp.maximum(m_prev, jnp.max(s, axis=0, keepdims=True))
            p = jnp.exp2(s - m_new[0:1, :]).astype(BF16)
            v_aug = jnp.concatenate([vt[hd * FOX_HEAD_DIM:(hd + 1) * FOX_HEAD_DIM, :], ones], axis=0)
            alpha = jnp.exp2(m_prev - m_new)[0:1, :]
            acc_sc[hd, :, qs] = alpha * acc_sc[hd, :, qs] + _dot(v_aug, p)
            m_sc[hd, :, qs] = m_new

    @pl.when(qi == 0)
    def _():
        scores(0, sa_sc)

    def two_tiles(jj):
        j = 2 * jj
        scores(j + 1, sb_sc)
        absorb(j, sa_sc, False)
        scores(j + 2, sa_sc)
        absorb(j + 1, sb_sc, False)

    def four_tiles(jjjj, carry):
        two_tiles(2 * jjjj)
        two_tiles(2 * jjjj + 1)
        return carry

    lax.fori_loop(0, qi // 2, four_tiles, 0)

    @pl.when(qi % 2 == 1)
    def _():
        two_tiles(qi - 1)
    scores(2 * qi + 1, sb_sc, late)
    half = TK // 2
    absorb(2 * qi, sa_sc, True, slice(0, half), slice(0, half))
    absorb(2 * qi, sa_sc, True, slice(half, TK))
    scores(0, sa_sc, early, qtn_sc)
    absorb(2 * qi, sa_sc, False, late)
    scores(0, sa_sc, late, qtn_sc)
    absorb(2 * qi + 1, sb_sc, True, slice(TK, TK + half), slice(0, half))
    absorb(2 * qi + 1, sb_sc, True, slice(TK + half, TQ))

    out_t = jnp.concatenate(
        [acc_sc[hd, 0:FOX_HEAD_DIM, :] / acc_sc[hd, FOX_HEAD_DIM:FOX_HEAD_DIM + 1, :] for hd in range(2)],
        axis=0)
    o_ref[0, :, lanes] = out_t.T.astype(o_ref.dtype)


def _fox_attend(qt, qat, k, ka, vt, tk=FOX_TILE):
    b, s, d = k.shape
    pairs = d // LANES
    tq = 2 * tk
    nq = s // tq
    n_p = 2
    wide = n_p * LANES
    assert qt.shape == vt.shape == (b, s // tk, d, tk)
    return pl.pallas_call(
        functools.partial(_foxattn_kernel, tk, tq, n_p),
        out_shape=jax.ShapeDtypeStruct((b, s, d), BF16),
        grid=(b, pairs // n_p, s // tq),
        in_specs=[pl.BlockSpec((1, tq // tk, wide, tk), lambda b_, p, i: (b_, i, p, 0)),
                  pl.BlockSpec((1, tq // tk, LANES, tk), lambda b_, p, i: (b_, i, 0, 0)),
                  pl.BlockSpec((1, tq // tk, wide, tk), lambda b_, p, i: (b_, jnp.minimum(i + 1, nq - 1), p, 0)),
                  pl.BlockSpec((1, tq // tk, LANES, tk), lambda b_, p, i: (b_, jnp.minimum(i + 1, nq - 1), 0, 0)),
                  pl.BlockSpec((1, s, wide), lambda b_, p, i: (b_, 0, p)),
                  pl.BlockSpec((1, s, LANES), lambda b_, p, i: (b_, 0, 0)),
                  pl.BlockSpec((1, s // tk, wide, tk), lambda b_, p, i: (b_, 0, p, 0))],
        out_specs=pl.BlockSpec((1, tq, wide), lambda b_, p, i: (b_, i, p)),
        scratch_shapes=[pltpu.VMEM((n_p, 2, tk, tq), F32), pltpu.VMEM((n_p, 2, tk, tq), F32),
                        pltpu.VMEM((n_p, 2, 2 * LANES, tq), BF16), pltpu.VMEM((n_p, 2, 2 * LANES, tq), BF16),
                        pltpu.VMEM((n_p, 2, 8, tq), F32), pltpu.VMEM((n_p, 2, V_ROWS, tq), F32),
                        pltpu.VMEM((tk, tq), jnp.int32)],
        compiler_params=_params(3),
    )(qt, qat, qt, qat, k, ka, vt)


def _cmul(a, t):
    w = a.shape[1] // 2
    ar, ai, tr, ti = a[:, :w], a[:, w:], t[:, :w], t[:, w:]
    return jnp.concatenate([ar * tr - ai * ti, ar * ti + ai * tr], axis=1)


def _s5_kernel(tm, x_ref, mod_ref, g_ref, bm_ref, cm_ref, e_ref, f_ref, a_ref, tri_ref, dskip_ref,
               wglu_ref, o_ref, u_sc, y_sc, s_sc):
    t = pl.program_id(1)
    shift, scale, gate = _mod_rows(mod_ref, 1)
    L = S5_CHUNK
    n_blocks = bm_ref.shape[0]

    @pl.when(t == 0)
    def _():
        s_sc[...] = jnp.zeros_like(s_sc)

    tri = tri_ref[...]
    chunks = [slice(c * L, (c + 1) * L) for c in range(ROW_BLOCK // L)]
    for r0 in range(0, tm, ROW_BLOCK):
        rows = slice(r0, r0 + ROW_BLOCK)
        x = x_ref[0, rows, :]
        u_sc[rows, :] = _adaln(x, g_ref[...], shift, scale)
        for kb in range(n_blocks):
            cols = slice(kb * LANES, (kb + 1) * LANES)
            bu = _dot(u_sc[rows, cols].astype(BF16), bm_ref[kb]).astype(BF16)
            sums = [_dot(tri, _cmul(bu[c], e_ref[kb])) for c in chunks]
            carried = [s_sc[kb:kb + 1, :]]
            for p in sums:
                carried.append(_cmul(p[L - 1:L, :] + carried[-1], a_ref[kb]))
            s_sc[kb:kb + 1, :] = carried[-1]
            xs = jnp.concatenate([_cmul((p + s).astype(BF16), f_ref[kb])
                                  for p, s in zip(sums, carried)], axis=0)
            y_sc[rows, cols] = _dot(xs, cm_ref[kb])
        y = y_sc[rows, :] + dskip_ref[...] * u_sc[rows, :]
        g = jax.nn.gelu(y, approximate=True)
        out = g * jax.nn.sigmoid(_dot(g.astype(BF16), wglu_ref[...]))
        o_ref[0, rows, :] = x + (1.0 + gate) * out


def _s5_tables(lam_re, lam_im, log_dt, b_re, b_im, c_re, c_im):
    g_, n_ = lam_re.shape
    i_ = b_re.shape[2]
    gl = LANES // i_
    nb = g_ // gl
    dt = jnp.exp(log_dt)[:, None]
    ar, ai = lam_re, lam_im
    mag = jnp.exp(ar * dt)
    lb_re, lb_im = mag * jnp.cos(ai * dt), mag * jnp.sin(ai * dt)
    den = ar * ar + ai * ai
    nr, ni = lb_re - 1.0, lb_im
    k_re = (nr * ar + ni * ai) / den
    k_im = (ni * ar - nr * ai) / den
    bb_re = k_re[..., None] * b_re - k_im[..., None] * b_im
    bb_im = k_re[..., None] * b_im + k_im[..., None] * b_re
    eye = jnp.eye(gl, dtype=F32)
    bb = jnp.stack([bb_re, bb_im]).reshape(2, nb, gl, n_, i_)
    bm = jnp.einsum('gh,pkgni->kgiphn', eye, bb).reshape(nb, gl * i_, 2 * gl * n_)
    cc = jnp.stack([c_re, -c_im]).reshape(2, nb, gl, i_, n_)
    cm = jnp.einsum('gh,pkgin->kpgnhi', eye, cc).reshape(nb, 2 * gl * n_, gl * i_)

    def lbar_pow(p, max_log_rate=None):
        log_rate = ar * dt
        if max_log_rate is not None:
            log_rate = jnp.clip(log_rate, -max_log_rate, max_log_rate)
        e = jnp.exp(p[:, None, None] * log_rate[None])
        th = p[:, None, None] * (ai * dt)[None]
        t = jnp.stack([e * jnp.cos(th), e * jnp.sin(th)], axis=1)
        return t.reshape(-1, 2, nb, gl, n_).transpose(2, 0, 1, 3, 4).reshape(nb, -1, 2 * gl * n_)

    mid = S5_CHUNK // 2
    j = jnp.arange(S5_CHUNK, dtype=F32) - mid
    return (bm.astype(BF16), cm.astype(BF16),
            lbar_pow(-j, S5_MAX_LOG_RATE).astype(BF16), lbar_pow(j, S5_MAX_LOG_RATE).astype(BF16),
            lbar_pow(jnp.full((1,), float(S5_CHUNK), F32)))


def _s5_layer(x, mod, gain, lam_re, lam_im, log_dt, b_re, b_im, c_re, c_im, d_skip, w_glu,
              tm=2 * ROW_BLOCK):
    b, s, d = x.shape
    bm, cm, e_tab, f_tab, a_tab = _s5_tables(lam_re, lam_im, log_dt, b_re, b_im, c_re, c_im)
    nb, _, sw = bm.shape
    L = S5_CHUNK
    tri = (jnp.arange(L)[:, None] >= jnp.arange(L)[None, :]).astype(BF16)
    x_spec, mod_spec = _tile_specs(tm, d)
    return pl.pallas_call(
        functools.partial(_s5_kernel, tm),
        out_shape=jax.ShapeDtypeStruct(x.shape, F32),
        grid=(b, s // tm),
        in_specs=[x_spec, mod_spec, _const_spec((1, d)),
                  _const_spec((nb, LANES, sw)), _const_spec((nb, sw, LANES)),
                  _const_spec((nb, L, sw)), _const_spec((nb, L, sw)), _const_spec((nb, 1, sw)),
                  _const_spec((L, L)), _const_spec((1, d)), _const_spec((d, d))],
        out_specs=x_spec,
        scratch_shapes=[pltpu.VMEM((tm, d), F32), pltpu.VMEM((tm, d), F32), pltpu.VMEM((nb, sw), F32)],
        compiler_params=_params(2),
    )(x, mod, gain.reshape(1, d), bm, cm, e_tab, f_tab, a_tab, tri, d_skip.reshape(1, d),
      w_glu.astype(BF16))


def _conv_kernel(tm, x_ref, mod_ref, g_ref, win_ref, cw_ref, wout_ref, o_ref, cz_sc):
    t = pl.program_id(1)
    d = x_ref.shape[2]
    shift, scale, gate = _mod_rows(mod_ref, 1)

    @pl.when(t == 0)
    def _():
        cz_sc[0:CONV_HALO, :] = jnp.zeros((CONV_HALO, d), F32)

    cw = cw_ref[...]
    for r0 in range(0, tm, ROW_BLOCK):
        rows = slice(r0, r0 + ROW_BLOCK)
        x = x_ref[0, rows, :]
        h = _adaln(x, g_ref[...], shift, scale).astype(BF16)
        lo = CONV_HALO + r0
        cz_sc[lo:lo + ROW_BLOCK, :] = _dot(h, win_ref[:, d:2 * d]) * _dot(h, win_ref[:, 2 * d:3 * d])
        conv = (cw[0:1] * cz_sc[lo - 2:lo - 2 + ROW_BLOCK, :]
                + cw[1:2] * cz_sc[lo - 1:lo - 1 + ROW_BLOCK, :]
                + cw[2:3] * cz_sc[lo:lo + ROW_BLOCK, :])
        gated = (_dot(h, win_ref[:, 0:d]) * conv).astype(BF16)
        o_ref[0, rows, :] = x + (1.0 + gate) * _dot(gated, wout_ref[...])
    cz_sc[0:CONV_HALO, :] = cz_sc[tm:tm + CONV_HALO, :]


def _conv_layer(x, mod, gain, w_in, conv_w, w_out, tm=2 * ROW_BLOCK):
    b, s, d = x.shape
    kw = conv_w.shape[0]
    x_spec, mod_spec = _tile_specs(tm, d)
    return pl.pallas_call(
        functools.partial(_conv_kernel, tm),
        out_shape=jax.ShapeDtypeStruct(x.shape, F32),
        grid=(b, s // tm),
        in_specs=[x_spec, mod_spec, _const_spec((1, d)),
                  _const_spec((d, 3 * d)), _const_spec((kw, d)), _const_spec((d, d))],
        out_specs=x_spec,
        scratch_shapes=[pltpu.VMEM((tm + CONV_HALO, d), F32)],
        compiler_params=_params(2),
    )(x, mod, gain.reshape(1, d), w_in.astype(BF16), conv_w.reshape(kw, d), w_out.astype(BF16))


def kernel(x, c, ada_w, ada_b, norm_g, ffn_w_in, ffn_w_out, pool_w, pool_scale, fox_w_in, fox_b_f, fox_q_gain, fox_k_gain, fox_w_o, s5_lam_re, s5_lam_im, s5_log_dt, s5_b_re, s5_b_im, s5_c_re, s5_c_im, s5_d, s5_w_glu, conv_w_in, conv_w, conv_w_out):
    b, s, d = x.shape
    depth = ada_w.shape[0]
    mod_all = _modulation(c, ada_w, ada_b).reshape(depth, b, N_ADA, d)
    n_mixers = 4
    for i in range(depth):
        mod = mod_all[i]
        x = _ffn(x, mod, norm_g[i, 0], ffn_w_in, ffn_w_out, i, 0, sub=0)
        m, r = i % n_mixers, i // n_mixers
        pending = None
        if m == 0:
            x = _pool_layer(x, mod, norm_g[i, 1], pool_w[r], pool_scale[r])
        elif m == 1:
            qt, k, vt, qat, ka = _fox_project(x, mod, norm_g[i, 1], fox_w_in[r], fox_b_f[r],
                                              fox_q_gain[r], fox_k_gain[r])
            pending = (_fox_attend(qt, qat, k, ka, vt), fox_w_o[r].astype(BF16))
        elif m == 2:
            x = _s5_layer(x, mod, norm_g[i, 1], s5_lam_re[r], s5_lam_im[r], s5_log_dt[r],
                          s5_b_re[r], s5_b_im[r], s5_c_re[r], s5_c_im[r], s5_d[r], s5_w_glu[r])
        else:
            x = _conv_layer(x, mod, norm_g[i, 1], conv_w_in[r], conv_w[r], conv_w_out[r])
        x = _ffn(x, mod, norm_g[i, 2], ffn_w_in, ffn_w_out, i, 1, sub=2, mixer_out=pending)
    return x
```

```python
import functools
import math

import jax
import jax.numpy as jnp
from jax import lax
from jax.experimental import pallas as pl
from jax.experimental.pallas import tpu as pltpu

F32 = jnp.float32
BF16 = jnp.bfloat16

NORM_EPS = 1e-6
N_SUBLAYERS = 3
N_ADA = 3 * N_SUBLAYERS
POOL_WINDOWS = (2, 4, 8, 16)
POOL_HALO = 32
FOX_HEADS = 16
FOX_HEAD_DIM = 64
S5_GROUP = 16
S5_STATE = 64
S5_CHUNK = 32
S5_MAX_LOG_RATE = 5.0
CONV_HALO = 8
ROW_BLOCK = 512
LANES = 128
NEG_BIG = -1e30
LOG2E = math.log2(math.e)
AUG_LANES = 8
V_ROWS = FOX_HEAD_DIM + 16
FOX_TILE = 512

VMEM_LIMIT = 56 * 1024 * 1024
FFN_VMEM_LIMIT = 62 * 1024 * 1024


def _params(n_grid, vmem=VMEM_LIMIT):
    return pltpu.CompilerParams(dimension_semantics=("arbitrary",) * n_grid,
                                vmem_limit_bytes=vmem)


def _dot(a, b):
    return jnp.dot(a, b, preferred_element_type=F32)


def _adaln(x, gain, shift, scale):
    ms = jnp.mean(x * x, axis=-1, keepdims=True)
    return x * lax.rsqrt(ms + NORM_EPS) * (gain * (1.0 + scale)) + shift


def _mod_rows(mod_ref, sub):
    m = mod_ref[0]
    return m[3 * sub:3 * sub + 1], m[3 * sub + 1:3 * sub + 2], m[3 * sub + 2:3 * sub + 3]


def _const_spec(shape):
    nd = len(shape)
    return pl.BlockSpec(shape, lambda *_: (0,) * nd, pipeline_mode=pl.Buffered(1))


def _tile_specs(tm, d):
    x_spec = pl.BlockSpec((1, tm, d), lambda b, t: (b, t, 0))
    mod_spec = pl.BlockSpec((1, N_ADA, d), lambda b, t: (b, 0, 0))
    return x_spec, mod_spec


def _mod_kernel(c_ref, w_ref, b_ref, o_ref):
    c = c_ref[...]
    cond = (c * jax.nn.sigmoid(c)).astype(BF16)
    o_ref[0] = _dot(cond, w_ref[0].astype(BF16)) + b_ref[0]


def _modulation(c, ada_w, ada_b):
    depth, d, n = ada_w.shape
    b = c.shape[0]
    tn = 1024
    return pl.pallas_call(
        _mod_kernel,
        out_shape=jax.ShapeDtypeStruct((depth, b, n), F32),
        grid=(depth, n // tn),
        in_specs=[pl.BlockSpec((b, d), lambda l, j: (0, 0)),
                  pl.BlockSpec((1, d, tn), lambda l, j: (l, 0, j)),
                  pl.BlockSpec((1, 1, tn), lambda l, j: (l, 0, j))],
        out_specs=pl.BlockSpec((1, b, tn), lambda l, j: (l, 0, j)),
        compiler_params=_params(2),
    )(c, ada_w, ada_b.reshape(depth, 1, n))


def _ffn_kernel(sub, tf, mixer_out, *refs):
    if mixer_out:
        x_ref, mod_ref, g_ref, win_ref, wout_ref, att_ref, wo_ref, o_ref, h_sc, a_sc = refs
    else:
        x_ref, mod_ref, g_ref, win_ref, wout_ref, o_ref, h_sc, a_sc = refs
    shift, scale, gate = _mod_rows(mod_ref, sub)
    dff = a_sc.shape[1]
    for r0 in range(0, x_ref.shape[1], ROW_BLOCK):
        rows = slice(r0, r0 + ROW_BLOCK)
        x = x_ref[0, rows, :]
        if mixer_out:
            x = x + (1.0 + _mod_rows(mod_ref, 1)[2]) * _dot(att_ref[0, rows, :], wo_ref[...])
        h_sc[rows, :] = _adaln(x, g_ref[...], shift, scale).astype(BF16)
        for j in range(dff // tf):
            h = h_sc[rows, :]
            g = _dot(h, win_ref[:, j * tf:(j + 1) * tf].astype(BF16))
            u = _dot(h, win_ref[:, dff + j * tf:dff + (j + 1) * tf].astype(BF16))
            a_sc[rows, j * tf:(j + 1) * tf] = (g * jax.nn.sigmoid(g) * u).astype(BF16)
        y = _dot(a_sc[rows, :], wout_ref[...].astype(BF16))
        o_ref[0, rows, :] = x + (0.5 * (1.0 + gate)) * y


def _ffn(x, mod, gain, w_in_all, w_out_all, layer, half, sub, mixer_out=None, tm=2 * ROW_BLOCK,
         tf=256):
    b, s, d = x.shape
    dff = w_out_all.shape[2]
    x_spec, mod_spec = _tile_specs(tm, d)

    def pick(rows, cols):
        return pl.BlockSpec((None, None, rows, cols), lambda b_, t: (layer, half, 0, 0),
                            pipeline_mode=pl.Buffered(1))

    operands = [x, mod, gain.reshape(1, d), w_in_all, w_out_all]
    in_specs = [x_spec, mod_spec, _const_spec((1, d)), pick(d, 2 * dff), pick(dff, d)]
    if mixer_out is not None:
        operands += list(mixer_out)
        in_specs += [x_spec, _const_spec((d, d))]
    return pl.pallas_call(
        functools.partial(_ffn_kernel, sub, tf, mixer_out is not None),
        out_shape=jax.ShapeDtypeStruct(x.shape, F32),
        grid=(b, s // tm),
        in_specs=in_specs,
        out_specs=x_spec,
        scratch_shapes=[pltpu.VMEM((tm, d), BF16), pltpu.VMEM((tm, dff), BF16)],
        compiler_params=_params(2, FFN_VMEM_LIMIT),
    )(*operands)


def _pool_kernel(tm, x_ref, mod_ref, g_ref, w_ref, sc_ref, o_ref, h_sc, p_sc):
    t = pl.program_id(1)
    x = x_ref[0]
    d = x.shape[1]
    cg = d // len(POOL_WINDOWS)
    n = tm + POOL_HALO
    shift, scale, gate = _mod_rows(mod_ref, 1)

    @pl.when(t == 0)
    def _():
        h_sc[0:POOL_HALO, :] = jnp.zeros((POOL_HALO, d), F32)

    h_sc[POOL_HALO:, :] = _adaln(x, g_ref[...], shift, scale)
    pos = t * tm + lax.broadcasted_iota(jnp.int32, (tm, 1), 0)
    ys = []
    for gi, w in enumerate(POOL_WINDOWS):
        sl = slice(gi * cg, (gi + 1) * cg)
        cur = h_sc[POOL_HALO:, sl]

        def rows_of(buf):
            if buf is None:
                return lambda lo, hi: h_sc[lo:hi, sl]
            return lambda lo, hi: p_sc[buf, lo:hi, :]

        src, span, level = rows_of(None), 1, 0
        while 2 * span < w:
            level += 1
            lo = 8 * level
            p_sc[level % 2, lo:n, :] = src(lo, n) + src(lo - span, n - span)
            src, span = rows_of(level % 2), 2 * span
        acc = src(POOL_HALO, n) + src(POOL_HALO - span, n - span)
        cnt = jnp.minimum(pos + 1, w).astype(F32)
        pooled = acc / cnt - cur
        ys.append(_dot(pooled.astype(BF16), w_ref[gi]))
    y = jnp.concatenate(ys, axis=1) * sc_ref[...]
    h_sc[0:POOL_HALO, :] = h_sc[tm:tm + POOL_HALO, :]
    o_ref[0] = x + (1.0 + gate) * y


def _pool_layer(x, mod, gain, w_grp, scale, tm=512):
    b, s, d = x.shape
    ng, cg, _ = w_grp.shape
    x_spec, mod_spec = _tile_specs(tm, d)
    return pl.pallas_call(
        functools.partial(_pool_kernel, tm),
        out_shape=jax.ShapeDtypeStruct(x.shape, F32),
        grid=(b, s // tm),
        in_specs=[x_spec, mod_spec, _const_spec((1, d)),
                  _const_spec((ng, cg, cg)), _const_spec((1, d))],
        out_specs=x_spec,
        scratch_shapes=[pltpu.VMEM((tm + POOL_HALO, d), F32), pltpu.VMEM((2, tm + POOL_HALO, cg), F32)],
        compiler_params=_params(2),
    )(x, mod, gain.reshape(1, d), w_grp.astype(BF16), scale.reshape(1, d))


def _split3(v):
    p1 = v.astype(BF16)
    r1 = v - p1.astype(F32)
    p2 = r1.astype(BF16)
    p3 = (r1 - p2.astype(F32)).astype(BF16)
    return p1, p2, p3


def _foxproj_kernel(x_ref, mod_ref, g_ref, wq_ref, wk_ref, wv_ref, wf_ref, bf_ref, qg_ref, kg_ref,
                    tri_ref, place_ref, ones_ref,
                    q_ref, k_ref, v_ref, qa_ref, ka_ref, carry_sc):
    t = pl.program_id(1)
    shift, scale, _ = _mod_rows(mod_ref, 1)
    tm = tri_ref.shape[0]
    nt_dims = (((1,), (1,)), ((), ()))

    def head_normed(xt, gain_ref, hd):
        rows = slice(hd * FOX_HEAD_DIM, (hd + 1) * FOX_HEAD_DIM)
        xh = xt[rows, :]
        r = lax.rsqrt(jnp.mean(xh * xh, axis=0, keepdims=True) + NORM_EPS)
        return (xh * r * gain_ref[rows, :]).astype(BF16)

    @pl.when(t == 0)
    def _():
        carry_sc[...] = jnp.zeros_like(carry_sc)

    tri = tri_ref[...]
    for i in range(x_ref.shape[1] // tm):
        rows = slice(i * tm, (i + 1) * tm)
        h = _adaln(x_ref[0, rows, :], g_ref[...], shift, scale).astype(BF16)
        qt = lax.dot_general(wq_ref[...], h, nt_dims, preferred_element_type=F32)
        kt = lax.dot_general(wk_ref[...], h, nt_dims, preferred_element_type=F32)
        for pair in range(FOX_HEADS // 2):
            lanes = slice(pair * LANES, (pair + 1) * LANES)
            q_ref[0, i, lanes, :] = jnp.concatenate(
                [head_normed(qt, qg_ref, 2 * pair), head_normed(qt, qg_ref, 2 * pair + 1)], axis=0)
            k_ref[0, rows, lanes] = jnp.concatenate(
                [head_normed(kt, kg_ref, 2 * pair), head_normed(kt, kg_ref, 2 * pair + 1)], axis=0).T
        v_ref[0, i] = lax.dot_general(wv_ref[...], h, nt_dims, preferred_element_type=F32).astype(BF16)
        fl = _dot(h, wf_ref[...]) + bf_ref[...]
        logf = jnp.minimum(fl, 0.0) - jnp.log(1.0 + jnp.exp(-jnp.abs(fl)))
        cf = sum(_dot(tri, p) for p in _split3(logf)) + carry_sc[0:1, :]
        carry_sc[0:1, :] = cf[tm - 1:tm, :]
        pieces = jnp.concatenate(_split3(cf * LOG2E), axis=1)
        aug = _dot(pieces, place_ref[...]) + ones_ref[...]
        qa_ref[0, i] = aug[:, :LANES].T.astype(BF16)
        ka_ref[0, rows, :] = aug[:, LANES:].astype(BF16)


def _fox_project(x, mod, gain, w_in, b_f, q_gain, k_gain, tm=FOX_TILE):
    b, s, d = x.shape
    x_spec, mod_spec = _tile_specs(tm, d)
    w = w_in.astype(BF16)
    wf = jnp.zeros((d, LANES), BF16).at[:, :FOX_HEADS].set(w[:, 3 * d:])
    bf = jnp.zeros((1, LANES), F32).at[0, :FOX_HEADS].set(b_f)
    qg = jnp.tile(q_gain, FOX_HEADS).reshape(d, 1) * (FOX_HEAD_DIM ** -0.5 * LOG2E)
    kg = jnp.tile(k_gain, FOX_HEADS).reshape(d, 1)
    tri = (jnp.arange(tm)[:, None] >= jnp.arange(tm)[None, :]).astype(BF16)
    hh, ii = jnp.meshgrid(jnp.arange(FOX_HEADS), jnp.arange(3), indexing="ij")
    place = jnp.zeros((3 * LANES, 2 * LANES), F32)
    place = place.at[ii * LANES + hh, AUG_LANES * hh + 3 + ii].set(1.0)
    place = place.at[ii * LANES + hh, LANES + AUG_LANES * hh + ii].set(-1.0)
    ones = jnp.zeros((1, 2 * LANES), F32)
    ones = ones.at[0, AUG_LANES * hh + ii].set(1.0).at[0, LANES + AUG_LANES * hh + 3 + ii].set(1.0)
    act = jax.ShapeDtypeStruct((b, s, d), BF16)
    aug = jax.ShapeDtypeStruct((b, s, LANES), BF16)
    act_t = jax.ShapeDtypeStruct((b, s // tm, d, tm), BF16)
    aug_t = jax.ShapeDtypeStruct((b, s // tm, LANES, tm), BF16)
    per_step = 4
    token_major = lambda width: pl.BlockSpec((1, per_step * tm, width), lambda b_, t: (b_, t, 0))
    channel_major = lambda rows: pl.BlockSpec((1, per_step, rows, tm), lambda b_, t: (b_, t, 0, 0))
    return pl.pallas_call(
        _foxproj_kernel,
        out_shape=(act_t, act, act_t, aug_t, aug),
        grid=(b, s // (per_step * tm)),
        in_specs=[token_major(d), mod_spec, _const_spec((1, d)),
                  _const_spec((d, d)), _const_spec((d, d)), _const_spec((d, d)),
                  _const_spec((d, LANES)), _const_spec((1, LANES)),
                  _const_spec((d, 1)), _const_spec((d, 1)), _const_spec((tm, tm)),
                  _const_spec((3 * LANES, 2 * LANES)), _const_spec((1, 2 * LANES))],
        out_specs=(channel_major(d), token_major(d), channel_major(d),
                   channel_major(LANES), token_major(LANES)),
        scratch_shapes=[pltpu.VMEM((8, LANES), F32)],
        compiler_params=_params(2),
    )(x, mod, gain.reshape(1, d), w[:, :d].T, w[:, d:2 * d].T, w[:, 2 * d:3 * d].T, wf, bf, qg, kg,
      tri, place.astype(BF16), ones)


def _foxattn_kernel(TK, TQ, q_ref, qa_ref, qn_ref, qan_ref, k_ref, ka_ref, vt_ref, o_ref,
                    sa_sc, sb_sc, qt_sc, qtn_sc, m_sc, acc_sc, diff_sc):
    first_step = (pl.program_id(0) == 0) & (pl.program_id(1) == 0) & (pl.program_id(2) == 0)
    pair = pl.program_id(1)
    qi = pl.program_id(2)
    chan = lax.broadcasted_iota(jnp.int32, (LANES, 1), 0)
    zero = jnp.zeros((LANES, TK), BF16)
    for hd in range(2):
        own = (chan < FOX_HEAD_DIM) == (hd == 0)
        bias_lo = AUG_LANES * (2 * pair + hd)
        own_bias = (chan >= bias_lo) & (chan < bias_lo + AUG_LANES)
        for src, src_a, dst in ((q_ref, qa_ref, qt_sc), (qn_ref, qan_ref, qtn_sc)):
            for i in range(TQ // TK):
                cols = slice(i * TK, (i + 1) * TK)
                dst[hd, 0:LANES, cols] = jnp.where(own, src[0, i], zero)
                dst[hd, LANES:2 * LANES, cols] = jnp.where(own_bias, src_a[0, i], zero)
        m_sc[hd] = jnp.full((8, TQ), NEG_BIG, F32)
        acc_sc[hd] = jnp.zeros((V_ROWS, TQ), F32)

    @pl.when(first_step)
    def _():
        diff_sc[...] = (lax.broadcasted_iota(jnp.int32, (TK, TQ), 0)
                        - lax.broadcasted_iota(jnp.int32, (TK, TQ), 1))

    every = slice(0, TQ)
    early, late = slice(0, TK), slice(TK, TQ)

    def scores(j, s_sc, qs=every, queries=qt_sc):
        rows = pl.ds(pl.multiple_of(j * TK, TK), TK)
        kj = jnp.concatenate([k_ref[0, rows, :], ka_ref[0, rows, :]], axis=1)
        for hd in range(2):
            s_sc[hd, :, qs] = _dot(kj, queries[hd, :, qs])

    def absorb(j, s_sc, masked, qs=every, ks=slice(0, TK)):
        vt = vt_ref[0, j, :, ks]
        ones = jnp.ones((V_ROWS - FOX_HEAD_DIM, ks.stop - ks.start), BF16)
        if masked:
            visible = diff_sc[ks, qs] <= qi * TQ - j * TK
        for hd in range(2):
            s = s_sc[hd, ks, qs]
            if masked:
                s = jnp.where(visible, s, NEG_BIG)
            m_prev = m_sc[hd, :, qs]
            m_new = jnp.maximum(m_prev, jnp.max(s, axis=0, keepdims=True))
            p = jnp.exp2(s - m_new[0:1, :]).astype(BF16)
            v_aug = jnp.concatenate([vt[hd * FOX_HEAD_DIM:(hd + 1) * FOX_HEAD_DIM, :], ones], axis=0)
            alpha = jnp.exp2(m_prev - m_new)[0:1, :]
            acc_sc[hd, :, qs] = alpha * acc_sc[hd, :, qs] + _dot(v_aug, p)
            m_sc[hd, :, qs] = m_new

    @pl.when(qi == 0)
    def _():
        scores(0, sa_sc)

    def two_tiles(jj):
        j = 2 * jj
        scores(j + 1, sb_sc)
        absorb(j, sa_sc, False)
        scores(j + 2, sa_sc)
        absorb(j + 1, sb_sc, False)

    def four_tiles(jjjj, carry):
        two_tiles(2 * jjjj)
        two_tiles(2 * jjjj + 1)
        return carry

    lax.fori_loop(0, qi // 2, four_tiles, 0)

    @pl.when(qi % 2 == 1)
    def _():
        two_tiles(qi - 1)
    scores(2 * qi + 1, sb_sc, late)
    half = TK // 2
    absorb(2 * qi, sa_sc, True, slice(0, half), slice(0, half))
    absorb(2 * qi, sa_sc, True, slice(half, TK))
    scores(0, sa_sc, early, qtn_sc)
    absorb(2 * qi, sa_sc, False, late)
    scores(0, sa_sc, late, qtn_sc)
    absorb(2 * qi + 1, sb_sc, True, slice(TK, TK + half), slice(0, half))
    absorb(2 * qi + 1, sb_sc, True, slice(TK + half, TQ))

    out_t = jnp.concatenate(
        [acc_sc[hd, 0:FOX_HEAD_DIM, :] / acc_sc[hd, FOX_HEAD_DIM:FOX_HEAD_DIM + 1, :] for hd in range(2)],
        axis=0)
    o_ref[0] = out_t.T.astype(o_ref.dtype)


def _fox_attend(qt, qat, k, ka, vt, tk=FOX_TILE):
    b, s, d = k.shape
    pairs = d // LANES
    tq = 2 * tk
    nq = s // tq
    assert qt.shape == vt.shape == (b, s // tk, d, tk)
    return pl.pallas_call(
        functools.partial(_foxattn_kernel, tk, tq),
        out_shape=jax.ShapeDtypeStruct((b, s, d), BF16),
        grid=(b, pairs, s // tq),
        in_specs=[pl.BlockSpec((1, tq // tk, LANES, tk), lambda b_, p, i: (b_, i, p, 0)),
                  pl.BlockSpec((1, tq // tk, LANES, tk), lambda b_, p, i: (b_, i, 0, 0)),
                  pl.BlockSpec((1, tq // tk, LANES, tk), lambda b_, p, i: (b_, jnp.minimum(i + 1, nq - 1), p, 0)),
                  pl.BlockSpec((1, tq // tk, LANES, tk), lambda b_, p, i: (b_, jnp.minimum(i + 1, nq - 1), 0, 0)),
                  pl.BlockSpec((1, s, LANES), lambda b_, p, i: (b_, 0, p)),
                  pl.BlockSpec((1, s, LANES), lambda b_, p, i: (b_, 0, 0)),
                  pl.BlockSpec((1, s // tk, LANES, tk), lambda b_, p, i: (b_, 0, p, 0))],
        out_specs=pl.BlockSpec((1, tq, LANES), lambda b_, p, i: (b_, i, p)),
        scratch_shapes=[pltpu.VMEM((2, tk, tq), F32), pltpu.VMEM((2, tk, tq), F32),
                        pltpu.VMEM((2, 2 * LANES, tq), BF16), pltpu.VMEM((2, 2 * LANES, tq), BF16),
                        pltpu.VMEM((2, 8, tq), F32), pltpu.VMEM((2, V_ROWS, tq), F32),
                        pltpu.VMEM((tk, tq), jnp.int32)],
        compiler_params=_params(3),
    )(qt, qat, qt, qat, k, ka, vt)


def _cmul(a, t):
    w = a.shape[1] // 2
    ar, ai, tr, ti = a[:, :w], a[:, w:], t[:, :w], t[:, w:]
    return jnp.concatenate([ar * tr - ai * ti, ar * ti + ai * tr], axis=1)


def _s5_kernel(tm, x_ref, mod_ref, g_ref, bm_ref, cm_ref, e_ref, f_ref, a_ref, tri_ref, dskip_ref,
               wglu_ref, o_ref, u_sc, y_sc, s_sc):
    t = pl.program_id(1)
    shift, scale, gate = _mod_rows(mod_ref, 1)
    L = S5_CHUNK
    n_blocks = bm_ref.shape[0]

    @pl.when(t == 0)
    def _():
        s_sc[...] = jnp.zeros_like(s_sc)

    tri = tri_ref[...]
    chunks = [slice(c * L, (c + 1) * L) for c in range(ROW_BLOCK // L)]
    for r0 in range(0, tm, ROW_BLOCK):
        rows = slice(r0, r0 + ROW_BLOCK)
        x = x_ref[0, rows, :]
        u_sc[rows, :] = _adaln(x, g_ref[...], shift, scale)
        for kb in range(n_blocks):
            cols = slice(kb * LANES, (kb + 1) * LANES)
            bu = _dot(u_sc[rows, cols].astype(BF16), bm_ref[kb]).astype(BF16)
            sums = [_dot(tri, _cmul(bu[c], e_ref[kb])) for c in chunks]
            carried = [s_sc[kb:kb + 1, :]]
            for p in sums:
                carried.append(_cmul(p[L - 1:L, :] + carried[-1], a_ref[kb]))
            s_sc[kb:kb + 1, :] = carried[-1]
            xs = jnp.concatenate([_cmul((p + s).astype(BF16), f_ref[kb])
                                  for p, s in zip(sums, carried)], axis=0)
            y_sc[rows, cols] = _dot(xs, cm_ref[kb])
        y = y_sc[rows, :] + dskip_ref[...] * u_sc[rows, :]
        g = jax.nn.gelu(y, approximate=True)
        out = g * jax.nn.sigmoid(_dot(g.astype(BF16), wglu_ref[...]))
        o_ref[0, rows, :] = x + (1.0 + gate) * out


def _s5_tables(lam_re, lam_im, log_dt, b_re, b_im, c_re, c_im):
    g_, n_ = lam_re.shape
    i_ = b_re.shape[2]
    gl = LANES // i_
    nb = g_ // gl
    dt = jnp.exp(log_dt)[:, None]
    ar, ai = lam_re, lam_im
    mag = jnp.exp(ar * dt)
    lb_re, lb_im = mag * jnp.cos(ai * dt), mag * jnp.sin(ai * dt)
    den = ar * ar + ai * ai
    nr, ni = lb_re - 1.0, lb_im
    k_re = (nr * ar + ni * ai) / den
    k_im = (ni * ar - nr * ai) / den
    bb_re = k_re[..., None] * b_re - k_im[..., None] * b_im
    bb_im = k_re[..., None] * b_im + k_im[..., None] * b_re
    eye = jnp.eye(gl, dtype=F32)
    bb = jnp.stack([bb_re, bb_im]).reshape(2, nb, gl, n_, i_)
    bm = jnp.einsum('gh,pkgni->kgiphn', eye, bb).reshape(nb, gl * i_, 2 * gl * n_)
    cc = jnp.stack([c_re, -c_im]).reshape(2, nb, gl, i_, n_)
    cm = jnp.einsum('gh,pkgin->kpgnhi', eye, cc).reshape(nb, 2 * gl * n_, gl * i_)

    def lbar_pow(p, max_log_rate=None):
        log_rate = ar * dt
        if max_log_rate is not None:
            log_rate = jnp.clip(log_rate, -max_log_rate, max_log_rate)
        e = jnp.exp(p[:, None, None] * log_rate[None])
        th = p[:, None, None] * (ai * dt)[None]
        t = jnp.stack([e * jnp.cos(th), e * jnp.sin(th)], axis=1)
        return t.reshape(-1, 2, nb, gl, n_).transpose(2, 0, 1, 3, 4).reshape(nb, -1, 2 * gl * n_)

    mid = S5_CHUNK // 2
    j = jnp.arange(S5_CHUNK, dtype=F32) - mid
    return (bm.astype(BF16), cm.astype(BF16),
            lbar_pow(-j, S5_MAX_LOG_RATE).astype(BF16), lbar_pow(j, S5_MAX_LOG_RATE).astype(BF16),
            lbar_pow(jnp.full((1,), float(S5_CHUNK), F32)))


def _s5_layer(x, mod, gain, lam_re, lam_im, log_dt, b_re, b_im, c_re, c_im, d_skip, w_glu,
              tm=2 * ROW_BLOCK):
    b, s, d = x.shape
    bm, cm, e_tab, f_tab, a_tab = _s5_tables(lam_re, lam_im, log_dt, b_re, b_im, c_re, c_im)
    nb, _, sw = bm.shape
    L = S5_CHUNK
    tri = (jnp.arange(L)[:, None] >= jnp.arange(L)[None, :]).astype(BF16)
    x_spec, mod_spec = _tile_specs(tm, d)
    return pl.pallas_call(
        functools.partial(_s5_kernel, tm),
        out_shape=jax.ShapeDtypeStruct(x.shape, F32),
        grid=(b, s // tm),
        in_specs=[x_spec, mod_spec, _const_spec((1, d)),
                  _const_spec((nb, LANES, sw)), _const_spec((nb, sw, LANES)),
                  _const_spec((nb, L, sw)), _const_spec((nb, L, sw)), _const_spec((nb, 1, sw)),
                  _const_spec((L, L)), _const_spec((1, d)), _const_spec((d, d))],
        out_specs=x_spec,
        scratch_shapes=[pltpu.VMEM((tm, d), F32), pltpu.VMEM((tm, d), F32), pltpu.VMEM((nb, sw), F32)],
        compiler_params=_params(2),
    )(x, mod, gain.reshape(1, d), bm, cm, e_tab, f_tab, a_tab, tri, d_skip.reshape(1, d),
      w_glu.astype(BF16))


def _conv_kernel(tm, x_ref, mod_ref, g_ref, win_ref, cw_ref, wout_ref, o_ref, cz_sc):
    t = pl.program_id(1)
    d = x_ref.shape[2]
    shift, scale, gate = _mod_rows(mod_ref, 1)

    @pl.when(t == 0)
    def _():
        cz_sc[0:CONV_HALO, :] = jnp.zeros((CONV_HALO, d), F32)

    cw = cw_ref[...]
    for r0 in range(0, tm, ROW_BLOCK):
        rows = slice(r0, r0 + ROW_BLOCK)
        x = x_ref[0, rows, :]
        h = _adaln(x, g_ref[...], shift, scale).astype(BF16)
        lo = CONV_HALO + r0
        cz_sc[lo:lo + ROW_BLOCK, :] = _dot(h, win_ref[:, d:2 * d]) * _dot(h, win_ref[:, 2 * d:3 * d])
        conv = (cw[0:1] * cz_sc[lo - 2:lo - 2 + ROW_BLOCK, :]
                + cw[1:2] * cz_sc[lo - 1:lo - 1 + ROW_BLOCK, :]
                + cw[2:3] * cz_sc[lo:lo + ROW_BLOCK, :])
        gated = (_dot(h, win_ref[:, 0:d]) * conv).astype(BF16)
        o_ref[0, rows, :] = x + (1.0 + gate) * _dot(gated, wout_ref[...])
    cz_sc[0:CONV_HALO, :] = cz_sc[tm:tm + CONV_HALO, :]


def _conv_layer(x, mod, gain, w_in, conv_w, w_out, tm=4 * ROW_BLOCK):
    b, s, d = x.shape
    kw = conv_w.shape[0]
    x_spec, mod_spec = _tile_specs(tm, d)
    return pl.pallas_call(
        functools.partial(_conv_kernel, tm),
        out_shape=jax.ShapeDtypeStruct(x.shape, F32),
        grid=(b, s // tm),
        in_specs=[x_spec, mod_spec, _const_spec((1, d)),
                  _const_spec((d, 3 * d)), _const_spec((kw, d)), _const_spec((d, d))],
        out_specs=x_spec,
        scratch_shapes=[pltpu.VMEM((tm + CONV_HALO, d), F32)],
        compiler_params=_params(2),
    )(x, mod, gain.reshape(1, d), w_in.astype(BF16), conv_w.reshape(kw, d), w_out.astype(BF16))


def kernel(x, c, ada_w, ada_b, norm_g, ffn_w_in, ffn_w_out, pool_w, pool_scale, fox_w_in, fox_b_f, fox_q_gain, fox_k_gain, fox_w_o, s5_lam_re, s5_lam_im, s5_log_dt, s5_b_re, s5_b_im, s5_c_re, s5_c_im, s5_d, s5_w_glu, conv_w_in, conv_w, conv_w_out):
    b, s, d = x.shape
    depth = ada_w.shape[0]
    mod_all = _modulation(c, ada_w, ada_b).reshape(depth, b, N_ADA, d)
    n_mixers = 4
    for i in range(depth):
        mod = mod_all[i]
        x = _ffn(x, mod, norm_g[i, 0], ffn_w_in, ffn_w_out, i, 0, sub=0)
        m, r = i % n_mixers, i // n_mixers
        pending = None
        if m == 0:
            x = _pool_layer(x, mod, norm_g[i, 1], pool_w[r], pool_scale[r])
        elif m == 1:
            qt, k, vt, qat, ka = _fox_project(x, mod, norm_g[i, 1], fox_w_in[r], fox_b_f[r],
                                              fox_q_gain[r], fox_k_gain[r])
            pending = (_fox_attend(qt, qat, k, ka, vt), fox_w_o[r].astype(BF16))
        elif m == 2:
            x = _s5_layer(x, mod, norm_g[i, 1], s5_lam_re[r], s5_lam_im[r], s5_log_dt[r],
                          s5_b_re[r], s5_b_im[r], s5_c_re[r], s5_c_im[r], s5_d[r], s5_w_glu[r])
        else:
            x = _conv_layer(x, mod, norm_g[i, 1], conv_w_in[r], conv_w[r], conv_w_out[r])
        x = _ffn(x, mod, norm_g[i, 2], ffn_w_in, ffn_w_out, i, 1, sub=2, mixer_out=pending)
    return x
```
